```python
import math
import jax, jax.numpy as jnp
from jax import lax
import numpy as np

D_MODEL = 1024
BATCH = 16
SEQ = 256
DEPTH = 1
DEC_BATCH = 8
DEC_SEQ = 1024
PAST_LEN = 256

F32 = jnp.float32
GRID_W = 64
DA_HEADS = 4
DA_HEAD_DIM = 64
DA_WIDTH = DA_HEADS * 2 * DA_HEAD_DIM
ROPE_PAIRS_PER_AXIS = DA_HEAD_DIM // 4
ROPE_BASE = 10000.0
Q_BLOCK = 128
RW_HEADS = 8
RW_HEAD_DIM = 64
RW_WIDTH = RW_HEADS * RW_HEAD_DIM
DECAY_LORA = 64
AAA_LORA = 64
GATE_LORA = 128
RW_COLS = 3 * RW_WIDTH + DECAY_LORA + AAA_LORA + GATE_LORA
RW_LNX_EPS = 64e-5
N_IN = 3 * DA_WIDTH + RW_COLS + 2 * D_MODEL
D_FF = 2816
CONV_W = 3
LN_EPS = 1e-5
ALPHA = (2.0 * DEPTH) ** 0.25
BETA = (8.0 * DEPTH) ** -0.25

kernel_name = 'hybrid_diffattn_birwkv7_prefix_dit_step'


def layer_norm(x, g, b, eps=LN_EPS):
    xf = x.astype(F32)
    mu = jnp.mean(xf, -1, keepdims=True)
    var = jnp.mean(jnp.square(xf - mu), -1, keepdims=True)
    return ((xf - mu) * lax.rsqrt(var + eps) * g + b).astype(x.dtype)


def rms_norm(x, g, eps=LN_EPS):
    xf = x.astype(F32)
    return (xf * lax.rsqrt(jnp.mean(jnp.square(xf), -1, keepdims=True) + eps) * g).astype(x.dtype)


def centred_taps(x):
    xp = jnp.pad(x, ((0, 0), (1, 1), (0, 0)))
    return xp[:, :-2], xp[:, 2:]


def axial_rope_tables(n):
    rows = n // GRID_W
    row = jnp.repeat(jnp.arange(rows, dtype=F32), GRID_W)
    col = jnp.tile(jnp.arange(GRID_W, dtype=F32), rows)
    inv = ROPE_BASE ** (-jnp.arange(ROPE_PAIRS_PER_AXIS, dtype=F32) / ROPE_PAIRS_PER_AXIS)
    ang = jnp.concatenate([row[:, None] * inv, col[:, None] * inv], -1)
    return jnp.cos(ang), jnp.sin(ang)


def apply_rope(x, cos, sin):
    half = x.shape[-1] // 2
    c = cos[None, :, None, None, :]
    s = sin[None, :, None, None, :]
    xf = x.astype(F32)
    x1, x2 = xf[..., :half], xf[..., half:]
    return jnp.concatenate([x1 * c - x2 * s, x2 * c + x1 * s], -1).astype(x.dtype)


def diff_attention(q, k, v, lam):
    B, Tq, H, _, d = q.shape
    nb = Tq // Q_BLOCK
    qb = jnp.moveaxis(q.reshape(B, nb, Q_BLOCK, H, 2, d), 1, 0)
    scale = d ** -0.5

    def block(qblk):
        s = jnp.einsum('bqhmd,bkhmd->bmhqk', qblk, k).astype(F32) * scale
        pr = jax.nn.softmax(s, axis=-1)
        a = pr[:, 0] - lam * pr[:, 1]
        return jnp.einsum('bhqk,bkhe->bqhe', a.astype(v.dtype), v)

    o = lax.map(block, qb)
    return jnp.moveaxis(o, 0, 1).reshape(B, Tq, H, 2 * d)


def rwkv7_scan(s0, r, w, k, v, a, b, reverse):
    xs = tuple(jnp.moveaxis(t.astype(F32), 1, 0) for t in (r, w, k, v, a, b))

    def step(S, inp):
        r_t, w_t, k_t, v_t, a_t, b_t = inp
        sa = jnp.einsum('bhvk,bhk->bhv', S, a_t)
        S = S * w_t[:, :, None, :] + sa[..., None] * b_t[:, :, None, :] + v_t[..., None] * k_t[:, :, None, :]
        return S, jnp.einsum('bhvk,bhk->bhv', S, r_t)

    s_final, ys = lax.scan(step, s0.astype(F32), xs, reverse=reverse)
    return jnp.moveaxis(ys, 0, 1), s_final


def trunk_layer(x, mod, p, layer, ctx=None):
    B, T, _ = x.shape
    sh1, sc1, g1, sh2, sc2, g2 = jnp.split(mod, 6, axis=-1)
    h = x * (1 + sc1) + sh1
    proj = h @ p['w_in']
    q, k, v, rw, gates = jnp.split(proj, [DA_WIDTH, 2 * DA_WIDTH, 3 * DA_WIDTH, 3 * DA_WIDTH + RW_COLS], axis=-1)

    q = q.reshape(B, T, DA_HEADS, 2, DA_HEAD_DIM)
    k = k.reshape(B, T, DA_HEADS, 2, DA_HEAD_DIM)
    v = v.reshape(B, T, DA_HEADS, 2 * DA_HEAD_DIM)
    lam_init = 0.8 - 0.6 * math.exp(-0.3 * layer)
    lq = p['da_lambda'].astype(F32)
    lam = jnp.exp(jnp.sum(lq[0] * lq[1])) - jnp.exp(jnp.sum(lq[2] * lq[3])) + lam_init
    if ctx is None:
        o_att = diff_attention(q, k, v, lam)
    else:
        k_ctx, v_ctx, s_ctx = ctx
        cos, sin = axial_rope_tables(T)
        keys = jnp.concatenate([apply_rope(k, cos, sin), k_ctx.astype(k.dtype)], axis=1)
        vals = jnp.concatenate([v, v_ctx.astype(v.dtype)], axis=1)
        o_att = diff_attention(apply_rope(q, cos, sin), keys, vals, lam)
    o_att = (rms_norm(o_att, p['da_subln_g']) * (1 - lam_init)).reshape(B, T, DA_WIDTH)

    prev, nxt = centred_taps(rw)
    mu = p['rw_mu']
    rw = rw + mu[0] * (prev - rw) + mu[1] * (nxt - rw)
    r, kr, vr, w_lo, a_lo, g_lo = jnp.split(
        rw, [RW_WIDTH, 2 * RW_WIDTH, 3 * RW_WIDTH, 3 * RW_WIDTH + DECAY_LORA, 3 * RW_WIDTH + DECAY_LORA + AAA_LORA], axis=-1)

    def heads(t):
        return t.reshape(B, T, RW_HEADS, RW_HEAD_DIM)

    r_h, v_h = heads(r), heads(vr)
    kk = heads((kr * p['rw_k_k']).astype(F32))
    kk = kk / jnp.maximum(jnp.sqrt(jnp.sum(kk * kk, -1, keepdims=True)), 1e-12)
    ys, bonus, states = [], [], []
    for d in range(2):
        w = -jax.nn.softplus(-(p['rw_w0'][d] + jnp.tanh(w_lo) @ p['rw_w_up'][d])) - 0.5
        decay = jnp.exp(-jnp.exp(w.astype(F32)))
        a = jax.nn.sigmoid(p['rw_a0'][d] + a_lo @ p['rw_a_up'][d])
        k_eff = heads(kr * (1 + (a - 1) * p['rw_k_a']))
        a_h = heads(a).astype(F32)
        if ctx is None:
            s0 = jnp.zeros((B, RW_HEADS, RW_HEAD_DIM, RW_HEAD_DIM), F32)
        else:
            s0 = s_ctx[:, d]
        y_d, s_d = rwkv7_scan(s0, r_h, heads(decay), k_eff, v_h, -kk, kk * a_h, reverse=(d == 1))
        ys.append(y_d)
        states.append(s_d)
        bonus.append(jnp.sum(r_h.astype(F32) * k_eff.astype(F32) * p['rw_r_k'], -1, keepdims=True))
    y_rw = layer_norm(ys[0] + ys[1], p['rw_lnx_g'].reshape(RW_HEADS, RW_HEAD_DIM),
                      p['rw_lnx_b'].reshape(RW_HEADS, RW_HEAD_DIM), eps=RW_LNX_EPS)
    y_rw = (y_rw + (bonus[0] + bonus[1]) * v_h.astype(F32)).reshape(B, T, RW_WIDTH).astype(x.dtype)
    g_rw = jax.nn.sigmoid(g_lo) @ p['rw_g_up']
    o_rw = (y_rw * g_rw) @ p['w_o_rwkv']

    ga, gb = jnp.split(jax.nn.sigmoid(gates), 2, axis=-1)
    mix = (ga * (o_att @ p['w_o_attn']) + gb * o_rw) @ p['w_out']
    x = layer_norm(ALPHA * x + g1 * mix, p['ln1_g'], p['ln1_b'])

    h = x * (1 + sc2) + sh2
    u, val = jnp.split(h @ p['w_up'], 2, axis=-1)
    prev, nxt = centred_taps(u)
    cw = p['conv_w']
    u = prev * cw[0] + u * cw[1] + nxt * cw[2] + p['conv_b']
    f = (jax.nn.gelu(u) * val) @ p['w_down']
    x = layer_norm(ALPHA * x + g2 * f, p['ln2_g'], p['ln2_b'])
    new_ctx = (k, v, jnp.stack(states, axis=1)) if ctx is None else None
    return x, new_ctx


def setup_inputs(seed: int = 0) -> dict:
    key = jax.random.key(seed)
    ks = iter(jax.random.split(key, 48))

    def nrm(shape, scale):
        return scale * jax.random.normal(next(ks), shape, F32)

    L = DEPTH
    D = D_MODEL
    return {
        'x_prompt': nrm((BATCH, SEQ, D), 1.0),
        'x_sample': nrm((DEC_BATCH, DEC_SEQ, D), 1.0),
        'cache_k': nrm((DEC_BATCH, L, PAST_LEN, DA_HEADS, 2, DA_HEAD_DIM), 1.0),
        'cache_v': nrm((DEC_BATCH, L, PAST_LEN, DA_HEADS, 2 * DA_HEAD_DIM), 1.0),
        'state_rwkv': nrm((DEC_BATCH, L, 2, RW_HEADS, RW_HEAD_DIM, RW_HEAD_DIM), 0.1),
        'c': nrm((DEC_BATCH, D), 1.0),
        'c_ctx': nrm((D,), 1.0),
        'w_ada': nrm((L, D, 6 * D), D ** -0.5),
        'b_ada': nrm((L, 6 * D), 0.02),
        'w_in': nrm((L, D, N_IN), D ** -0.5),
        'rw_mu': jax.random.uniform(next(ks), (L, 2, RW_COLS), F32, 0.0, 0.5),
        'rw_w0': nrm((L, 2, RW_WIDTH), 0.5),
        'rw_w_up': nrm((L, 2, DECAY_LORA, RW_WIDTH), 0.5 * DECAY_LORA ** -0.5),
        'rw_a0': nrm((L, 2, RW_WIDTH), 0.3),
        'rw_a_up': nrm((L, 2, AAA_LORA, RW_WIDTH), AAA_LORA ** -0.5),
        'rw_g_up': nrm((L, GATE_LORA, RW_WIDTH), GATE_LORA ** -0.5),
        'rw_k_k': 0.85 + nrm((L, RW_WIDTH), 0.05),
        'rw_k_a': 1.0 + nrm((L, RW_WIDTH), 0.05),
        'rw_r_k': nrm((L, RW_HEADS, RW_HEAD_DIM), 0.1),
        'rw_lnx_g': 1.0 + nrm((L, RW_WIDTH), 0.05),
        'rw_lnx_b': nrm((L, RW_WIDTH), 0.02),
        'da_lambda': nrm((L, 4, DA_HEAD_DIM), 0.1),
        'da_subln_g': 1.0 + nrm((L, 2 * DA_HEAD_DIM), 0.05),
        'w_o_attn': nrm((L, DA_WIDTH, D), DA_WIDTH ** -0.5),
        'w_o_rwkv': nrm((L, RW_WIDTH, D), RW_WIDTH ** -0.5),
        'w_out': nrm((L, D, D), BETA * D ** -0.5),
        'ln1_g': 1.0 + nrm((L, D), 0.05),
        'ln1_b': nrm((L, D), 0.02),
        'w_up': nrm((L, D, 2 * D_FF), D ** -0.5),
        'conv_w': nrm((L, CONV_W, D_FF), CONV_W ** -0.5),
        'conv_b': nrm((L, D_FF), 0.02),
        'w_down': nrm((L, D_FF, D), BETA * D_FF ** -0.5),
        'ln2_g': 1.0 + nrm((L, D), 0.05),
        'ln2_b': nrm((L, D), 0.02),
    }


def reference(x_prompt, x_sample, cache_k, cache_v, state_rwkv, c, c_ctx, w_ada, b_ada, w_in,
              rw_mu, rw_w0, rw_w_up, rw_a0, rw_a_up, rw_g_up, rw_k_k, rw_k_a, rw_r_k,
              rw_lnx_g, rw_lnx_b, da_lambda, da_subln_g, w_o_attn, w_o_rwkv, w_out,
              ln1_g, ln1_b, w_up, conv_w, conv_b, w_down, ln2_g, ln2_b):
    y_prompt = x_prompt
    y_sample = x_sample
    new_k, new_v, new_s = [], [], []
    for l in range(DEPTH):
        p = {
            'w_in': w_in[l], 'rw_mu': rw_mu[l], 'rw_w0': rw_w0[l], 'rw_w_up': rw_w_up[l],
            'rw_a0': rw_a0[l], 'rw_a_up': rw_a_up[l], 'rw_g_up': rw_g_up[l], 'rw_k_k': rw_k_k[l],
            'rw_k_a': rw_k_a[l], 'rw_r_k': rw_r_k[l], 'rw_lnx_g': rw_lnx_g[l], 'rw_lnx_b': rw_lnx_b[l],
            'da_lambda': da_lambda[l], 'da_subln_g': da_subln_g[l], 'w_o_attn': w_o_attn[l],
            'w_o_rwkv': w_o_rwkv[l], 'w_out': w_out[l], 'ln1_g': ln1_g[l], 'ln1_b': ln1_b[l],
            'w_up': w_up[l], 'conv_w': conv_w[l], 'conv_b': conv_b[l], 'w_down': w_down[l],
            'ln2_g': ln2_g[l], 'ln2_b': ln2_b[l],
        }
        mod_ctx = (jax.nn.silu(c_ctx) @ w_ada[l] + b_ada[l])[None, None, :]
        mod_lat = (jax.nn.silu(c) @ w_ada[l] + b_ada[l])[:, None, :]
        y_prompt, ctx_l = trunk_layer(y_prompt, mod_ctx, p, l)
        new_k.append(ctx_l[0])
        new_v.append(ctx_l[1])
        new_s.append(ctx_l[2])
        y_sample, _ = trunk_layer(y_sample, mod_lat, p, l,
                                  ctx=(cache_k[:, l], cache_v[:, l], state_rwkv[:, l]))
    return (y_prompt, y_sample, jnp.stack(new_k, axis=1), jnp.stack(new_v, axis=1), jnp.stack(new_s, axis=1))
```

```python
import functools
import math

import jax
import jax.numpy as jnp
from jax import lax
from jax.experimental import pallas as pl
from jax.experimental.pallas import tpu as pltpu

F32 = jnp.float32
BF16 = jnp.bfloat16
HIGHEST = lax.Precision.HIGHEST

D_MODEL = 1024
GRID_W = 64
DA_HEADS = 4
DA_HEAD_DIM = 64
DA_WIDTH = DA_HEADS * 2 * DA_HEAD_DIM
ROPE_PAIRS_PER_AXIS = DA_HEAD_DIM // 4
ROPE_BASE = 10000.0
RW_HEADS = 8
RW_HEAD_DIM = 64
RW_WIDTH = RW_HEADS * RW_HEAD_DIM
DECAY_LORA = 64
AAA_LORA = 64
GATE_LORA = 128
RW_COLS = 3 * RW_WIDTH + DECAY_LORA + AAA_LORA + GATE_LORA
RW_LNX_EPS = 64e-5
QKVR_COLS = 3 * DA_WIDTH + RW_COLS
D_FF = 2816
LN_EPS = 1e-5
DEPTH = 1
ALPHA = (2.0 * DEPTH) ** 0.25

CHUNK = 64
ATTN_Q_BLOCK = 256
ROW_TILE = 512
MLP_ROW_TILE = 1024
FF_TILE = 256
MOD_COL_TILE = 768
MOD_ROWS = 16
VMEM_LIMIT = 48 * 1024 * 1024

_NN = (((1,), (0,)), ((), ()))
_NT = (((1,), (1,)), ((), ()))
_TN = (((0,), (0,)), ((), ()))


def _dot(a, b, dims=_NN, precision=None):
    return lax.dot_general(a, b, dims, precision=precision, preferred_element_type=F32)


def _layer_norm(z, g, b, eps):
    mu = jnp.mean(z, axis=-1, keepdims=True)
    zc = z - mu
    var = jnp.mean(zc * zc, axis=-1, keepdims=True)
    return zc * lax.rsqrt(var + eps) * g + b


def _mod_kernel(c_ref, w_ref, b_ref, o_ref):
    cv = c_ref[...]
    s = cv * jax.nn.sigmoid(cv)
    o_ref[...] = _dot(s, w_ref[...], precision=HIGHEST) + b_ref[...]


def _modulation(cvec, w_ada, b_ada):
    n = w_ada.shape[1]
    return pl.pallas_call(
        _mod_kernel,
        grid=(n // MOD_COL_TILE,),
        in_specs=[
            pl.BlockSpec((MOD_ROWS, D_MODEL), lambda j: (0, 0)),
            pl.BlockSpec((D_MODEL, MOD_COL_TILE), lambda j: (0, j)),
            pl.BlockSpec((1, MOD_COL_TILE), lambda j: (0, j)),
        ],
        out_specs=pl.BlockSpec((MOD_ROWS, MOD_COL_TILE), lambda j: (0, j)),
        out_shape=jax.ShapeDtypeStruct((MOD_ROWS, n), F32),
        compiler_params=pltpu.CompilerParams(
            dimension_semantics=("parallel",), vmem_limit_bytes=VMEM_LIMIT),
        name="mod",
    )(cvec, w_ada, b_ada)


def _mod_row_map(rows_per_batch, fixed_row):
    if fixed_row is not None:
        return lambda i: (fixed_row, 0, 0)
    tiles = rows_per_batch // ROW_TILE
    return lambda i: (i // tiles, 0, 0)


def _inproj_kernel(x_ref, mod_ref, w_ref, q_ref, k_ref, v_ref, rw_ref):
    mod = mod_ref[...]
    sh1 = mod[:, 0:D_MODEL]
    sc1 = mod[:, D_MODEL:2 * D_MODEL]
    h = (x_ref[...] * (1.0 + sc1) + sh1).astype(BF16)
    q_ref[...] = _dot(h, w_ref[:, 0:DA_WIDTH])
    k_ref[...] = _dot(h, w_ref[:, DA_WIDTH:2 * DA_WIDTH])
    v_ref[...] = _dot(h, w_ref[:, 2 * DA_WIDTH:3 * DA_WIDTH])
    rw_ref[...] = _dot(h, w_ref[:, 3 * DA_WIDTH:QKVR_COLS])


def _input_projection(x2d, mod3, w_qkvr, seq_len, fixed_row):
    m = x2d.shape[0]
    row = lambda i: (i, 0)
    return pl.pallas_call(
        _inproj_kernel,
        grid=(m // ROW_TILE,),
        in_specs=[
            pl.BlockSpec((ROW_TILE, D_MODEL), row),
            pl.BlockSpec((None, 1, 6 * D_MODEL), _mod_row_map(seq_len, fixed_row)),
            pl.BlockSpec((D_MODEL, QKVR_COLS), lambda i: (0, 0)),
        ],
        out_specs=[
            pl.BlockSpec((ROW_TILE, DA_WIDTH), row),
            pl.BlockSpec((ROW_TILE, DA_WIDTH), row),
            pl.BlockSpec((ROW_TILE, DA_WIDTH), row),
            pl.BlockSpec((ROW_TILE, RW_COLS), row),
        ],
        out_shape=[
            jax.ShapeDtypeStruct((m, DA_WIDTH), F32),
            jax.ShapeDtypeStruct((m, DA_WIDTH), F32),
            jax.ShapeDtypeStruct((m, DA_WIDTH), F32),
            jax.ShapeDtypeStruct((m, RW_COLS), F32),
        ],
        compiler_params=pltpu.CompilerParams(
            dimension_semantics=("parallel",), vmem_limit_bytes=VMEM_LIMIT),
        name="inproj",
    )(x2d, mod3, w_qkvr)


def _rope(x, cos, sin_signed):
    lane = lax.broadcasted_iota(jnp.int32, x.shape, 1)
    partner = jnp.where((lane & 63) < 32, pltpu.roll(x, 96, 1), pltpu.roll(x, 32, 1))
    return x * cos + partner * sin_signed


def _softmax_av(qm, keys, vals):
    scores = [_dot(qm, kg, _NT) for kg in keys]
    mx = scores[0].max(axis=-1, keepdims=True)
    for s in scores[1:]:
        mx = jnp.maximum(mx, s.max(axis=-1, keepdims=True))
    den = None
    out = None
    for s, vg in zip(scores, vals):
        p = jnp.exp(s - mx)
        d = jnp.sum(p, axis=-1, keepdims=True)
        o = _dot(p.astype(BF16), vg)
        den = d if den is None else den + d
        out = o if out is None else out + o
    return out / den


def _attn_kernel(*refs, has_ctx, seq_len, lam_init):
    if has_ctx:
        q_ref, k_ref, v_ref, kc_ref, vc_ref, cos_ref, sin_ref, lq_ref, g_ref, o_ref = refs
    else:
        q_ref, k_ref, v_ref, lq_ref, g_ref, o_ref = refs
    d = DA_HEAD_DIM
    lq = lq_ref[...]
    lam = (jnp.exp(jnp.sum(lq[0:1] * lq[1:2], axis=-1, keepdims=True))
           - jnp.exp(jnp.sum(lq[2:3] * lq[3:4], axis=-1, keepdims=True)) + lam_init)
    k = k_ref[...]
    if has_ctx:
        k = _rope(k, cos_ref[...], sin_ref[...])
    keys = [[k[:, m * d:(m + 1) * d].astype(BF16)] for m in range(2)]
    vals = [v_ref[...].astype(BF16)]
    if has_ctx:
        kc = kc_ref[...]
        for m in range(2):
            keys[m].append(kc[:, m * d:(m + 1) * d].astype(BF16))
        vals.append(vc_ref[...].astype(BF16))
    g = g_ref[...]
    for qb in range(seq_len // ATTN_Q_BLOCK):
        rows = slice(qb * ATTN_Q_BLOCK, (qb + 1) * ATTN_Q_BLOCK)
        q = q_ref[rows, :]
        if has_ctx:
            q = _rope(q, cos_ref[rows, :], sin_ref[rows, :])
        q = q * (d ** -0.5)
        o1 = _softmax_av(q[:, 0:d].astype(BF16), keys[0], vals)
        o2 = _softmax_av(q[:, d:2 * d].astype(BF16), keys[1], vals)
        o = o1 - lam * o2
        ms = jnp.mean(o * o, axis=-1, keepdims=True)
        o_ref[rows, :] = o * lax.rsqrt(ms + LN_EPS) * g * (1.0 - lam_init)


def _rope_tables(n):
    rows = n // GRID_W
    row = jnp.repeat(jnp.arange(rows, dtype=F32), GRID_W)
    col = jnp.tile(jnp.arange(GRID_W, dtype=F32), rows)
    inv = ROPE_BASE ** (-jnp.arange(ROPE_PAIRS_PER_AXIS, dtype=F32) / ROPE_PAIRS_PER_AXIS)
    ang = jnp.concatenate([row[:, None] * inv, col[:, None] * inv], -1)
    cos, sin = jnp.cos(ang), jnp.sin(ang)
    return jnp.tile(cos, (1, 4)), jnp.tile(jnp.concatenate([-sin, sin], -1), (1, 2))


def _attention(q2d, k2d, v2d, da_lambda, subln_g, batch, seq_len, layer, ctx=None):
    has_ctx = ctx is not None
    w = 2 * DA_HEAD_DIM
    head = lambda b, h: (b, h)
    const = lambda b, h: (0, 0)
    in_specs = [pl.BlockSpec((seq_len, w), head)] * 3
    args = [q2d, k2d, v2d]
    if has_ctx:
        kc, vc = ctx
        past = kc.shape[1]
        in_specs += [pl.BlockSpec((None, past, w), lambda b, h: (b, 0, h))] * 2
        in_specs += [pl.BlockSpec((seq_len, w), const)] * 2
        args += [kc, vc, *_rope_tables(seq_len)]
    in_specs += [pl.BlockSpec((4, DA_HEAD_DIM), const), pl.BlockSpec((1, w), const)]
    args += [da_lambda, subln_g]
    lam_init = 0.8 - 0.6 * math.exp(-0.3 * layer)
    return pl.pallas_call(
        functools.partial(_attn_kernel, has_ctx=has_ctx, seq_len=seq_len, lam_init=lam_init),
        grid=(batch, DA_HEADS),
        in_specs=in_specs,
        out_specs=pl.BlockSpec((seq_len, w), head),
        out_shape=jax.ShapeDtypeStruct((batch * seq_len, DA_WIDTH), F32),
        compiler_params=pltpu.CompilerParams(
            dimension_semantics=("parallel", "parallel"), vmem_limit_bytes=VMEM_LIMIT),
        name="attn",
    )(*args)


_M_STRICT = (0, 2)
_M_INCL = (1, 3)
_M_BLOCK8, _M_OFF16, _M_OFF32, _M_OFF64, _M_EYE = 4, 5, 6, 7, 8


def _chunk_masks():
    t = jnp.arange(CHUNK)[:, None]
    s = jnp.arange(CHUNK)[None, :]
    same = lambda n: (t // n) == (s // n)
    masks = [t > s, t >= s, t < s, t <= s,
             same(8), same(16) & ~same(8), same(32) & ~same(16), ~same(32), t == s]
    return jnp.stack(masks).astype(F32)


def _head_sum_matrix():
    i = jnp.arange(RW_WIDTH)
    return ((i[:, None] // RW_HEAD_DIM) == (i[None, :] // RW_HEAD_DIM)).astype(F32)


def _unit_triangular_inverse(a, m_ref):
    hi = functools.partial(_dot, precision=HIGHEST)
    a8 = a * m_ref[_M_BLOCK8]
    a2 = hi(a8, a8)
    a4 = hi(a2, a2)
    p = m_ref[_M_EYE] + a8
    p = p + hi(p, a2)
    p = p + hi(p, a4)
    for off in (_M_OFF16, _M_OFF32, _M_OFF64):
        p = p + hi(hi(p, a * m_ref[off]), p)
    return p


def _rwkv_kernel(*refs, seq_len, has_state_in, has_state_out):
    refs = list(refs)
    rw_ref = refs.pop(0)
    s0_ref = refs.pop(0) if has_state_in else None
    (mu_ref, w0_ref, wup_ref, a0_ref, aup_ref, gup_ref, kk_ref, ka_ref, rk_ref,
     lng_ref, lnb_ref, e_ref, m_ref) = refs[:13]
    refs = refs[13:]
    o_ref = refs.pop(0)
    so_ref = refs.pop(0) if has_state_out else None
    y_scr, bv_scr, gr_scr, s_scr = refs

    C = CHUNK
    N = RW_HEAD_DIM
    W = RW_WIDTH
    nc = seq_len // C
    hi = functools.partial(_dot, precision=HIGHEST)

    if has_state_in:
        s_scr[...] = s0_ref[...]
    else:
        s_scr[...] = jnp.zeros_like(s_scr)

    def chunk_pass(d, c):
        r0 = pl.multiple_of(c * C, C)
        rows = pl.ds(r0, C)
        cur = rw_ref[rows, :]
        prow = rw_ref[pl.ds(jnp.maximum(r0 - 1, 0), 1), :] * jnp.where(r0 > 0, 1.0, 0.0)
        nrow = rw_ref[pl.ds(jnp.minimum(r0 + C, seq_len - 1), 1), :] * jnp.where(r0 + C < seq_len, 1.0, 0.0)
        rid = lax.broadcasted_iota(jnp.int32, (C, 1), 0)
        prev = jnp.where(rid == 0, prow, pltpu.roll(cur, 1, 0))
        nxt = jnp.where(rid == C - 1, nrow, pltpu.roll(cur, C - 1, 0))
        xm = cur + mu_ref[0:1, :] * (prev - cur) + mu_ref[1:2, :] * (nxt - cur)

        r = xm[:, 0:W]
        kr = xm[:, W:2 * W]
        vr = xm[:, 2 * W:3 * W]
        w_lo = xm[:, 3 * W:3 * W + DECAY_LORA]
        a_lo = xm[:, 3 * W + DECAY_LORA:3 * W + DECAY_LORA + AAA_LORA]
        g_lo = xm[:, 3 * W + DECAY_LORA + AAA_LORA:RW_COLS]

        kk = kr * kk_ref[...]
        kk = kk / jnp.maximum(jnp.sqrt(hi(kk * kk, e_ref[...])), 1e-12)
        wlog = -jax.nn.softplus(-(w0_ref[d:d + 1, :] + hi(jnp.tanh(w_lo), wup_ref[d]))) - 0.5
        log_decay = -jnp.exp(wlog)
        a = jax.nn.sigmoid(a0_ref[d:d + 1, :] + hi(a_lo, aup_ref[d]))
        keff = kr * (1.0 + (a - 1.0) * ka_ref[...])
        bonus = hi(r * keff * rk_ref[...], e_ref[...])
        bv_scr[d, rows, :] = bonus * vr
        if d == 0:
            gr_scr[rows, :] = _dot(jax.nn.sigmoid(g_lo).astype(BF16), gup_ref[...])

        cum = hi(m_ref[_M_INCL[d]], log_decay)
        g_in = jnp.exp(cum)
        g_inv = jnp.exp(-cum)
        a_t = (-kk * jnp.exp(cum - log_decay)).astype(BF16)
        r_t = (r * g_in).astype(BF16)
        b_t = (kk * a * g_inv).astype(BF16)
        k_t = (keff * g_inv).astype(BF16)
        v_b = vr.astype(BF16)
        g_end = g_in[C - 1:C, :] if d == 0 else g_in[0:1, :]
        strict = m_ref[_M_STRICT[d]]
        incl = m_ref[_M_INCL[d]]

        ys = []
        for h in range(RW_HEADS):
            sl = slice(h * N, (h + 1) * N)
            s_old = s_scr[d, h]
            ar = jnp.concatenate([a_t[:, sl], r_t[:, sl]], axis=0)
            xb = _dot(ar, b_t[:, sl], _NT)
            xk = _dot(ar, k_t[:, sl], _NT)
            xs = _dot(ar, s_old.astype(BF16), _NT)
            vh = v_b[:, sl]
            t_inv = _unit_triangular_inverse(xb[0:C] * strict, m_ref)
            rhs = xs[0:C] + _dot((xk[0:C] * strict).astype(BF16), vh)
            u = hi(t_inv, rhs)
            ub = u.astype(BF16)
            ys.append(xs[C:2 * C]
                      + _dot((xb[C:2 * C] * incl).astype(BF16), ub)
                      + _dot((xk[C:2 * C] * incl).astype(BF16), vh))
            s_new = s_old + _dot(ub, b_t[:, sl], _TN) + _dot(vh, k_t[:, sl], _TN)
            s_scr[d, h] = s_new * g_end[:, sl]
        y_scr[d, rows, :] = jnp.concatenate(ys, axis=-1)

    def scan_body(i, carry):
        chunk_pass(0, i)
        chunk_pass(1, nc - 1 - i)
        return carry

    lax.fori_loop(0, nc, scan_body, 0)

    def finish_body(i, carry):
        rows = pl.ds(pl.multiple_of(i * C, C), C)
        y = y_scr[0, rows, :] + y_scr[1, rows, :]
        yc = y - hi(y, e_ref[...]) * (1.0 / N)
        var = hi(yc * yc, e_ref[...]) * (1.0 / N)
        yn = yc * lax.rsqrt(var + RW_LNX_EPS) * lng_ref[...] + lnb_ref[...]
        o_ref[rows, :] = (yn + bv_scr[0, rows, :] + bv_scr[1, rows, :]) * gr_scr[rows, :]
        return carry

    lax.fori_loop(0, nc, finish_body, 0)
    if has_state_out:
        so_ref[...] = s_scr[...]


def _rwkv(rw2d, p, batch, seq_len, state_in=None, want_state=False):
    has_state_in = state_in is not None
    const2 = lambda b: (0, 0)
    const3 = lambda b: (0, 0, 0)
    state_spec = pl.BlockSpec((None, 2, RW_HEADS, RW_HEAD_DIM, RW_HEAD_DIM), lambda b: (b, 0, 0, 0, 0))
    in_specs = [pl.BlockSpec((seq_len, RW_COLS), lambda b: (b, 0))]
    args = [rw2d]
    if has_state_in:
        in_specs.append(state_spec)
        args.append(state_in)
    in_specs += [
        pl.BlockSpec((2, RW_COLS), const2),
        pl.BlockSpec((2, RW_WIDTH), const2),
        pl.BlockSpec((2, DECAY_LORA, RW_WIDTH), const3),
        pl.BlockSpec((2, RW_WIDTH), const2),
        pl.BlockSpec((2, AAA_LORA, RW_WIDTH), const3),
        pl.BlockSpec((GATE_LORA, RW_WIDTH), const2),
        pl.BlockSpec((1, RW_WIDTH), const2),
        pl.BlockSpec((1, RW_WIDTH), const2),
        pl.BlockSpec((1, RW_WIDTH), const2),
        pl.BlockSpec((1, RW_WIDTH), const2),
        pl.BlockSpec((1, RW_WIDTH), const2),
        pl.BlockSpec((RW_WIDTH, RW_WIDTH), const2),
        pl.BlockSpec((9, CHUNK, CHUNK), const3),
    ]
    args += [p['rw_mu'], p['rw_w0'], p['rw_w_up'], p['rw_a0'], p['rw_a_up'],
             p['rw_g_up'].astype(BF16), p['rw_k_k'][None], p['rw_k_a'][None],
             p['rw_r_k'].reshape(1, RW_WIDTH), p['rw_lnx_g'][None], p['rw_lnx_b'][None],
             _head_sum_matrix(), _chunk_masks()]
    out_specs = [pl.BlockSpec((seq_len, RW_WIDTH), lambda b: (b, 0))]
    out_shape = [jax.ShapeDtypeStruct((batch * seq_len, RW_WIDTH), F32)]
    if want_state:
        out_specs.append(state_spec)
        out_shape.append(jax.ShapeDtypeStruct((batch, 2, RW_HEADS, RW_HEAD_DIM, RW_HEAD_DIM), F32))
    outs = pl.pallas_call(
        functools.partial(_rwkv_kernel, seq_len=seq_len, has_state_in=has_state_in,
                          has_state_out=want_state),
        grid=(batch,),
        in_specs=in_specs,
        out_specs=out_specs,
        out_shape=out_shape,
        scratch_shapes=[
            pltpu.VMEM((2, seq_len, RW_WIDTH), F32),
            pltpu.VMEM((2, seq_len, RW_WIDTH), F32),
            pltpu.VMEM((seq_len, RW_WIDTH), F32),
            pltpu.VMEM((2, RW_HEADS, RW_HEAD_DIM, RW_HEAD_DIM), F32),
        ],
        compiler_params=pltpu.CompilerParams(
            dimension_semantics=("parallel",), vmem_limit_bytes=VMEM_LIMIT),
        name="rwkv",
    )(*args)
    return outs if want_state else (outs[0], None)


def _merge_kernel(x_ref, mod_ref, oa_ref, yg_ref, wg_ref, woa_ref, wor_ref, wout_ref,
                  g_ref, b_ref, o_ref):
    mod = mod_ref[...]
    sh1 = mod[:, 0:D_MODEL]
    sc1 = mod[:, D_MODEL:2 * D_MODEL]
    g1 = mod[:, 2 * D_MODEL:3 * D_MODEL]
    x = x_ref[...]
    h = (x * (1.0 + sc1) + sh1).astype(BF16)
    gates = jax.nn.sigmoid(_dot(h, wg_ref[...]))
    att = _dot(oa_ref[...].astype(BF16), woa_ref[...])
    rwk = _dot(yg_ref[...].astype(BF16), wor_ref[...])
    merged = gates[:, 0:D_MODEL] * att + gates[:, D_MODEL:2 * D_MODEL] * rwk
    mix = _dot(merged.astype(BF16), wout_ref[...])
    o_ref[...] = _layer_norm(ALPHA * x + g1 * mix, g_ref[...], b_ref[...], LN_EPS)


def _merge(x2d, mod3, o_att, yg, w_gates, p, seq_len, fixed_row):
    m = x2d.shape[0]
    row = lambda i: (i, 0)
    const = lambda i: (0, 0)
    return pl.pallas_call(
        _merge_kernel,
        grid=(m // ROW_TILE,),
        in_specs=[
            pl.BlockSpec((ROW_TILE, D_MODEL), row),
            pl.BlockSpec((None, 1, 6 * D_MODEL), _mod_row_map(seq_len, fixed_row)),
            pl.BlockSpec((ROW_TILE, DA_WIDTH), row),
            pl.BlockSpec((ROW_TILE, RW_WIDTH), row),
            pl.BlockSpec((D_MODEL, 2 * D_MODEL), const),
            pl.BlockSpec((DA_WIDTH, D_MODEL), const),
            pl.BlockSpec((RW_WIDTH, D_MODEL), const),
            pl.BlockSpec((D_MODEL, D_MODEL), const),
            pl.BlockSpec((1, D_MODEL), const),
            pl.BlockSpec((1, D_MODEL), const),
        ],
        out_specs=pl.BlockSpec((ROW_TILE, D_MODEL), row),
        out_shape=jax.ShapeDtypeStruct((m, D_MODEL), F32),
        compiler_params=pltpu.CompilerParams(
            dimension_semantics=("parallel",), vmem_limit_bytes=VMEM_LIMIT),
        name="merge",
    )(x2d, mod3, o_att, yg, w_gates, p['w_o_attn'].astype(BF16), p['w_o_rwkv'].astype(BF16),
      p['w_out'].astype(BF16), p['ln1_g'][None], p['ln1_b'][None])


def _mlp_kernel(x_ref, mod_ref, wu_ref, wv_ref, cw_ref, cb_ref, wd_ref, g_ref, b_ref, o_ref,
                h_scr, acc_scr, *, seq_len):
    j = pl.program_id(1)
    mod = mod_ref[...]

    @pl.when(j == 0)
    def _():
        sh2 = mod[:, 3 * D_MODEL:4 * D_MODEL]
        sc2 = mod[:, 4 * D_MODEL:5 * D_MODEL]
        h_scr[...] = (x_ref[...] * (1.0 + sc2) + sh2).astype(BF16)
        acc_scr[...] = jnp.zeros_like(acc_scr)

    h = h_scr[...]
    u = _dot(h, wu_ref[...])
    val = _dot(h, wv_ref[...])
    rows = u.shape[0]
    pos = lax.broadcasted_iota(jnp.int32, (rows, 1), 0) & (seq_len - 1)
    prev = jnp.where(pos == 0, 0.0, pltpu.roll(u, 1, 0))
    nxt = jnp.where(pos == seq_len - 1, 0.0, pltpu.roll(u, rows - 1, 0))
    cw = cw_ref[...]
    u = prev * cw[0:1, :] + u * cw[1:2, :] + nxt * cw[2:3, :] + cb_ref[...]
    acc_scr[...] += _dot((jax.nn.gelu(u) * val).astype(BF16), wd_ref[...])

    @pl.when(j == pl.num_programs(1) - 1)
    def _():
        g2 = mod[:, 5 * D_MODEL:6 * D_MODEL]
        z = ALPHA * x_ref[...] + g2 * acc_scr[...]
        o_ref[...] = _layer_norm(z, g_ref[...], b_ref[...], LN_EPS)


def _mlp(x2d, mod3, p, seq_len, fixed_row):
    m = x2d.shape[0]
    assert seq_len & (seq_len - 1) == 0 and MLP_ROW_TILE % seq_len == 0
    nff = D_FF // FF_TILE
    if fixed_row is not None:
        mod_map = lambda i, j: (fixed_row, 0, 0)
    else:
        mod_map = lambda i, j: (i * MLP_ROW_TILE // seq_len, 0, 0)
    row = lambda i, j: (i, 0)
    const = lambda i, j: (0, 0)
    w_up = p['w_up'].astype(BF16)
    return pl.pallas_call(
        functools.partial(_mlp_kernel, seq_len=seq_len),
        grid=(m // MLP_ROW_TILE, nff),
        in_specs=[
            pl.BlockSpec((MLP_ROW_TILE, D_MODEL), row),
            pl.BlockSpec((None, 1, 6 * D_MODEL), mod_map),
            pl.BlockSpec((D_MODEL, FF_TILE), lambda i, j: (0, j)),
            pl.BlockSpec((D_MODEL, FF_TILE), lambda i, j: (0, nff + j)),
            pl.BlockSpec((3, FF_TILE), lambda i, j: (0, j)),
            pl.BlockSpec((1, FF_TILE), lambda i, j: (0, j)),
            pl.BlockSpec((FF_TILE, D_MODEL), lambda i, j: (j, 0)),
            pl.BlockSpec((1, D_MODEL), const),
            pl.BlockSpec((1, D_MODEL), const),
        ],
        out_specs=pl.BlockSpec((MLP_ROW_TILE, D_MODEL), row),
        out_shape=jax.ShapeDtypeStruct((m, D_MODEL), F32),
        scratch_shapes=[
            pltpu.VMEM((MLP_ROW_TILE, D_MODEL), BF16),
            pltpu.VMEM((MLP_ROW_TILE, D_MODEL), F32),
        ],
        compiler_params=pltpu.CompilerParams(
            dimension_semantics=("parallel", "arbitrary"), vmem_limit_bytes=VMEM_LIMIT),
        name="mlp",
    )(x2d, mod3, w_up, w_up, p['conv_w'], p['conv_b'][None], p['w_down'].astype(BF16),
      p['ln2_g'][None], p['ln2_b'][None])


def _trunk_layer(x, mod3, fixed_row, p, w_qkvr, w_gates, layer, ctx=None):
    batch, seq_len, _ = x.shape
    x2d = x.reshape(batch * seq_len, D_MODEL)
    q, k, v, rw = _input_projection(x2d, mod3, w_qkvr, seq_len, fixed_row)
    if ctx is None:
        o_att = _attention(q, k, v, p['da_lambda'], p['da_subln_g'][None], batch, seq_len, layer)
        yg, state = _rwkv(rw, p, batch, seq_len, want_state=True)
    else:
        k_ctx, v_ctx, s_ctx = ctx
        past = k_ctx.shape[1]
        o_att = _attention(q, k, v, p['da_lambda'], p['da_subln_g'][None], batch, seq_len, layer,
                           ctx=(k_ctx.reshape(batch, past, DA_WIDTH), v_ctx.reshape(batch, past, DA_WIDTH)))
        yg, state = _rwkv(rw, p, batch, seq_len, state_in=s_ctx)
    x1 = _merge(x2d, mod3, o_att, yg, w_gates, p, seq_len, fixed_row)
    y = _mlp(x1, mod3, p, seq_len, fixed_row)
    new_ctx = None
    if ctx is None:
        new_ctx = (k.reshape(batch, seq_len, DA_HEADS, 2, DA_HEAD_DIM),
                   v.reshape(batch, seq_len, DA_HEADS, 2 * DA_HEAD_DIM), state)
    return y.reshape(batch, seq_len, D_MODEL), new_ctx


def kernel(x_prompt, x_sample, cache_k, cache_v, state_rwkv, c, c_ctx, w_ada, b_ada, w_in, rw_mu, rw_w0, rw_w_up, rw_a0, rw_a_up, rw_g_up, rw_k_k, rw_k_a, rw_r_k, rw_lnx_g, rw_lnx_b, da_lambda, da_subln_g, w_o_attn, w_o_rwkv, w_out, ln1_g, ln1_b, w_up, conv_w, conv_b, w_down, ln2_g, ln2_b):
    dec_batch = x_sample.shape[0]
    assert dec_batch < MOD_ROWS
    y_prompt, y_sample = x_prompt, x_sample
    new_k, new_v, new_s = [], [], []
    for l in range(DEPTH):
        p = {
            'rw_mu': rw_mu[l], 'rw_w0': rw_w0[l], 'rw_w_up': rw_w_up[l], 'rw_a0': rw_a0[l],
            'rw_a_up': rw_a_up[l], 'rw_g_up': rw_g_up[l], 'rw_k_k': rw_k_k[l], 'rw_k_a': rw_k_a[l],
            'rw_r_k': rw_r_k[l], 'rw_lnx_g': rw_lnx_g[l], 'rw_lnx_b': rw_lnx_b[l],
            'da_lambda': da_lambda[l], 'da_subln_g': da_subln_g[l], 'w_o_attn': w_o_attn[l],
            'w_o_rwkv': w_o_rwkv[l], 'w_out': w_out[l], 'ln1_g': ln1_g[l], 'ln1_b': ln1_b[l],
            'w_up': w_up[l], 'conv_w': conv_w[l], 'conv_b': conv_b[l], 'w_down': w_down[l],
            'ln2_g': ln2_g[l], 'ln2_b': ln2_b[l],
        }
        cvec = jnp.concatenate(
            [c, c_ctx[None], jnp.zeros((MOD_ROWS - dec_batch - 1, D_MODEL), F32)], axis=0)
        mod3 = _modulation(cvec, w_ada[l], b_ada[l][None]).reshape(MOD_ROWS, 1, 6 * D_MODEL)
        w_qkvr = w_in[l][:, :QKVR_COLS].astype(BF16)
        w_gates = w_in[l][:, QKVR_COLS:].astype(BF16)
        y_prompt, ctx_l = _trunk_layer(y_prompt, mod3, dec_batch, p, w_qkvr, w_gates, l)
        new_k.append(ctx_l[0])
        new_v.append(ctx_l[1])
        new_s.append(ctx_l[2])
        y_sample, _ = _trunk_layer(y_sample, mod3, None, p, w_qkvr, w_gates, l,
                                   ctx=(cache_k[:, l], cache_v[:, l], state_rwkv[:, l]))
    return (y_prompt, y_sample, jnp.stack(new_k, axis=1), jnp.stack(new_v, axis=1),
            jnp.stack(new_s, axis=1))
```

```python
import functools
import math

import jax
import jax.numpy as jnp
from jax import lax
from jax.experimental import pallas as pl
from jax.experimental.pallas import tpu as pltpu

F32 = jnp.float32
BF16 = jnp.bfloat16
HIGHEST = lax.Precision.HIGHEST

D_MODEL = 1024
GRID_W = 64
DA_HEADS = 4
DA_HEAD_DIM = 64
DA_WIDTH = DA_HEADS * 2 * DA_HEAD_DIM
ROPE_PAIRS_PER_AXIS = DA_HEAD_DIM // 4
ROPE_BASE = 10000.0
RW_HEADS = 8
RW_HEAD_DIM = 64
RW_WIDTH = RW_HEADS * RW_HEAD_DIM
DECAY_LORA = 64
AAA_LORA = 64
GATE_LORA = 128
RW_COLS = 3 * RW_WIDTH + DECAY_LORA + AAA_LORA + GATE_LORA
RW_LNX_EPS = 64e-5
QKVR_COLS = 3 * DA_WIDTH + RW_COLS
D_FF = 2816
LN_EPS = 1e-5
DEPTH = 1
ALPHA = (2.0 * DEPTH) ** 0.25

CHUNK = 64
ATTN_Q_BLOCK = 256
ROW_TILE = 512
MLP_ROW_TILE = 1024
FF_TILE = 256
MOD_COL_TILE = 768
MOD_ROWS = 16
VMEM_LIMIT = 48 * 1024 * 1024

_NN = (((1,), (0,)), ((), ()))
_NT = (((1,), (1,)), ((), ()))
_TN = (((0,), (0,)), ((), ()))


def _dot(a, b, dims=_NN, precision=None):
    return lax.dot_general(a, b, dims, precision=precision, preferred_element_type=F32)


def _layer_norm(z, g, b, eps):
    mu = jnp.mean(z, axis=-1, keepdims=True)
    zc = z - mu
    var = jnp.mean(zc * zc, axis=-1, keepdims=True)
    return zc * lax.rsqrt(var + eps) * g + b


def _mod_kernel(c_ref, w_ref, b_ref, o_ref):
    cv = c_ref[...]
    s = cv * jax.nn.sigmoid(cv)
    o_ref[...] = _dot(s, w_ref[...], precision=HIGHEST) + b_ref[...]


def _modulation(cvec, w_ada, b_ada):
    n = w_ada.shape[1]
    return pl.pallas_call(
        _mod_kernel,
        grid=(n // MOD_COL_TILE,),
        in_specs=[
            pl.BlockSpec((MOD_ROWS, D_MODEL), lambda j: (0, 0)),
            pl.BlockSpec((D_MODEL, MOD_COL_TILE), lambda j: (0, j)),
            pl.BlockSpec((1, MOD_COL_TILE), lambda j: (0, j)),
        ],
        out_specs=pl.BlockSpec((MOD_ROWS, MOD_COL_TILE), lambda j: (0, j)),
        out_shape=jax.ShapeDtypeStruct((MOD_ROWS, n), F32),
        compiler_params=pltpu.CompilerParams(
            dimension_semantics=("parallel",), vmem_limit_bytes=VMEM_LIMIT),
        name="mod",
    )(cvec, w_ada, b_ada)


def _mod_row_map(rows_per_batch, fixed_row):
    if fixed_row is not None:
        return lambda i: (fixed_row, 0, 0)
    tiles = rows_per_batch // ROW_TILE
    return lambda i: (i // tiles, 0, 0)


def _inproj_kernel(x_ref, mod_ref, w_ref, q_ref, k_ref, v_ref, rw_ref):
    mod = mod_ref[...]
    sh1 = mod[:, 0:D_MODEL]
    sc1 = mod[:, D_MODEL:2 * D_MODEL]
    h = (x_ref[...] * (1.0 + sc1) + sh1).astype(BF16)
    q_ref[...] = _dot(h, w_ref[:, 0:DA_WIDTH])
    k_ref[...] = _dot(h, w_ref[:, DA_WIDTH:2 * DA_WIDTH])
    v_ref[...] = _dot(h, w_ref[:, 2 * DA_WIDTH:3 * DA_WIDTH])
    rw_ref[...] = _dot(h, w_ref[:, 3 * DA_WIDTH:QKVR_COLS])


def _input_projection(x2d, mod3, w_qkvr, seq_len, fixed_row):
    m = x2d.shape[0]
    row = lambda i: (i, 0)
    return pl.pallas_call(
        _inproj_kernel,
        grid=(m // ROW_TILE,),
        in_specs=[
            pl.BlockSpec((ROW_TILE, D_MODEL), row),
            pl.BlockSpec((None, 1, 6 * D_MODEL), _mod_row_map(seq_len, fixed_row)),
            pl.BlockSpec((D_MODEL, QKVR_COLS), lambda i: (0, 0)),
        ],
        out_specs=[
            pl.BlockSpec((ROW_TILE, DA_WIDTH), row),
            pl.BlockSpec((ROW_TILE, DA_WIDTH), row),
            pl.BlockSpec((ROW_TILE, DA_WIDTH), row),
            pl.BlockSpec((ROW_TILE, RW_COLS), row),
        ],
        out_shape=[
            jax.ShapeDtypeStruct((m, DA_WIDTH), F32),
            jax.ShapeDtypeStruct((m, DA_WIDTH), F32),
            jax.ShapeDtypeStruct((m, DA_WIDTH), F32),
            jax.ShapeDtypeStruct((m, RW_COLS), F32),
        ],
        compiler_params=pltpu.CompilerParams(
            dimension_semantics=("parallel",), vmem_limit_bytes=VMEM_LIMIT),
        name="inproj",
    )(x2d, mod3, w_qkvr)


def _rope(x, cos, sin_signed):
    lane = lax.broadcasted_iota(jnp.int32, x.shape, 1)
    partner = jnp.where((lane & 63) < 32, pltpu.roll(x, 96, 1), pltpu.roll(x, 32, 1))
    return x * cos + partner * sin_signed


def _softmax_av(qm, keys, vals):
    scores = [_dot(qm, kg, _NT) for kg in keys]
    mx = scores[0].max(axis=-1, keepdims=True)
    for s in scores[1:]:
        mx = jnp.maximum(mx, s.max(axis=-1, keepdims=True))
    den = None
    out = None
    for s, vg in zip(scores, vals):
        p = jnp.exp(s - mx)
        d = jnp.sum(p, axis=-1, keepdims=True)
        o = _dot(p.astype(BF16), vg)
        den = d if den is None else den + d
        out = o if out is None else out + o
    return out / den


def _attn_kernel(*refs, has_ctx, seq_len, lam_init):
    if has_ctx:
        q_ref, k_ref, v_ref, kc_ref, vc_ref, cos_ref, sin_ref, lq_ref, g_ref, o_ref = refs
    else:
        q_ref, k_ref, v_ref, lq_ref, g_ref, o_ref = refs
    d = DA_HEAD_DIM
    lq = lq_ref[...]
    lam = (jnp.exp(jnp.sum(lq[0:1] * lq[1:2], axis=-1, keepdims=True))
           - jnp.exp(jnp.sum(lq[2:3] * lq[3:4], axis=-1, keepdims=True)) + lam_init)
    k = k_ref[...]
    if has_ctx:
        k = _rope(k, cos_ref[...], sin_ref[...])
    keys = [[k[:, m * d:(m + 1) * d].astype(BF16)] for m in range(2)]
    vals = [v_ref[...].astype(BF16)]
    if has_ctx:
        kc = kc_ref[...]
        for m in range(2):
            keys[m].append(kc[:, m * d:(m + 1) * d].astype(BF16))
        vals.append(vc_ref[...].astype(BF16))
    g = g_ref[...]
    for qb in range(seq_len // ATTN_Q_BLOCK):
        rows = slice(qb * ATTN_Q_BLOCK, (qb + 1) * ATTN_Q_BLOCK)
        q = q_ref[rows, :]
        if has_ctx:
            q = _rope(q, cos_ref[rows, :], sin_ref[rows, :])
        q = q * (d ** -0.5)
        o1 = _softmax_av(q[:, 0:d].astype(BF16), keys[0], vals)
        o2 = _softmax_av(q[:, d:2 * d].astype(BF16), keys[1], vals)
        o = o1 - lam * o2
        ms = jnp.mean(o * o, axis=-1, keepdims=True)
        o_ref[rows, :] = o * lax.rsqrt(ms + LN_EPS) * g * (1.0 - lam_init)


def _rope_tables(n):
    rows = n // GRID_W
    row = jnp.repeat(jnp.arange(rows, dtype=F32), GRID_W)
    col = jnp.tile(jnp.arange(GRID_W, dtype=F32), rows)
    inv = ROPE_BASE ** (-jnp.arange(ROPE_PAIRS_PER_AXIS, dtype=F32) / ROPE_PAIRS_PER_AXIS)
    ang = jnp.concatenate([row[:, None] * inv, col[:, None] * inv], -1)
    cos, sin = jnp.cos(ang), jnp.sin(ang)
    return jnp.tile(cos, (1, 4)), jnp.tile(jnp.concatenate([-sin, sin], -1), (1, 2))


def _attention(q2d, k2d, v2d, da_lambda, subln_g, batch, seq_len, layer, ctx=None):
    has_ctx = ctx is not None
    w = 2 * DA_HEAD_DIM
    head = lambda b, h: (b, h)
    const = lambda b, h: (0, 0)
    in_specs = [pl.BlockSpec((seq_len, w), head)] * 3
    args = [q2d, k2d, v2d]
    if has_ctx:
        kc, vc = ctx
        past = kc.shape[1]
        in_specs += [pl.BlockSpec((None, past, w), lambda b, h: (b, 0, h))] * 2
        in_specs += [pl.BlockSpec((seq_len, w), const)] * 2
        args += [kc, vc, *_rope_tables(seq_len)]
    in_specs += [pl.BlockSpec((4, DA_HEAD_DIM), const), pl.BlockSpec((1, w), const)]
    args += [da_lambda, subln_g]
    lam_init = 0.8 - 0.6 * math.exp(-0.3 * layer)
    return pl.pallas_call(
        functools.partial(_attn_kernel, has_ctx=has_ctx, seq_len=seq_len, lam_init=lam_init),
        grid=(batch, DA_HEADS),
        in_specs=in_specs,
        out_specs=pl.BlockSpec((seq_len, w), head),
        out_shape=jax.ShapeDtypeStruct((batch * seq_len, DA_WIDTH), F32),
        compiler_params=pltpu.CompilerParams(
            dimension_semantics=("parallel", "parallel"), vmem_limit_bytes=VMEM_LIMIT),
        name="attn",
    )(*args)


PAIR = 2 * RW_HEAD_DIM
N_PAIRS = RW_HEADS // 2
FINISH_ROWS = 256

_M_STRICT = (0, 2)
_M_INCL = (1, 3)
_M_BLOCK8, _M_OFF16, _M_OFF32, _M_OFF64, _M_EYE = 4, 5, 6, 7, 8


def _pair_masks():
    t = jnp.arange(PAIR)[:, None]
    s = jnp.arange(PAIR)[None, :]
    same = lambda n: (t // n) == (s // n)
    head = same(CHUNK)
    masks = [head & (t > s), head & (t >= s), head & (t < s), head & (t <= s),
             same(8), same(16) & ~same(8), same(32) & ~same(16), head & ~same(32), t == s]
    return jnp.stack(masks).astype(F32)


def _cumsum_matrices():
    t = jnp.arange(CHUNK)[:, None]
    s = jnp.arange(CHUNK)[None, :]
    return jnp.stack([t >= s, t <= s]).astype(BF16)


def _pair_sum_matrix():
    i = jnp.arange(PAIR)
    return ((i[:, None] // RW_HEAD_DIM) == (i[None, :] // RW_HEAD_DIM)).astype(BF16)


def _split3(x):
    hi = x.astype(BF16)
    r1 = x - hi.astype(F32)
    mid = r1.astype(BF16)
    lo = (r1 - mid.astype(F32)).astype(BF16)
    return hi, mid, lo


def _head_sums(x, e_ref):
    rows = x.shape[0]
    xs = jnp.concatenate([x[:, p * PAIR:(p + 1) * PAIR] for p in range(N_PAIRS)], axis=0)
    s = _dot(jnp.concatenate(_split3(xs), axis=0), e_ref[...])
    n = N_PAIRS * rows
    s = s[0:n] + s[n:2 * n] + s[2 * n:3 * n]
    return jnp.concatenate([s[p * rows:(p + 1) * rows] for p in range(N_PAIRS)], axis=1)


def _unit_triangular_inverses(mats, m_ref):
    mm = lambda x, y: _dot(x.astype(BF16), y.astype(BF16))
    a8 = [a * m_ref[_M_BLOCK8] for a in mats]
    a2 = [mm(x, x) for x in a8]
    a4 = [mm(x, x) for x in a2]
    ps = [m_ref[_M_EYE] + x for x in a8]
    ps = [p + mm(p, x) for p, x in zip(ps, a2)]
    ps = [p + mm(p, x) for p, x in zip(ps, a4)]
    for off in (_M_OFF16, _M_OFF32, _M_OFF64):
        ts = [mm(p, a * m_ref[off]) for p, a in zip(ps, mats)]
        ps = [p + mm(t, p) for p, t in zip(ps, ts)]
    return ps


def _rwkv_kernel(*refs, seq_len, has_state_in, has_state_out):
    refs = list(refs)
    rw_ref = refs.pop(0)
    s0_ref = refs.pop(0) if has_state_in else None
    (mu_ref, w0_ref, wup_ref, a0_ref, aup_ref, gup_ref, kk_ref, ka_ref, rk_ref,
     lng_ref, lnb_ref, e_ref, tri_ref, m_ref) = refs[:14]
    refs = refs[14:]
    o_ref = refs.pop(0)
    so_ref = refs.pop(0) if has_state_out else None
    y_scr, bv_scr, gr_scr, s_scr = refs

    C = CHUNK
    N = RW_HEAD_DIM
    W = RW_WIDTH
    nc = seq_len // C
    hi = functools.partial(_dot, precision=HIGHEST)
    zeros_nn = jnp.zeros((N, N), F32)

    for d in range(2):
        for p in range(N_PAIRS):
            if has_state_in:
                top = jnp.concatenate([s0_ref[d, 2 * p], zeros_nn], axis=1)
                bot = jnp.concatenate([zeros_nn, s0_ref[d, 2 * p + 1]], axis=1)
                s_scr[d, p] = jnp.concatenate([top, bot], axis=0)
            else:
                s_scr[d, p] = jnp.zeros((PAIR, PAIR), F32)

    first_head = lax.broadcasted_iota(jnp.int32, (C, PAIR), 1) < N

    def block_diag(x):
        return jnp.concatenate([jnp.where(first_head, x, 0.0), jnp.where(first_head, 0.0, x)], axis=0)

    def chunk_inputs(d, c):
        r0 = pl.multiple_of(c * C, C)
        rows = pl.ds(r0, C)
        cur = rw_ref[rows, :]
        prow = rw_ref[pl.ds(jnp.maximum(r0 - 1, 0), 1), :] * jnp.where(r0 > 0, 1.0, 0.0)
        nrow = rw_ref[pl.ds(jnp.minimum(r0 + C, seq_len - 1), 1), :] * jnp.where(r0 + C < seq_len, 1.0, 0.0)
        rid = lax.broadcasted_iota(jnp.int32, (C, 1), 0)
        prev = jnp.where(rid == 0, prow, pltpu.roll(cur, 1, 0))
        nxt = jnp.where(rid == C - 1, nrow, pltpu.roll(cur, C - 1, 0))
        xm = cur + mu_ref[0:1, :] * (prev - cur) + mu_ref[1:2, :] * (nxt - cur)

        r = xm[:, 0:W]
        kr = xm[:, W:2 * W]
        vr = xm[:, 2 * W:3 * W]
        w_lo = xm[:, 3 * W:3 * W + DECAY_LORA]
        a_lo = xm[:, 3 * W + DECAY_LORA:3 * W + DECAY_LORA + AAA_LORA]
        g_lo = xm[:, 3 * W + DECAY_LORA + AAA_LORA:RW_COLS]

        wlog = -jax.nn.softplus(-(w0_ref[d:d + 1, :] + hi(jnp.tanh(w_lo), wup_ref[d]))) - 0.5
        log_decay = -jnp.exp(wlog)
        a = jax.nn.sigmoid(a0_ref[d:d + 1, :] + hi(a_lo, aup_ref[d]))
        keff = kr * (1.0 + (a - 1.0) * ka_ref[...])
        kk = kr * kk_ref[...]
        sums = _head_sums(jnp.concatenate([kk * kk, r * keff * rk_ref[...]], axis=0), e_ref)
        kk = kk / jnp.maximum(jnp.sqrt(sums[0:C]), 1e-12)
        bv_scr[d, rows, :] = sums[C:2 * C] * vr
        if d == 0:
            gr_scr[rows, :] = _dot(jax.nn.sigmoid(g_lo).astype(BF16), gup_ref[...])

        cum = _dot(tri_ref[d], jnp.concatenate(_split3(log_decay), axis=1))
        cum = cum[:, 0:W] + cum[:, W:2 * W] + cum[:, 2 * W:3 * W]
        g_in = jnp.exp(cum)
        g_inv = jnp.exp(-cum)
        return dict(rows=rows, a=-kk * jnp.exp(cum - log_decay), r=r * g_in, b=kk * a * g_inv,
                    k=keff * g_inv, v=vr, g_end=g_in[C - 1:C, :] if d == 0 else g_in[0:1, :])

    def scan_body(i, carry):
        inputs = (chunk_inputs(0, i), chunk_inputs(1, nc - 1 - i))
        chains = [(d, p) for d in range(2) for p in range(N_PAIRS)]
        ar, b_bd, k_bd, v_bd, s_old = [], [], [], [], []
        for d, p in chains:
            x = inputs[d]
            sl = slice(p * PAIR, (p + 1) * PAIR)
            ar.append(jnp.concatenate([block_diag(x['a'][:, sl]), block_diag(x['r'][:, sl])],
                                      axis=0).astype(BF16))
            b_bd.append(block_diag(x['b'][:, sl]).astype(BF16))
            k_bd.append(block_diag(x['k'][:, sl]).astype(BF16))
            v_bd.append(block_diag(x['v'][:, sl]).astype(BF16))
            s_old.append(s_scr[d, p])
        n = range(len(chains))
        xb = [_dot(ar[i], b_bd[i], _NT) for i in n]
        xk = [_dot(ar[i], k_bd[i], _NT) for i in n]
        xs = [_dot(ar[i], s_old[i].astype(BF16), _NT) for i in n]
        strict = [m_ref[_M_STRICT[d]] for d, _ in chains]
        incl = [m_ref[_M_INCL[d]] for d, _ in chains]
        t_inv = _unit_triangular_inverses([xb[i][0:PAIR] * strict[i] for i in n], m_ref)
        rhs = [xs[i][0:PAIR] + _dot((xk[i][0:PAIR] * strict[i]).astype(BF16), v_bd[i]) for i in n]
        u = [_dot(t_inv[i].astype(BF16), rhs[i].astype(BF16)).astype(BF16) for i in n]
        y = [xs[i][PAIR:2 * PAIR]
             + _dot((xb[i][PAIR:2 * PAIR] * incl[i]).astype(BF16), u[i])
             + _dot((xk[i][PAIR:2 * PAIR] * incl[i]).astype(BF16), v_bd[i]) for i in n]
        s_new = [s_old[i] + _dot(u[i], b_bd[i], _TN) + _dot(v_bd[i], k_bd[i], _TN) for i in n]
        for i, (d, p) in enumerate(chains):
            sl = slice(p * PAIR, (p + 1) * PAIR)
            y_scr[d, inputs[d]['rows'], sl] = y[i][0:C] + y[i][C:2 * C]
            s_scr[d, p] = s_new[i] * inputs[d]['g_end'][:, sl]
        return carry

    lax.fori_loop(0, nc, scan_body, 0)

    def finish_body(i, carry):
        rows = pl.ds(pl.multiple_of(i * FINISH_ROWS, FINISH_ROWS), FINISH_ROWS)
        y = y_scr[0, rows, :] + y_scr[1, rows, :]
        yc = y - _head_sums(y, e_ref) * (1.0 / N)
        var = _head_sums(yc * yc, e_ref) * (1.0 / N)
        yn = yc * lax.rsqrt(var + RW_LNX_EPS) * lng_ref[...] + lnb_ref[...]
        o_ref[rows, :] = (yn + bv_scr[0, rows, :] + bv_scr[1, rows, :]) * gr_scr[rows, :]
        return carry

    lax.fori_loop(0, seq_len // FINISH_ROWS, finish_body, 0)
    if has_state_out:
        for d in range(2):
            for p in range(N_PAIRS):
                s_pair = s_scr[d, p]
                so_ref[d, 2 * p] = s_pair[0:N, 0:N]
                so_ref[d, 2 * p + 1] = s_pair[N:PAIR, N:PAIR]


def _rwkv(rw2d, p, batch, seq_len, state_in=None, want_state=False):
    has_state_in = state_in is not None
    const2 = lambda b: (0, 0)
    const3 = lambda b: (0, 0, 0)
    state_spec = pl.BlockSpec((None, 2, RW_HEADS, RW_HEAD_DIM, RW_HEAD_DIM), lambda b: (b, 0, 0, 0, 0))
    in_specs = [pl.BlockSpec((seq_len, RW_COLS), lambda b: (b, 0))]
    args = [rw2d]
    if has_state_in:
        in_specs.append(state_spec)
        args.append(state_in)
    in_specs += [
        pl.BlockSpec((2, RW_COLS), const2),
        pl.BlockSpec((2, RW_WIDTH), const2),
        pl.BlockSpec((2, DECAY_LORA, RW_WIDTH), const3),
        pl.BlockSpec((2, RW_WIDTH), const2),
        pl.BlockSpec((2, AAA_LORA, RW_WIDTH), const3),
        pl.BlockSpec((GATE_LORA, RW_WIDTH), const2),
        pl.BlockSpec((1, RW_WIDTH), const2),
        pl.BlockSpec((1, RW_WIDTH), const2),
        pl.BlockSpec((1, RW_WIDTH), const2),
        pl.BlockSpec((1, RW_WIDTH), const2),
        pl.BlockSpec((1, RW_WIDTH), const2),
        pl.BlockSpec((PAIR, PAIR), const2),
        pl.BlockSpec((2, CHUNK, CHUNK), const3),
        pl.BlockSpec((9, PAIR, PAIR), const3),
    ]
    args += [p['rw_mu'], p['rw_w0'], p['rw_w_up'], p['rw_a0'], p['rw_a_up'],
             p['rw_g_up'].astype(BF16), p['rw_k_k'][None], p['rw_k_a'][None],
             p['rw_r_k'].reshape(1, RW_WIDTH), p['rw_lnx_g'][None], p['rw_lnx_b'][None],
             _pair_sum_matrix(), _cumsum_matrices(), _pair_masks()]
    out_specs = [pl.BlockSpec((seq_len, RW_WIDTH), lambda b: (b, 0))]
    out_shape = [jax.ShapeDtypeStruct((batch * seq_len, RW_WIDTH), F32)]
    if want_state:
        out_specs.append(state_spec)
        out_shape.append(jax.ShapeDtypeStruct((batch, 2, RW_HEADS, RW_HEAD_DIM, RW_HEAD_DIM), F32))
    outs = pl.pallas_call(
        functools.partial(_rwkv_kernel, seq_len=seq_len, has_state_in=has_state_in,
                          has_state_out=want_state),
        grid=(batch,),
        in_specs=in_specs,
        out_specs=out_specs,
        out_shape=out_shape,
        scratch_shapes=[
            pltpu.VMEM((2, seq_len, RW_WIDTH), F32),
            pltpu.VMEM((2, seq_len, RW_WIDTH), F32),
            pltpu.VMEM((seq_len, RW_WIDTH), F32),
            pltpu.VMEM((2, N_PAIRS, PAIR, PAIR), F32),
        ],
        compiler_params=pltpu.CompilerParams(
            dimension_semantics=("parallel",), vmem_limit_bytes=VMEM_LIMIT),
        name="rwkv",
    )(*args)
    return outs if want_state else (outs[0], None)


def _merge_kernel(x_ref, mod_ref, oa_ref, yg_ref, wg_ref, woa_ref, wor_ref, wout_ref,
                  g_ref, b_ref, o_ref):
    mod = mod_ref[...]
    sh1 = mod[:, 0:D_MODEL]
    sc1 = mod[:, D_MODEL:2 * D_MODEL]
    g1 = mod[:, 2 * D_MODEL:3 * D_MODEL]
    x = x_ref[...]
    h = (x * (1.0 + sc1) + sh1).astype(BF16)
    gates = jax.nn.sigmoid(_dot(h, wg_ref[...]))
    att = _dot(oa_ref[...].astype(BF16), woa_ref[...])
    rwk = _dot(yg_ref[...].astype(BF16), wor_ref[...])
    merged = gates[:, 0:D_MODEL] * att + gates[:, D_MODEL:2 * D_MODEL] * rwk
    mix = _dot(merged.astype(BF16), wout_ref[...])
    o_ref[...] = _layer_norm(ALPHA * x + g1 * mix, g_ref[...], b_ref[...], LN_EPS)


def _merge(x2d, mod3, o_att, yg, w_gates, p, seq_len, fixed_row):
    m = x2d.shape[0]
    row = lambda i: (i, 0)
    const = lambda i: (0, 0)
    return pl.pallas_call(
        _merge_kernel,
        grid=(m // ROW_TILE,),
        in_specs=[
            pl.BlockSpec((ROW_TILE, D_MODEL), row),
            pl.BlockSpec((None, 1, 6 * D_MODEL), _mod_row_map(seq_len, fixed_row)),
            pl.BlockSpec((ROW_TILE, DA_WIDTH), row),
            pl.BlockSpec((ROW_TILE, RW_WIDTH), row),
            pl.BlockSpec((D_MODEL, 2 * D_MODEL), const),
            pl.BlockSpec((DA_WIDTH, D_MODEL), const),
            pl.BlockSpec((RW_WIDTH, D_MODEL), const),
            pl.BlockSpec((D_MODEL, D_MODEL), const),
            pl.BlockSpec((1, D_MODEL), const),
            pl.BlockSpec((1, D_MODEL), const),
        ],
        out_specs=pl.BlockSpec((ROW_TILE, D_MODEL), row),
        out_shape=jax.ShapeDtypeStruct((m, D_MODEL), F32),
        compiler_params=pltpu.CompilerParams(
            dimension_semantics=("parallel",), vmem_limit_bytes=VMEM_LIMIT),
        name="merge",
    )(x2d, mod3, o_att, yg, w_gates, p['w_o_attn'].astype(BF16), p['w_o_rwkv'].astype(BF16),
      p['w_out'].astype(BF16), p['ln1_g'][None], p['ln1_b'][None])


def _mlp_kernel(x_ref, mod_ref, wu_ref, wv_ref, cw_ref, cb_ref, wd_ref, g_ref, b_ref, o_ref,
                h_scr, acc_scr, *, seq_len):
    j = pl.program_id(1)
    mod = mod_ref[...]

    @pl.when(j == 0)
    def _():
        sh2 = mod[:, 3 * D_MODEL:4 * D_MODEL]
        sc2 = mod[:, 4 * D_MODEL:5 * D_MODEL]
        h_scr[...] = (x_ref[...] * (1.0 + sc2) + sh2).astype(BF16)
        acc_scr[...] = jnp.zeros_like(acc_scr)

    h = h_scr[...]
    u = _dot(h, wu_ref[...])
    val = _dot(h, wv_ref[...])
    rows = u.shape[0]
    pos = lax.broadcasted_iota(jnp.int32, (rows, 1), 0) & (seq_len - 1)
    prev = jnp.where(pos == 0, 0.0, pltpu.roll(u, 1, 0))
    nxt = jnp.where(pos == seq_len - 1, 0.0, pltpu.roll(u, rows - 1, 0))
    cw = cw_ref[...]
    u = prev * cw[0:1, :] + u * cw[1:2, :] + nxt * cw[2:3, :] + cb_ref[...]
    acc_scr[...] += _dot((jax.nn.gelu(u) * val).astype(BF16), wd_ref[...])

    @pl.when(j == pl.num_programs(1) - 1)
    def _():
        g2 = mod[:, 5 * D_MODEL:6 * D_MODEL]
        z = ALPHA * x_ref[...] + g2 * acc_scr[...]
        o_ref[...] = _layer_norm(z, g_ref[...], b_ref[...], LN_EPS)


def _mlp(x2d, mod3, p, seq_len, fixed_row):
    m = x2d.shape[0]
    assert seq_len & (seq_len - 1) == 0 and MLP_ROW_TILE % seq_len == 0
    nff = D_FF // FF_TILE
    if fixed_row is not None:
        mod_map = lambda i, j: (fixed_row, 0, 0)
    else:
        mod_map = lambda i, j: (i * MLP_ROW_TILE // seq_len, 0, 0)
    row = lambda i, j: (i, 0)
    const = lambda i, j: (0, 0)
    w_up = p['w_up'].astype(BF16)
    return pl.pallas_call(
        functools.partial(_mlp_kernel, seq_len=seq_len),
        grid=(m // MLP_ROW_TILE, nff),
        in_specs=[
            pl.BlockSpec((MLP_ROW_TILE, D_MODEL), row),
            pl.BlockSpec((None, 1, 6 * D_MODEL), mod_map),
            pl.BlockSpec((D_MODEL, FF_TILE), lambda i, j: (0, j)),
            pl.BlockSpec((D_MODEL, FF_TILE), lambda i, j: (0, nff + j)),
            pl.BlockSpec((3, FF_TILE), lambda i, j: (0, j)),
            pl.BlockSpec((1, FF_TILE), lambda i, j: (0, j)),
            pl.BlockSpec((FF_TILE, D_MODEL), lambda i, j: (j, 0)),
            pl.BlockSpec((1, D_MODEL), const),
            pl.BlockSpec((1, D_MODEL), const),
        ],
        out_specs=pl.BlockSpec((MLP_ROW_TILE, D_MODEL), row),
        out_shape=jax.ShapeDtypeStruct((m, D_MODEL), F32),
        scratch_shapes=[
            pltpu.VMEM((MLP_ROW_TILE, D_MODEL), BF16),
            pltpu.VMEM((MLP_ROW_TILE, D_MODEL), F32),
        ],
        compiler_params=pltpu.CompilerParams(
            dimension_semantics=("parallel", "arbitrary"), vmem_limit_bytes=VMEM_LIMIT),
        name="mlp",
    )(x2d, mod3, w_up, w_up, p['conv_w'], p['conv_b'][None], p['w_down'].astype(BF16),
      p['ln2_g'][None], p['ln2_b'][None])


def _trunk_layer(x, mod3, fixed_row, p, w_qkvr, w_gates, layer, ctx=None):
    batch, seq_len, _ = x.shape
    x2d = x.reshape(batch * seq_len, D_MODEL)
    q, k, v, rw = _input_projection(x2d, mod3, w_qkvr, seq_len, fixed_row)
    if ctx is None:
        o_att = _attention(q, k, v, p['da_lambda'], p['da_subln_g'][None], batch, seq_len, layer)
        yg, state = _rwkv(rw, p, batch, seq_len, want_state=True)
    else:
        k_ctx, v_ctx, s_ctx = ctx
        past = k_ctx.shape[1]
        o_att = _attention(q, k, v, p['da_lambda'], p['da_subln_g'][None], batch, seq_len, layer,
                           ctx=(k_ctx.reshape(batch, past, DA_WIDTH), v_ctx.reshape(batch, past, DA_WIDTH)))
        yg, state = _rwkv(rw, p, batch, seq_len, state_in=s_ctx)
    x1 = _merge(x2d, mod3, o_att, yg, w_gates, p, seq_len, fixed_row)
    y = _mlp(x1, mod3, p, seq_len, fixed_row)
    new_ctx = None
    if ctx is None:
        new_ctx = (k.reshape(batch, seq_len, DA_HEADS, 2, DA_HEAD_DIM),
                   v.reshape(batch, seq_len, DA_HEADS, 2 * DA_HEAD_DIM), state)
    return y.reshape(batch, seq_len, D_MODEL), new_ctx


def kernel(x_prompt, x_sample, cache_k, cache_v, state_rwkv, c, c_ctx, w_ada, b_ada, w_in, rw_mu, rw_w0, rw_w_up, rw_a0, rw_a_up, rw_g_up, rw_k_k, rw_k_a, rw_r_k, rw_lnx_g, rw_lnx_b, da_lambda, da_subln_g, w_o_attn, w_o_rwkv, w_out, ln1_g, ln1_b, w_up, conv_w, conv_b, w_down, ln2_g, ln2_b):
    dec_batch = x_sample.shape[0]
    assert dec_batch < MOD_ROWS
    y_prompt, y_sample = x_prompt, x_sample
    new_k, new_v, new_s = [], [], []
    for l in range(DEPTH):
        p = {
            'rw_mu': rw_mu[l], 'rw_w0': rw_w0[l], 'rw_w_up': rw_w_up[l], 'rw_a0': rw_a0[l],
            'rw_a_up': rw_a_up[l], 'rw_g_up': rw_g_up[l], 'rw_k_k': rw_k_k[l], 'rw_k_a': rw_k_a[l],
            'rw_r_k': rw_r_k[l], 'rw_lnx_g': rw_lnx_g[l], 'rw_lnx_b': rw_lnx_b[l],
            'da_lambda': da_lambda[l], 'da_subln_g': da_subln_g[l], 'w_o_attn': w_o_attn[l],
            'w_o_rwkv': w_o_rwkv[l], 'w_out': w_out[l], 'ln1_g': ln1_g[l], 'ln1_b': ln1_b[l],
            'w_up': w_up[l], 'conv_w': conv_w[l], 'conv_b': conv_b[l], 'w_down': w_down[l],
            'ln2_g': ln2_g[l], 'ln2_b': ln2_b[l],
        }
        cvec = jnp.concatenate(
            [c, c_ctx[None], jnp.zeros((MOD_ROWS - dec_batch - 1, D_MODEL), F32)], axis=0)
        mod3 = _modulation(cvec, w_ada[l], b_ada[l][None]).reshape(MOD_ROWS, 1, 6 * D_MODEL)
        w_qkvr = w_in[l][:, :QKVR_COLS].astype(BF16)
        w_gates = w_in[l][:, QKVR_COLS:].astype(BF16)
        y_prompt, ctx_l = _trunk_layer(y_prompt, mod3, dec_batch, p, w_qkvr, w_gates, l)
        new_k.append(ctx_l[0])
        new_v.append(ctx_l[1])
        new_s.append(ctx_l[2])
        y_sample, _ = _trunk_layer(y_sample, mod3, None, p, w_qkvr, w_gates, l,
                                   ctx=(cache_k[:, l], cache_v[:, l], state_rwkv[:, l]))
    return (y_prompt, y_sample, jnp.stack(new_k, axis=1), jnp.stack(new_v, axis=1),
            jnp.stack(new_s, axis=1))
```

```python
import functools
import math

import jax
import jax.numpy as jnp
from jax import lax
from jax.experimental import pallas as pl
from jax.experimental.pallas import tpu as pltpu

F32 = jnp.float32
BF16 = jnp.bfloat16
HIGHEST = lax.Precision.HIGHEST

D_MODEL = 1024
GRID_W = 64
DA_HEADS = 4
DA_HEAD_DIM = 64
DA_WIDTH = DA_HEADS * 2 * DA_HEAD_DIM
ROPE_PAIRS_PER_AXIS = DA_HEAD_DIM // 4
ROPE_BASE = 10000.0
RW_HEADS = 8
RW_HEAD_DIM = 64
RW_WIDTH = RW_HEADS * RW_HEAD_DIM
DECAY_LORA = 64
AAA_LORA = 64
GATE_LORA = 128
RW_COLS = 3 * RW_WIDTH + DECAY_LORA + AAA_LORA + GATE_LORA
RW_LNX_EPS = 64e-5
QKVR_COLS = 3 * DA_WIDTH + RW_COLS
D_FF = 2816
LN_EPS = 1e-5
DEPTH = 1
ALPHA = (2.0 * DEPTH) ** 0.25

CHUNK = 64
ATTN_Q_BLOCK = 256
ROW_TILE = 512
MLP_ROW_TILE = 1024
FF_TILE = 256
MOD_COL_TILE = 768
MOD_ROWS = 16
VMEM_LIMIT = 48 * 1024 * 1024
MLP_VMEM_LIMIT = 56 * 1024 * 1024

_NN = (((1,), (0,)), ((), ()))
_NT = (((1,), (1,)), ((), ()))
_TN = (((0,), (0,)), ((), ()))


def _dot(a, b, dims=_NN, precision=None):
    return lax.dot_general(a, b, dims, precision=precision, preferred_element_type=F32)


def _layer_norm(z, g, b, eps):
    mu = jnp.mean(z, axis=-1, keepdims=True)
    zc = z - mu
    var = jnp.mean(zc * zc, axis=-1, keepdims=True)
    return zc * lax.rsqrt(var + eps) * g + b


def _mod_kernel(c_ref, w_ref, b_ref, o_ref):
    cv = c_ref[...]
    s = cv * jax.nn.sigmoid(cv)
    o_ref[...] = _dot(s, w_ref[...], precision=HIGHEST) + b_ref[...]


def _modulation(cvec, w_ada, b_ada):
    n = w_ada.shape[1]
    return pl.pallas_call(
        _mod_kernel,
        grid=(n // MOD_COL_TILE,),
        in_specs=[
            pl.BlockSpec((MOD_ROWS, D_MODEL), lambda j: (0, 0)),
            pl.BlockSpec((D_MODEL, MOD_COL_TILE), lambda j: (0, j)),
            pl.BlockSpec((1, MOD_COL_TILE), lambda j: (0, j)),
        ],
        out_specs=pl.BlockSpec((MOD_ROWS, MOD_COL_TILE), lambda j: (0, j)),
        out_shape=jax.ShapeDtypeStruct((MOD_ROWS, n), F32),
        compiler_params=pltpu.CompilerParams(
            dimension_semantics=("parallel",), vmem_limit_bytes=VMEM_LIMIT),
        name="mod",
    )(cvec, w_ada, b_ada)


def _mod_row_map(rows_per_batch, fixed_row):
    if fixed_row is not None:
        return lambda i: (fixed_row, 0, 0)
    tiles = rows_per_batch // ROW_TILE
    return lambda i: (i // tiles, 0, 0)


def _inproj_kernel(x_ref, mod_ref, w_ref, q_ref, k_ref, v_ref, rw_ref):
    mod = mod_ref[...]
    sh1 = mod[:, 0:D_MODEL]
    sc1 = mod[:, D_MODEL:2 * D_MODEL]
    h = (x_ref[...] * (1.0 + sc1) + sh1).astype(BF16)
    q_ref[...] = _dot(h, w_ref[:, 0:DA_WIDTH])
    k_ref[...] = _dot(h, w_ref[:, DA_WIDTH:2 * DA_WIDTH])
    v_ref[...] = _dot(h, w_ref[:, 2 * DA_WIDTH:3 * DA_WIDTH])
    rw_ref[...] = _dot(h, w_ref[:, 3 * DA_WIDTH:QKVR_COLS])


def _input_projection(x2d, mod3, w_qkvr, seq_len, fixed_row):
    m = x2d.shape[0]
    row = lambda i: (i, 0)
    return pl.pallas_call(
        _inproj_kernel,
        grid=(m // ROW_TILE,),
        in_specs=[
            pl.BlockSpec((ROW_TILE, D_MODEL), row),
            pl.BlockSpec((None, 1, 6 * D_MODEL), _mod_row_map(seq_len, fixed_row)),
            pl.BlockSpec((D_MODEL, QKVR_COLS), lambda i: (0, 0)),
        ],
        out_specs=[
            pl.BlockSpec((ROW_TILE, DA_WIDTH), row),
            pl.BlockSpec((ROW_TILE, DA_WIDTH), row),
            pl.BlockSpec((ROW_TILE, DA_WIDTH), row),
            pl.BlockSpec((ROW_TILE, RW_COLS), row),
        ],
        out_shape=[
            jax.ShapeDtypeStruct((m, DA_WIDTH), F32),
            jax.ShapeDtypeStruct((m, DA_WIDTH), F32),
            jax.ShapeDtypeStruct((m, DA_WIDTH), F32),
            jax.ShapeDtypeStruct((m, RW_COLS), F32),
        ],
        compiler_params=pltpu.CompilerParams(
            dimension_semantics=("parallel",), vmem_limit_bytes=VMEM_LIMIT),
        name="inproj",
    )(x2d, mod3, w_qkvr)


def _rope(x, cos, sin_signed):
    lane = lax.broadcasted_iota(jnp.int32, x.shape, 1)
    partner = jnp.where((lane & 63) < 32, pltpu.roll(x, 96, 1), pltpu.roll(x, 32, 1))
    return x * cos + partner * sin_signed


def _softmax_av(qm, keys, vals):
    scores = [_dot(qm, kg, _NT) for kg in keys]
    mx = scores[0].max(axis=-1, keepdims=True)
    for s in scores[1:]:
        mx = jnp.maximum(mx, s.max(axis=-1, keepdims=True))
    den = None
    out = None
    for s, vg in zip(scores, vals):
        p = jnp.exp(s - mx)
        d = jnp.sum(p, axis=-1, keepdims=True)
        o = _dot(p.astype(BF16), vg)
        den = d if den is None else den + d
        out = o if out is None else out + o
    return out / den


def _attn_kernel(*refs, has_ctx, seq_len, lam_init):
    if has_ctx:
        q_ref, k_ref, v_ref, kc_ref, vc_ref, cos_ref, sin_ref, lq_ref, g_ref, o_ref = refs
    else:
        q_ref, k_ref, v_ref, lq_ref, g_ref, o_ref = refs
    d = DA_HEAD_DIM
    lq = lq_ref[...]
    lam = (jnp.exp(jnp.sum(lq[0:1] * lq[1:2], axis=-1, keepdims=True))
           - jnp.exp(jnp.sum(lq[2:3] * lq[3:4], axis=-1, keepdims=True)) + lam_init)
    k = k_ref[...]
    if has_ctx:
        k = _rope(k, cos_ref[...], sin_ref[...])
    keys = [[k[:, m * d:(m + 1) * d].astype(BF16)] for m in range(2)]
    vals = [v_ref[...].astype(BF16)]
    if has_ctx:
        kc = kc_ref[...]
        for m in range(2):
            keys[m].append(kc[:, m * d:(m + 1) * d].astype(BF16))
        vals.append(vc_ref[...].astype(BF16))
    g = g_ref[...]
    for qb in range(seq_len // ATTN_Q_BLOCK):
        rows = slice(qb * ATTN_Q_BLOCK, (qb + 1) * ATTN_Q_BLOCK)
        q = q_ref[rows, :]
        if has_ctx:
            q = _rope(q, cos_ref[rows, :], sin_ref[rows, :])
        q = q * (d ** -0.5)
        o1 = _softmax_av(q[:, 0:d].astype(BF16), keys[0], vals)
        o2 = _softmax_av(q[:, d:2 * d].astype(BF16), keys[1], vals)
        o = o1 - lam * o2
        ms = jnp.mean(o * o, axis=-1, keepdims=True)
        o_ref[rows, :] = o * lax.rsqrt(ms + LN_EPS) * g * (1.0 - lam_init)


def _rope_tables(n):
    rows = n // GRID_W
    row = jnp.repeat(jnp.arange(rows, dtype=F32), GRID_W)
    col = jnp.tile(jnp.arange(GRID_W, dtype=F32), rows)
    inv = ROPE_BASE ** (-jnp.arange(ROPE_PAIRS_PER_AXIS, dtype=F32) / ROPE_PAIRS_PER_AXIS)
    ang = jnp.concatenate([row[:, None] * inv, col[:, None] * inv], -1)
    cos, sin = jnp.cos(ang), jnp.sin(ang)
    return jnp.tile(cos, (1, 4)), jnp.tile(jnp.concatenate([-sin, sin], -1), (1, 2))


def _attention(q2d, k2d, v2d, da_lambda, subln_g, batch, seq_len, layer, ctx=None):
    has_ctx = ctx is not None
    w = 2 * DA_HEAD_DIM
    head = lambda b, h: (b, h)
    const = lambda b, h: (0, 0)
    in_specs = [pl.BlockSpec((seq_len, w), head)] * 3
    args = [q2d, k2d, v2d]
    if has_ctx:
        kc, vc = ctx
        past = kc.shape[1]
        in_specs += [pl.BlockSpec((None, past, w), lambda b, h: (b, 0, h))] * 2
        in_specs += [pl.BlockSpec((seq_len, w), const)] * 2
        args += [kc, vc, *_rope_tables(seq_len)]
    in_specs += [pl.BlockSpec((4, DA_HEAD_DIM), const), pl.BlockSpec((1, w), const)]
    args += [da_lambda, subln_g]
    lam_init = 0.8 - 0.6 * math.exp(-0.3 * layer)
    return pl.pallas_call(
        functools.partial(_attn_kernel, has_ctx=has_ctx, seq_len=seq_len, lam_init=lam_init),
        grid=(batch, DA_HEADS),
        in_specs=in_specs,
        out_specs=pl.BlockSpec((seq_len, w), head),
        out_shape=jax.ShapeDtypeStruct((batch * seq_len, DA_WIDTH), F32),
        compiler_params=pltpu.CompilerParams(
            dimension_semantics=("parallel", "parallel"), vmem_limit_bytes=VMEM_LIMIT),
        name="attn",
    )(*args)


PAIR = 2 * RW_HEAD_DIM
N_PAIRS = RW_HEADS // 2
FINISH_ROWS = 256

_M_STRICT = (0, 2)
_M_INCL = (1, 3)
_M_BLOCK8, _M_OFF16, _M_OFF32, _M_OFF64, _M_EYE = 4, 5, 6, 7, 8


def _pair_masks():
    t = jnp.arange(PAIR)[:, None]
    s = jnp.arange(PAIR)[None, :]
    same = lambda n: (t // n) == (s // n)
    head = same(CHUNK)
    masks = [head & (t > s), head & (t >= s), head & (t < s), head & (t <= s),
             same(8), same(16) & ~same(8), same(32) & ~same(16), head & ~same(32), t == s]
    return jnp.stack(masks).astype(F32)


def _cumsum_matrices():
    t = jnp.arange(CHUNK)[:, None]
    s = jnp.arange(CHUNK)[None, :]
    return jnp.stack([t >= s, t <= s]).astype(BF16)


def _pair_sum_matrix():
    i = jnp.arange(PAIR)
    return ((i[:, None] // RW_HEAD_DIM) == (i[None, :] // RW_HEAD_DIM)).astype(BF16)


def _split3(x):
    hi = x.astype(BF16)
    r1 = x - hi.astype(F32)
    mid = r1.astype(BF16)
    lo = (r1 - mid.astype(F32)).astype(BF16)
    return hi, mid, lo


def _head_sums(x, e_ref):
    rows = x.shape[0]
    xs = jnp.concatenate([x[:, p * PAIR:(p + 1) * PAIR] for p in range(N_PAIRS)], axis=0)
    s = _dot(xs.astype(BF16), e_ref[...])
    n = N_PAIRS * rows
    return jnp.concatenate([s[p * rows:(p + 1) * rows] for p in range(N_PAIRS)], axis=1)


def _weave(main, side, after):
    for k, _ in enumerate(main):
        if k in after:
            next(side, None)
    for _ in side:
        pass


def _rwkv_kernel(*refs, seq_len, has_state_in, has_state_out):
    refs = list(refs)
    rw_ref = refs.pop(0)
    s0_ref = refs.pop(0) if has_state_in else None
    (mu_ref, w0_ref, wup_ref, a0_ref, aup_ref, gup_ref, kk_ref, ka_ref, rk_ref,
     lng_ref, lnb_ref, e_ref, tri_ref, m_ref) = refs[:14]
    refs = refs[14:]
    o_ref = refs.pop(0)
    so_ref = refs.pop(0) if has_state_out else None
    y_scr, bv_scr, gr_scr, s_scr, ar_scr, bk_scr, v_scr, ge_scr = refs

    C = CHUNK
    N = RW_HEAD_DIM
    W = RW_WIDTH
    nc = seq_len // C
    assert nc % 2 == 0
    mm = lambda x, y: _dot(x.astype(BF16), y.astype(BF16))
    zeros_nn = jnp.zeros((N, N), F32)
    chains = [(d, p) for d in range(2) for p in range(N_PAIRS)]

    for d, p in chains:
        if has_state_in:
            top = jnp.concatenate([s0_ref[d, 2 * p], zeros_nn], axis=1)
            bot = jnp.concatenate([zeros_nn, s0_ref[d, 2 * p + 1]], axis=1)
            s_scr[d, p] = jnp.concatenate([top, bot], axis=0)
        else:
            s_scr[d, p] = jnp.zeros((PAIR, PAIR), F32)

    first_head = lax.broadcasted_iota(jnp.int32, (C, PAIR), 1) < N

    def block_diag(x):
        return jnp.concatenate([jnp.where(first_head, x, 0.0), jnp.where(first_head, 0.0, x)], axis=0)

    def chunk_rows(i):
        return [pl.ds(pl.multiple_of(c * C, C), C) for c in (i, nc - 1 - i)]

    def prep_stages(i, slot):
        st = []
        for d, rows in enumerate(chunk_rows(i)):
            r0 = rows.start
            cur = rw_ref[rows, :]
            prow = rw_ref[pl.ds(jnp.maximum(r0 - 1, 0), 1), :] * jnp.where(r0 > 0, 1.0, 0.0)
            nrow = (rw_ref[pl.ds(jnp.minimum(r0 + C, seq_len - 1), 1), :]
                    * jnp.where(r0 + C < seq_len, 1.0, 0.0))
            rid = lax.broadcasted_iota(jnp.int32, (C, 1), 0)
            prev = jnp.where(rid == 0, prow, pltpu.roll(cur, 1, 0))
            nxt = jnp.where(rid == C - 1, nrow, pltpu.roll(cur, C - 1, 0))
            xm = cur + mu_ref[0:1, :] * (prev - cur) + mu_ref[1:2, :] * (nxt - cur)
            w_lo = xm[:, 3 * W:3 * W + DECAY_LORA]
            a_lo = xm[:, 3 * W + DECAY_LORA:3 * W + DECAY_LORA + AAA_LORA]
            st.append(dict(rows=rows, r=xm[:, 0:W], kr=xm[:, W:2 * W], vr=xm[:, 2 * W:3 * W],
                           g_lo=xm[:, 3 * W + DECAY_LORA + AAA_LORA:RW_COLS],
                           w_up=_dot(jnp.tanh(w_lo).astype(BF16), wup_ref[d]),
                           a_up=_dot(a_lo.astype(BF16), aup_ref[d])))
        yield
        for d, x in enumerate(st):
            wlog = -jax.nn.softplus(-(w0_ref[d:d + 1, :] + x['w_up'])) - 0.5
            x['log_decay'] = -jnp.exp(wlog)
            x['a'] = jax.nn.sigmoid(a0_ref[d:d + 1, :] + x['a_up'])
            x['keff'] = x['kr'] * (1.0 + (x['a'] - 1.0) * ka_ref[...])
            x['kk'] = x['kr'] * kk_ref[...]
            x['sums'] = _head_sums(
                jnp.concatenate([x['kk'] * x['kk'], x['r'] * x['keff'] * rk_ref[...]], axis=0), e_ref)
            if d == 0:
                gr_scr[x['rows'], :] = _dot(jax.nn.sigmoid(x['g_lo']).astype(BF16), gup_ref[...])
            cum = _dot(tri_ref[d], jnp.concatenate(_split3(x['log_decay']), axis=1))
            x['cum'] = cum[:, 0:W] + cum[:, W:2 * W] + cum[:, 2 * W:3 * W]
        yield
        for d, x in enumerate(st):
            kk = x['kk'] / jnp.maximum(jnp.sqrt(x['sums'][0:C]), 1e-12)
            bv_scr[d, x['rows'], :] = x['sums'][C:2 * C] * x['vr']
            cum = x['cum']
            g_in = jnp.exp(cum)
            g_inv = jnp.exp(-cum)
            a_t = -kk * jnp.exp(cum - x['log_decay'])
            r_t = x['r'] * g_in
            b_t = kk * x['a'] * g_inv
            k_t = x['keff'] * g_inv
            ge_scr[slot, d] = g_in[C - 1:C, :] if d == 0 else g_in[0:1, :]
            for p in range(N_PAIRS):
                sl = slice(p * PAIR, (p + 1) * PAIR)
                ar_scr[slot, d, p] = jnp.concatenate(
                    [block_diag(a_t[:, sl]), block_diag(r_t[:, sl])], axis=0).astype(BF16)
                bk_scr[slot, d, p] = jnp.concatenate(
                    [block_diag(b_t[:, sl]), block_diag(k_t[:, sl])], axis=0).astype(BF16)
                v_scr[slot, d, p] = block_diag(x['vr'][:, sl]).astype(BF16)
        yield

    def chain_stages(i, slot):
        P = PAIR
        n = range(len(chains))
        rows = chunk_rows(i)
        ar = [ar_scr[slot, d, p] for d, p in chains]
        bk = [bk_scr[slot, d, p] for d, p in chains]
        vb = [v_scr[slot, d, p] for d, p in chains]
        s_old = [s_scr[d, p] for d, p in chains]
        strict = [m_ref[_M_STRICT[d]] for d, _ in chains]
        incl = [m_ref[_M_INCL[d]] for d, _ in chains]
        x = [_dot(ar[i], jnp.concatenate([bk[i], s_old[i].astype(BF16)], axis=0), _NT) for i in n]
        yield
        amat = [x[i][0:P, 0:P] * strict[i] for i in n]
        a8 = [a * m_ref[_M_BLOCK8] for a in amat]
        a2 = [mm(a, a) for a in a8]
        rhs = [x[i][0:P, 2 * P:3 * P] + mm(x[i][0:P, P:2 * P] * strict[i], vb[i]) for i in n]
        yield
        a4 = [mm(a, a) for a in a2]
        ps = [m_ref[_M_EYE] + a for a in a8]
        ps = [t + mm(t, a) for t, a in zip(ps, a2)]
        yield
        ps = [t + mm(t, a) for t, a in zip(ps, a4)]
        yield
        for off in (_M_OFF16, _M_OFF32, _M_OFF64):
            ts = [mm(t, a * m_ref[off]) for t, a in zip(ps, amat)]
            yield
            ps = [t + mm(q, t) for t, q in zip(ps, ts)]
            yield
        u = [mm(ps[i], rhs[i]).astype(BF16) for i in n]
        yield
        uv = [jnp.concatenate([u[i], vb[i]], axis=0) for i in n]
        w_y = [jnp.concatenate([x[i][P:2 * P, 0:P] * incl[i], x[i][P:2 * P, P:2 * P] * incl[i]],
                               axis=1).astype(BF16) for i in n]
        y = [x[i][P:2 * P, 2 * P:3 * P] + _dot(w_y[i], uv[i]) for i in n]
        s_new = [s_old[i] + _dot(uv[i], bk[i], _TN) for i in n]
        for i, (d, p) in enumerate(chains):
            sl = slice(p * P, (p + 1) * P)
            y_scr[d, rows[d], sl] = y[i][0:C] + y[i][C:2 * C]
            s_scr[d, p] = s_new[i] * ge_scr[slot, d][:, sl]
        yield

    for _ in prep_stages(0, 0):
        pass

    def scan_body(j, carry):
        i = 2 * j
        _weave(chain_stages(i, 0), prep_stages(i + 1, 1), after=(0, 2, 4))
        _weave(chain_stages(i + 1, 1), prep_stages(jnp.minimum(i + 2, nc - 1), 0), after=(0, 2, 4))
        return carry

    lax.fori_loop(0, nc // 2, scan_body, 0)

    def finish_body(i, carry):
        rows = pl.ds(pl.multiple_of(i * FINISH_ROWS, FINISH_ROWS), FINISH_ROWS)
        y = y_scr[0, rows, :] + y_scr[1, rows, :]
        yc = y - _head_sums(y, e_ref) * (1.0 / N)
        var = _head_sums(yc * yc, e_ref) * (1.0 / N)
        yn = yc * lax.rsqrt(var + RW_LNX_EPS) * lng_ref[...] + lnb_ref[...]
        o_ref[rows, :] = (yn + bv_scr[0, rows, :] + bv_scr[1, rows, :]) * gr_scr[rows, :]
        return carry

    lax.fori_loop(0, seq_len // FINISH_ROWS, finish_body, 0)
    if has_state_out:
        for d, p in chains:
            s_pair = s_scr[d, p]
            so_ref[d, 2 * p] = s_pair[0:N, 0:N]
            so_ref[d, 2 * p + 1] = s_pair[N:PAIR, N:PAIR]


def _rwkv(rw2d, p, batch, seq_len, state_in=None, want_state=False):
    has_state_in = state_in is not None
    const2 = lambda b: (0, 0)
    const3 = lambda b: (0, 0, 0)
    state_spec = pl.BlockSpec((None, 2, RW_HEADS, RW_HEAD_DIM, RW_HEAD_DIM), lambda b: (b, 0, 0, 0, 0))
    in_specs = [pl.BlockSpec((seq_len, RW_COLS), lambda b: (b, 0))]
    args = [rw2d]
    if has_state_in:
        in_specs.append(state_spec)
        args.append(state_in)
    in_specs += [
        pl.BlockSpec((2, RW_COLS), const2),
        pl.BlockSpec((2, RW_WIDTH), const2),
        pl.BlockSpec((2, DECAY_LORA, RW_WIDTH), const3),
        pl.BlockSpec((2, RW_WIDTH), const2),
        pl.BlockSpec((2, AAA_LORA, RW_WIDTH), const3),
        pl.BlockSpec((GATE_LORA, RW_WIDTH), const2),
        pl.BlockSpec((1, RW_WIDTH), const2),
        pl.BlockSpec((1, RW_WIDTH), const2),
        pl.BlockSpec((1, RW_WIDTH), const2),
        pl.BlockSpec((1, RW_WIDTH), const2),
        pl.BlockSpec((1, RW_WIDTH), const2),
        pl.BlockSpec((PAIR, PAIR), const2),
        pl.BlockSpec((2, CHUNK, CHUNK), const3),
        pl.BlockSpec((9, PAIR, PAIR), const3),
    ]
    args += [p['rw_mu'], p['rw_w0'], p['rw_w_up'].astype(BF16), p['rw_a0'], p['rw_a_up'].astype(BF16),
             p['rw_g_up'].astype(BF16), p['rw_k_k'][None], p['rw_k_a'][None],
             p['rw_r_k'].reshape(1, RW_WIDTH), p['rw_lnx_g'][None], p['rw_lnx_b'][None],
             _pair_sum_matrix(), _cumsum_matrices(), _pair_masks()]
    out_specs = [pl.BlockSpec((seq_len, RW_WIDTH), lambda b: (b, 0))]
    out_shape = [jax.ShapeDtypeStruct((batch * seq_len, RW_WIDTH), F32)]
    if want_state:
        out_specs.append(state_spec)
        out_shape.append(jax.ShapeDtypeStruct((batch, 2, RW_HEADS, RW_HEAD_DIM, RW_HEAD_DIM), F32))
    outs = pl.pallas_call(
        functools.partial(_rwkv_kernel, seq_len=seq_len, has_state_in=has_state_in,
                          has_state_out=want_state),
        grid=(batch,),
        in_specs=in_specs,
        out_specs=out_specs,
        out_shape=out_shape,
        scratch_shapes=[
            pltpu.VMEM((2, seq_len, RW_WIDTH), F32),
            pltpu.VMEM((2, seq_len, RW_WIDTH), F32),
            pltpu.VMEM((seq_len, RW_WIDTH), F32),
            pltpu.VMEM((2, N_PAIRS, PAIR, PAIR), F32),
            pltpu.VMEM((2, 2, N_PAIRS, 2 * PAIR, PAIR), BF16),
            pltpu.VMEM((2, 2, N_PAIRS, 2 * PAIR, PAIR), BF16),
            pltpu.VMEM((2, 2, N_PAIRS, PAIR, PAIR), BF16),
            pltpu.VMEM((2, 2, 1, RW_WIDTH), F32),
        ],
        compiler_params=pltpu.CompilerParams(
            dimension_semantics=("parallel",), vmem_limit_bytes=VMEM_LIMIT),
        name="rwkv",
    )(*args)
    return outs if want_state else (outs[0], None)


def _merge_kernel(x_ref, mod_ref, oa_ref, yg_ref, wg_ref, woa_ref, wor_ref, wout_ref,
                  g_ref, b_ref, o_ref):
    mod = mod_ref[...]
    sh1 = mod[:, 0:D_MODEL]
    sc1 = mod[:, D_MODEL:2 * D_MODEL]
    g1 = mod[:, 2 * D_MODEL:3 * D_MODEL]
    x = x_ref[...]
    h = (x * (1.0 + sc1) + sh1).astype(BF16)
    gates = jax.nn.sigmoid(_dot(h, wg_ref[...]))
    att = _dot(oa_ref[...].astype(BF16), woa_ref[...])
    rwk = _dot(yg_ref[...].astype(BF16), wor_ref[...])
    merged = gates[:, 0:D_MODEL] * att + gates[:, D_MODEL:2 * D_MODEL] * rwk
    mix = _dot(merged.astype(BF16), wout_ref[...])
    o_ref[...] = _layer_norm(ALPHA * x + g1 * mix, g_ref[...], b_ref[...], LN_EPS)


def _merge(x2d, mod3, o_att, yg, w_gates, p, seq_len, fixed_row):
    m = x2d.shape[0]
    row = lambda i: (i, 0)
    const = lambda i: (0, 0)
    return pl.pallas_call(
        _merge_kernel,
        grid=(m // ROW_TILE,),
        in_specs=[
            pl.BlockSpec((ROW_TILE, D_MODEL), row),
            pl.BlockSpec((None, 1, 6 * D_MODEL), _mod_row_map(seq_len, fixed_row)),
            pl.BlockSpec((ROW_TILE, DA_WIDTH), row),
            pl.BlockSpec((ROW_TILE, RW_WIDTH), row),
            pl.BlockSpec((D_MODEL, 2 * D_MODEL), const),
            pl.BlockSpec((DA_WIDTH, D_MODEL), const),
            pl.BlockSpec((RW_WIDTH, D_MODEL), const),
            pl.BlockSpec((D_MODEL, D_MODEL), const),
            pl.BlockSpec((1, D_MODEL), const),
            pl.BlockSpec((1, D_MODEL), const),
        ],
        out_specs=pl.BlockSpec((ROW_TILE, D_MODEL), row),
        out_shape=jax.ShapeDtypeStruct((m, D_MODEL), F32),
        compiler_params=pltpu.CompilerParams(
            dimension_semantics=("parallel",), vmem_limit_bytes=VMEM_LIMIT),
        name="merge",
    )(x2d, mod3, o_att, yg, w_gates, p['w_o_attn'].astype(BF16), p['w_o_rwkv'].astype(BF16),
      p['w_out'].astype(BF16), p['ln1_g'][None], p['ln1_b'][None])


def _mlp_kernel(x_ref, mod_ref, wup_ref, cw_ref, cb_ref, wd_ref, g_ref, b_ref, o_ref,
                h_scr, act_scr, *, seq_len):
    mod = mod_ref[...]
    sh2 = mod[:, 3 * D_MODEL:4 * D_MODEL]
    sc2 = mod[:, 4 * D_MODEL:5 * D_MODEL]
    g2 = mod[:, 5 * D_MODEL:6 * D_MODEL]
    h_scr[...] = (x_ref[...] * (1.0 + sc2) + sh2).astype(BF16)
    rows = x_ref.shape[0]
    pos = lax.broadcasted_iota(jnp.int32, (rows, 1), 0) & (seq_len - 1)
    first = pos == 0
    last = pos == seq_len - 1
    for j in range(D_FF // FF_TILE):
        cols = slice(j * FF_TILE, (j + 1) * FF_TILE)
        h = h_scr[...]
        u = _dot(h, wup_ref[:, cols])
        val = _dot(h, wup_ref[:, D_FF + j * FF_TILE:D_FF + (j + 1) * FF_TILE])
        prev = jnp.where(first, 0.0, pltpu.roll(u, 1, 0))
        nxt = jnp.where(last, 0.0, pltpu.roll(u, rows - 1, 0))
        cw = cw_ref[:, cols]
        u = prev * cw[0:1, :] + u * cw[1:2, :] + nxt * cw[2:3, :] + cb_ref[:, cols]
        act_scr[:, cols] = (jax.nn.gelu(u) * val).astype(BF16)
    f = _dot(act_scr[...], wd_ref[...])
    o_ref[...] = _layer_norm(ALPHA * x_ref[...] + g2 * f, g_ref[...], b_ref[...], LN_EPS)


def _mlp(x2d, mod3, p, seq_len, fixed_row):
    m = x2d.shape[0]
    assert seq_len & (seq_len - 1) == 0 and MLP_ROW_TILE % seq_len == 0
    if fixed_row is not None:
        mod_map = lambda i: (fixed_row, 0, 0)
    else:
        mod_map = lambda i: (i * MLP_ROW_TILE // seq_len, 0, 0)
    row = lambda i: (i, 0)
    const = lambda i: (0, 0)
    resident = pl.Buffered(1)
    return pl.pallas_call(
        functools.partial(_mlp_kernel, seq_len=seq_len),
        grid=(m // MLP_ROW_TILE,),
        in_specs=[
            pl.BlockSpec((MLP_ROW_TILE, D_MODEL), row),
            pl.BlockSpec((None, 1, 6 * D_MODEL), mod_map),
            pl.BlockSpec((D_MODEL, 2 * D_FF), const, pipeline_mode=resident),
            pl.BlockSpec((3, D_FF), const),
            pl.BlockSpec((1, D_FF), const),
            pl.BlockSpec((D_FF, D_MODEL), const, pipeline_mode=resident),
            pl.BlockSpec((1, D_MODEL), const),
            pl.BlockSpec((1, D_MODEL), const),
        ],
        out_specs=pl.BlockSpec((MLP_ROW_TILE, D_MODEL), row),
        out_shape=jax.ShapeDtypeStruct((m, D_MODEL), F32),
        scratch_shapes=[
            pltpu.VMEM((MLP_ROW_TILE, D_MODEL), BF16),
            pltpu.VMEM((MLP_ROW_TILE, D_FF), BF16),
        ],
        compiler_params=pltpu.CompilerParams(
            dimension_semantics=("parallel",), vmem_limit_bytes=MLP_VMEM_LIMIT),
        name="mlp",
    )(x2d, mod3, p['w_up'].astype(BF16), p['conv_w'], p['conv_b'][None], p['w_down'].astype(BF16),
      p['ln2_g'][None], p['ln2_b'][None])


def _trunk_layer(x, mod3, fixed_row, p, w_qkvr, w_gates, layer, ctx=None):
    batch, seq_len, _ = x.shape
    x2d = x.reshape(batch * seq_len, D_MODEL)
    q, k, v, rw = _input_projection(x2d, mod3, w_qkvr, seq_len, fixed_row)
    if ctx is None:
        o_att = _attention(q, k, v, p['da_lambda'], p['da_subln_g'][None], batch, seq_len, layer)
        yg, state = _rwkv(rw, p, batch, seq_len, want_state=True)
    else:
        k_ctx, v_ctx, s_ctx = ctx
        past = k_ctx.shape[1]
        o_att = _attention(q, k, v, p['da_lambda'], p['da_subln_g'][None], batch, seq_len, layer,
                           ctx=(k_ctx.reshape(batch, past, DA_WIDTH), v_ctx.reshape(batch, past, DA_WIDTH)))
        yg, state = _rwkv(rw, p, batch, seq_len, state_in=s_ctx)
    x1 = _merge(x2d, mod3, o_att, yg, w_gates, p, seq_len, fixed_row)
    y = _mlp(x1, mod3, p, seq_len, fixed_row)
    new_ctx = None
    if ctx is None:
        new_ctx = (k.reshape(batch, seq_len, DA_HEADS, 2, DA_HEAD_DIM),
                   v.reshape(batch, seq_len, DA_HEADS, 2 * DA_HEAD_DIM), state)
    return y.reshape(batch, seq_len, D_MODEL), new_ctx


def kernel(x_prompt, x_sample, cache_k, cache_v, state_rwkv, c, c_ctx, w_ada, b_ada, w_in, rw_mu, rw_w0, rw_w_up, rw_a0, rw_a_up, rw_g_up, rw_k_k, rw_k_a, rw_r_k, rw_lnx_g, rw_lnx_b, da_lambda, da_subln_g, w_o_attn, w_o_rwkv, w_out, ln1_g, ln1_b, w_up, conv_w, conv_b, w_down, ln2_g, ln2_b):
    dec_batch = x_sample.shape[0]
    assert dec_batch < MOD_ROWS
    y_prompt, y_sample = x_prompt, x_sample
    new_k, new_v, new_s = [], [], []
    for l in range(DEPTH):
        p = {
            'rw_mu': rw_mu[l], 'rw_w0': rw_w0[l], 'rw_w_up': rw_w_up[l], 'rw_a0': rw_a0[l],
            'rw_a_up': rw_a_up[l], 'rw_g_up': rw_g_up[l], 'rw_k_k': rw_k_k[l], 'rw_k_a': rw_k_a[l],
            'rw_r_k': rw_r_k[l], 'rw_lnx_g': rw_lnx_g[l], 'rw_lnx_b': rw_lnx_b[l],
            'da_lambda': da_lambda[l], 'da_subln_g': da_subln_g[l], 'w_o_attn': w_o_attn[l],
            'w_o_rwkv': w_o_rwkv[l], 'w_out': w_out[l], 'ln1_g': ln1_g[l], 'ln1_b': ln1_b[l],
            'w_up': w_up[l], 'conv_w': conv_w[l], 'conv_b': conv_b[l], 'w_down': w_down[l],
            'ln2_g': ln2_g[l], 'ln2_b': ln2_b[l],
        }
        cvec = jnp.concatenate(
            [c, c_ctx[None], jnp.zeros((MOD_ROWS - dec_batch - 1, D_MODEL), F32)], axis=0)
        mod3 = _modulation(cvec, w_ada[l], b_ada[l][None]).reshape(MOD_ROWS, 1, 6 * D_MODEL)
        w_qkvr = w_in[l][:, :QKVR_COLS].astype(BF16)
        w_gates = w_in[l][:, QKVR_COLS:].astype(BF16)
        y_prompt, ctx_l = _trunk_layer(y_prompt, mod3, dec_batch, p, w_qkvr, w_gates, l)
        new_k.append(ctx_l[0])
        new_v.append(ctx_l[1])
        new_s.append(ctx_l[2])
        y_sample, _ = _trunk_layer(y_sample, mod3, None, p, w_qkvr, w_gates, l,
                                   ctx=(cache_k[:, l], cache_v[:, l], state_rwkv[:, l]))
    return (y_prompt, y_sample, jnp.stack(new_k, axis=1), jnp.stack(new_v, axis=1),
            jnp.stack(new_s, axis=1))
```

```python
import functools
import math

import jax
import jax.numpy as jnp
from jax import lax
from jax.experimental import pallas as pl
from jax.experimental.pallas import tpu as pltpu

F32 = jnp.float32
BF16 = jnp.bfloat16
HIGHEST = lax.Precision.HIGHEST

D_MODEL = 1024
GRID_W = 64
DA_HEADS = 4
DA_HEAD_DIM = 64
DA_WIDTH = DA_HEADS * 2 * DA_HEAD_DIM
ROPE_PAIRS_PER_AXIS = DA_HEAD_DIM // 4
ROPE_BASE = 10000.0
RW_HEADS = 8
RW_HEAD_DIM = 64
RW_WIDTH = RW_HEADS * RW_HEAD_DIM
DECAY_LORA = 64
AAA_LORA = 64
GATE_LORA = 128
RW_COLS = 3 * RW_WIDTH + DECAY_LORA + AAA_LORA + GATE_LORA
RW_LNX_EPS = 64e-5
QKVR_COLS = 3 * DA_WIDTH + RW_COLS
D_FF = 2816
LN_EPS = 1e-5
DEPTH = 1
ALPHA = (2.0 * DEPTH) ** 0.25

CHUNK = 64
ATTN_Q_BLOCK = 256
ROW_TILE = 512
MLP_ROW_TILE = 1024
FF_TILE = 256
MOD_COL_TILE = 768
MOD_ROWS = 16
VMEM_LIMIT = 48 * 1024 * 1024
MLP_VMEM_LIMIT = 56 * 1024 * 1024

_NN = (((1,), (0,)), ((), ()))
_NT = (((1,), (1,)), ((), ()))
_TN = (((0,), (0,)), ((), ()))


def _dot(a, b, dims=_NN, precision=None):
    return lax.dot_general(a, b, dims, precision=precision, preferred_element_type=F32)


def _layer_norm(z, g, b, eps):
    mu = jnp.mean(z, axis=-1, keepdims=True)
    zc = z - mu
    var = jnp.mean(zc * zc, axis=-1, keepdims=True)
    return zc * lax.rsqrt(var + eps) * g + b


def _mod_kernel(c_ref, w_ref, b_ref, o_ref):
    cv = c_ref[...]
    s = cv * jax.nn.sigmoid(cv)
    o_ref[...] = _dot(s, w_ref[...], precision=HIGHEST) + b_ref[...]


def _modulation(cvec, w_ada, b_ada):
    n = w_ada.shape[1]
    return pl.pallas_call(
        _mod_kernel,
        grid=(n // MOD_COL_TILE,),
        in_specs=[
            pl.BlockSpec((MOD_ROWS, D_MODEL), lambda j: (0, 0)),
            pl.BlockSpec((D_MODEL, MOD_COL_TILE), lambda j: (0, j)),
            pl.BlockSpec((1, MOD_COL_TILE), lambda j: (0, j)),
        ],
        out_specs=pl.BlockSpec((MOD_ROWS, MOD_COL_TILE), lambda j: (0, j)),
        out_shape=jax.ShapeDtypeStruct((MOD_ROWS, n), F32),
        compiler_params=pltpu.CompilerParams(
            dimension_semantics=("parallel",), vmem_limit_bytes=VMEM_LIMIT),
        name="mod",
    )(cvec, w_ada, b_ada)


def _mod_row_map(rows_per_batch, fixed_row):
    if fixed_row is not None:
        return lambda i: (fixed_row, 0, 0)
    tiles = rows_per_batch // ROW_TILE
    return lambda i: (i // tiles, 0, 0)


def _inproj_kernel(x_ref, mod_ref, w_ref, q_ref, k_ref, v_ref, rw_ref):
    mod = mod_ref[...]
    sh1 = mod[:, 0:D_MODEL]
    sc1 = mod[:, D_MODEL:2 * D_MODEL]
    h = (x_ref[...] * (1.0 + sc1) + sh1).astype(BF16)
    q_ref[...] = _dot(h, w_ref[:, 0:DA_WIDTH])
    k_ref[...] = _dot(h, w_ref[:, DA_WIDTH:2 * DA_WIDTH])
    v_ref[...] = _dot(h, w_ref[:, 2 * DA_WIDTH:3 * DA_WIDTH])
    rw_ref[...] = _dot(h, w_ref[:, 3 * DA_WIDTH:QKVR_COLS])


def _input_projection(x2d, mod3, w_qkvr, seq_len, fixed_row):
    m = x2d.shape[0]
    row = lambda i: (i, 0)
    return pl.pallas_call(
        _inproj_kernel,
        grid=(m // ROW_TILE,),
        in_specs=[
            pl.BlockSpec((ROW_TILE, D_MODEL), row),
            pl.BlockSpec((None, 1, 6 * D_MODEL), _mod_row_map(seq_len, fixed_row)),
            pl.BlockSpec((D_MODEL, QKVR_COLS), lambda i: (0, 0)),
        ],
        out_specs=[
            pl.BlockSpec((ROW_TILE, DA_WIDTH), row),
            pl.BlockSpec((ROW_TILE, DA_WIDTH), row),
            pl.BlockSpec((ROW_TILE, DA_WIDTH), row),
            pl.BlockSpec((ROW_TILE, RW_COLS), row),
        ],
        out_shape=[
            jax.ShapeDtypeStruct((m, DA_WIDTH), F32),
            jax.ShapeDtypeStruct((m, DA_WIDTH), F32),
            jax.ShapeDtypeStruct((m, DA_WIDTH), F32),
            jax.ShapeDtypeStruct((m, RW_COLS), F32),
        ],
        compiler_params=pltpu.CompilerParams(
            dimension_semantics=("parallel",), vmem_limit_bytes=VMEM_LIMIT),
        name="inproj",
    )(x2d, mod3, w_qkvr)


def _rope(x, cos, sin_signed):
    lane = lax.broadcasted_iota(jnp.int32, x.shape, 1)
    partner = jnp.where((lane & 63) < 32, pltpu.roll(x, 96, 1), pltpu.roll(x, 32, 1))
    return x * cos + partner * sin_signed


def _softmax_av(qm, keys, vals):
    scores = [_dot(qm, kg, _NT) for kg in keys]
    mx = scores[0].max(axis=-1, keepdims=True)
    for s in scores[1:]:
        mx = jnp.maximum(mx, s.max(axis=-1, keepdims=True))
    den = None
    out = None
    for s, vg in zip(scores, vals):
        p = jnp.exp(s - mx)
        d = jnp.sum(p, axis=-1, keepdims=True)
        o = _dot(p.astype(BF16), vg)
        den = d if den is None else den + d
        out = o if out is None else out + o
    return out / den


def _attn_kernel(*refs, has_ctx, seq_len, lam_init):
    if has_ctx:
        q_ref, k_ref, v_ref, kc_ref, vc_ref, cos_ref, sin_ref, lq_ref, g_ref, o_ref = refs
    else:
        q_ref, k_ref, v_ref, lq_ref, g_ref, o_ref = refs
    d = DA_HEAD_DIM
    lq = lq_ref[...]
    lam = (jnp.exp(jnp.sum(lq[0:1] * lq[1:2], axis=-1, keepdims=True))
           - jnp.exp(jnp.sum(lq[2:3] * lq[3:4], axis=-1, keepdims=True)) + lam_init)
    k = k_ref[...]
    if has_ctx:
        k = _rope(k, cos_ref[...], sin_ref[...])
    keys = [[k[:, m * d:(m + 1) * d].astype(BF16)] for m in range(2)]
    vals = [v_ref[...].astype(BF16)]
    if has_ctx:
        kc = kc_ref[...]
        for m in range(2):
            keys[m].append(kc[:, m * d:(m + 1) * d].astype(BF16))
        vals.append(vc_ref[...].astype(BF16))
    g = g_ref[...]
    for qb in range(seq_len // ATTN_Q_BLOCK):
        rows = slice(qb * ATTN_Q_BLOCK, (qb + 1) * ATTN_Q_BLOCK)
        q = q_ref[rows, :]
        if has_ctx:
            q = _rope(q, cos_ref[rows, :], sin_ref[rows, :])
        q = q * (d ** -0.5)
        o1 = _softmax_av(q[:, 0:d].astype(BF16), keys[0], vals)
        o2 = _softmax_av(q[:, d:2 * d].astype(BF16), keys[1], vals)
        o = o1 - lam * o2
        ms = jnp.mean(o * o, axis=-1, keepdims=True)
        o_ref[rows, :] = o * lax.rsqrt(ms + LN_EPS) * g * (1.0 - lam_init)


def _rope_tables(n):
    rows = n // GRID_W
    row = jnp.repeat(jnp.arange(rows, dtype=F32), GRID_W)
    col = jnp.tile(jnp.arange(GRID_W, dtype=F32), rows)
    inv = ROPE_BASE ** (-jnp.arange(ROPE_PAIRS_PER_AXIS, dtype=F32) / ROPE_PAIRS_PER_AXIS)
    ang = jnp.concatenate([row[:, None] * inv, col[:, None] * inv], -1)
    cos, sin = jnp.cos(ang), jnp.sin(ang)
    return jnp.tile(cos, (1, 4)), jnp.tile(jnp.concatenate([-sin, sin], -1), (1, 2))


def _attention(q2d, k2d, v2d, da_lambda, subln_g, batch, seq_len, layer, ctx=None):
    has_ctx = ctx is not None
    w = 2 * DA_HEAD_DIM
    head = lambda b, h: (b, h)
    const = lambda b, h: (0, 0)
    in_specs = [pl.BlockSpec((seq_len, w), head)] * 3
    args = [q2d, k2d, v2d]
    if has_ctx:
        kc, vc = ctx
        past = kc.shape[1]
        in_specs += [pl.BlockSpec((None, past, w), lambda b, h: (b, 0, h))] * 2
        in_specs += [pl.BlockSpec((seq_len, w), const)] * 2
        args += [kc, vc, *_rope_tables(seq_len)]
    in_specs += [pl.BlockSpec((4, DA_HEAD_DIM), const), pl.BlockSpec((1, w), const)]
    args += [da_lambda, subln_g]
    lam_init = 0.8 - 0.6 * math.exp(-0.3 * layer)
    return pl.pallas_call(
        functools.partial(_attn_kernel, has_ctx=has_ctx, seq_len=seq_len, lam_init=lam_init),
        grid=(batch, DA_HEADS),
        in_specs=in_specs,
        out_specs=pl.BlockSpec((seq_len, w), head),
        out_shape=jax.ShapeDtypeStruct((batch * seq_len, DA_WIDTH), F32),
        compiler_params=pltpu.CompilerParams(
            dimension_semantics=("parallel", "parallel"), vmem_limit_bytes=VMEM_LIMIT),
        name="attn",
    )(*args)


PAIR = 2 * RW_HEAD_DIM
N_PAIRS = RW_HEADS // 2
FINISH_ROWS = 256
RWKV_GROUP_ROWS = 1024

_M_STRICT = (0, 2)
_M_INCL = (1, 3)
_M_BLOCK8, _M_OFF16, _M_OFF32, _M_OFF64, _M_EYE = 4, 5, 6, 7, 8


def _pair_masks():
    t = jnp.arange(PAIR)[:, None]
    s = jnp.arange(PAIR)[None, :]
    same = lambda n: (t // n) == (s // n)
    head = same(CHUNK)
    masks = [head & (t > s), head & (t >= s), head & (t < s), head & (t <= s),
             same(8), same(16) & ~same(8), same(32) & ~same(16), head & ~same(32), t == s]
    return jnp.stack(masks).astype(BF16)


def _cumsum_matrices():
    t = jnp.arange(CHUNK)[:, None]
    s = jnp.arange(CHUNK)[None, :]
    return jnp.stack([t >= s, t <= s]).astype(BF16)


def _pair_sum_matrix():
    i = jnp.arange(PAIR)
    return ((i[:, None] // RW_HEAD_DIM) == (i[None, :] // RW_HEAD_DIM)).astype(BF16)


def _split3(x):
    hi = x.astype(BF16)
    r1 = x - hi.astype(F32)
    mid = r1.astype(BF16)
    lo = (r1 - mid.astype(F32)).astype(BF16)
    return hi, mid, lo


def _head_sums(x, e_ref):
    rows = x.shape[0]
    xs = jnp.concatenate([x[:, p * PAIR:(p + 1) * PAIR] for p in range(N_PAIRS)], axis=0)
    s = _dot(xs.astype(BF16), e_ref[...])
    n = N_PAIRS * rows
    return jnp.concatenate([s[p * rows:(p + 1) * rows] for p in range(N_PAIRS)], axis=1)


def _run_schedule(order, **stages):
    for name in order:
        next(stages[name], None)
    for gen in stages.values():
        for _ in gen:
            pass


_STEP_ORDER = "C B A B C B A B C B B A B B B B".split()


def _rwkv_kernel(*refs, seq_len, group, has_state_in, has_state_out):
    refs = list(refs)
    rw_ref = refs.pop(0)
    s0_ref = refs.pop(0) if has_state_in else None
    (mu_ref, w0_ref, wup_ref, a0_ref, aup_ref, gup_ref, kk_ref, ka_ref, rk_ref,
     lng_ref, lnb_ref, e_ref, tri_ref, m_ref) = refs[:14]
    refs = refs[14:]
    o_ref = refs.pop(0)
    so_ref = refs.pop(0) if has_state_out else None
    (y_scr, bv_scr, gr_scr, s_scr, sio_scr, ar_scr, bk_scr, v_scr, ge_scr,
     t_scr, wy_scr, rl_scr) = refs

    C = CHUNK
    N = RW_HEAD_DIM
    W = RW_WIDTH
    P = PAIR
    nc = seq_len // C
    n_steps = group * nc
    total_rows = group * seq_len
    assert nc & (nc - 1) == 0 and n_steps % 2 == 0
    mm = lambda x, y: _dot(x.astype(BF16), y.astype(BF16))
    mmb = lambda x, y: _dot(x, y).astype(BF16)
    zeros_nn = jnp.zeros((N, N), F32)
    chains = [(d, p) for d in range(2) for p in range(N_PAIRS)]
    n = range(len(chains))

    if has_state_in:
        for g in range(group):
            for d, p in chains:
                top = jnp.concatenate([s0_ref[g, d, 2 * p], zeros_nn], axis=1)
                bot = jnp.concatenate([zeros_nn, s0_ref[g, d, 2 * p + 1]], axis=1)
                sio_scr[g, d, p] = jnp.concatenate([top, bot], axis=0)
    for d, p in chains:
        s_scr[d, p] = jnp.zeros((P, P), F32)

    first_head = lax.broadcasted_iota(jnp.int32, (C, P), 1) < N

    def block_diag(x):
        return jnp.concatenate([jnp.where(first_head, x, 0.0), jnp.where(first_head, 0.0, x)], axis=0)

    def step_chunks(i):
        i = jnp.minimum(i, n_steps - 1)
        g = i // nc
        il = i % nc
        return g, il, (il, nc - 1 - il)

    def prep_stages(i, slot):
        g, _, cs = step_chunks(i)
        st = []
        for d, c in enumerate(cs):
            r0 = pl.multiple_of(g * seq_len + c * C, C)
            rows = pl.ds(r0, C)
            cur = rw_ref[rows, :]
            prow = rw_ref[pl.ds(jnp.maximum(r0 - 1, 0), 1), :] * jnp.where(c > 0, 1.0, 0.0)
            nrow = (rw_ref[pl.ds(jnp.minimum(r0 + C, total_rows - 1), 1), :]
                    * jnp.where(c < nc - 1, 1.0, 0.0))
            rid = lax.broadcasted_iota(jnp.int32, (C, 1), 0)
            prev = jnp.where(rid == 0, prow, pltpu.roll(cur, 1, 0))
            nxt = jnp.where(rid == C - 1, nrow, pltpu.roll(cur, C - 1, 0))
            xm = cur + mu_ref[0:1, :] * (prev - cur) + mu_ref[1:2, :] * (nxt - cur)
            w_lo = xm[:, 3 * W:3 * W + DECAY_LORA]
            a_lo = xm[:, 3 * W + DECAY_LORA:3 * W + DECAY_LORA + AAA_LORA]
            st.append(dict(rows=rows, r=xm[:, 0:W], kr=xm[:, W:2 * W], vr=xm[:, 2 * W:3 * W],
                           g_lo=xm[:, 3 * W + DECAY_LORA + AAA_LORA:RW_COLS],
                           w_up=_dot(jnp.tanh(w_lo).astype(BF16), wup_ref[d]),
                           a_up=_dot(a_lo.astype(BF16), aup_ref[d])))
        yield
        for d, x in enumerate(st):
            wlog = -jax.nn.softplus(-(w0_ref[d:d + 1, :] + x['w_up'])) - 0.5
            x['log_decay'] = -jnp.exp(wlog)
            x['a'] = jax.nn.sigmoid(a0_ref[d:d + 1, :] + x['a_up'])
            x['keff'] = x['kr'] * (1.0 + (x['a'] - 1.0) * ka_ref[...])
            x['kk'] = x['kr'] * kk_ref[...]
            x['sums'] = _head_sums(
                jnp.concatenate([x['kk'] * x['kk'], x['r'] * x['keff'] * rk_ref[...]], axis=0), e_ref)
            if d == 0:
                gr_scr[x['rows'], :] = _dot(jax.nn.sigmoid(x['g_lo']).astype(BF16), gup_ref[...])
            cum = _dot(tri_ref[d], jnp.concatenate(_split3(x['log_decay']), axis=1))
            x['cum'] = cum[:, 0:W] + cum[:, W:2 * W] + cum[:, 2 * W:3 * W]
        yield
        for d, x in enumerate(st):
            kk = x['kk'] / jnp.maximum(jnp.sqrt(x['sums'][0:C]), 1e-12)
            bv_scr[d, x['rows'], :] = x['sums'][C:2 * C] * x['vr']
            cum = x['cum']
            g_in = jnp.exp(cum)
            g_inv = jnp.exp(-cum)
            a_t = -kk * jnp.exp(cum - x['log_decay'])
            r_t = x['r'] * g_in
            b_t = kk * x['a'] * g_inv
            k_t = x['keff'] * g_inv
            ge_scr[slot, d] = g_in[C - 1:C, :] if d == 0 else g_in[0:1, :]
            for p in range(N_PAIRS):
                sl = slice(p * P, (p + 1) * P)
                ar_scr[slot, d, p] = jnp.concatenate(
                    [block_diag(a_t[:, sl]), block_diag(r_t[:, sl])], axis=0).astype(BF16)
                bk_scr[slot, d, p] = jnp.concatenate(
                    [block_diag(b_t[:, sl]), block_diag(k_t[:, sl])], axis=0).astype(BF16)
                v_scr[slot, d, p] = block_diag(x['vr'][:, sl]).astype(BF16)
        yield

    def local_stages(slot):
        ar = [ar_scr[slot, d, p] for d, p in chains]
        bk = [bk_scr[slot, d, p] for d, p in chains]
        vb = [v_scr[slot, d, p] for d, p in chains]
        strict = [m_ref[_M_STRICT[d]] for d, _ in chains]
        incl = [m_ref[_M_INCL[d]] for d, _ in chains]
        x = [_dot(ar[i], bk[i], _NT).astype(BF16) for i in n]
        yield
        amat = [x[i][0:P, 0:P] * strict[i] for i in n]
        a8 = [a * m_ref[_M_BLOCK8] for a in amat]
        a2 = [mmb(a, a) for a in a8]
        for i, (d, p) in enumerate(chains):
            rl_scr[slot, d, p] = _dot(x[i][0:P, P:2 * P] * strict[i], vb[i])
            wy_scr[slot, d, p] = jnp.concatenate(
                [x[i][P:2 * P, 0:P] * incl[i], x[i][P:2 * P, P:2 * P] * incl[i]], axis=1)
        yield
        a4 = [mmb(a, a) for a in a2]
        ps = [m_ref[_M_EYE] + a for a in a8]
        ps = [t + mmb(t, a) for t, a in zip(ps, a2)]
        yield
        ps = [t + mmb(t, a) for t, a in zip(ps, a4)]
        yield
        for off in (_M_OFF16, _M_OFF32, _M_OFF64):
            ts = [mmb(t, a * m_ref[off]) for t, a in zip(ps, amat)]
            yield
            ps = [t + mmb(q, t) for t, q in zip(ps, ts)]
            if off == _M_OFF64:
                for i, (d, p) in enumerate(chains):
                    t_scr[slot, d, p] = ps[i]
            yield

    def state_stages(i, slot):
        g, il, cs = step_chunks(i)
        ar = [ar_scr[slot, d, p] for d, p in chains]
        bk = [bk_scr[slot, d, p] for d, p in chains]
        vb = [v_scr[slot, d, p] for d, p in chains]
        g_end = [ge_scr[slot, d] for d in range(2)]
        if has_state_in:
            s_old = [jnp.where(il == 0, sio_scr[g, d, p], s_scr[d, p]) for d, p in chains]
        else:
            s_old = [jnp.where(il == 0, 0.0, s_scr[d, p]) for d, p in chains]
        xs = [_dot(ar[i], s_old[i].astype(BF16), _NT) for i in n]
        yield
        u = [mm(t_scr[slot, d, p], xs[i][0:P] + rl_scr[slot, d, p]).astype(BF16)
             for i, (d, p) in enumerate(chains)]
        yield
        uv = [jnp.concatenate([u[i], vb[i]], axis=0) for i in n]
        y = [xs[i][P:2 * P] + _dot(wy_scr[slot, d, p], uv[i]) for i, (d, p) in enumerate(chains)]
        s_new = [s_old[i] + _dot(uv[i], bk[i], _TN) for i in n]
        for i, (d, p) in enumerate(chains):
            sl = slice(p * P, (p + 1) * P)
            rows = pl.ds(pl.multiple_of(g * seq_len + cs[d] * C, C), C)
            y_scr[d, rows, sl] = y[i][0:C] + y[i][C:2 * C]
            s_end = s_new[i] * g_end[d][:, sl]
            s_scr[d, p] = s_end
            if has_state_out:
                sio_scr[g, d, p] = s_end
        yield

    for _ in prep_stages(0, 0):
        pass
    _run_schedule("B A B A B A".split(), A=prep_stages(1, 1), B=local_stages(0))

    def scan_body(j, carry):
        i = 2 * j
        _run_schedule(_STEP_ORDER, C=state_stages(i, 0), B=local_stages(1), A=prep_stages(i + 2, 0))
        _run_schedule(_STEP_ORDER, C=state_stages(i + 1, 1), B=local_stages(0), A=prep_stages(i + 3, 1))
        return carry

    lax.fori_loop(0, n_steps // 2, scan_body, 0)

    def finish_body(i, carry):
        rows = pl.ds(pl.multiple_of(i * FINISH_ROWS, FINISH_ROWS), FINISH_ROWS)
        y = y_scr[0, rows, :] + y_scr[1, rows, :]
        yc = y - _head_sums(y, e_ref) * (1.0 / N)
        var = _head_sums(yc * yc, e_ref) * (1.0 / N)
        yn = yc * lax.rsqrt(var + RW_LNX_EPS) * lng_ref[...] + lnb_ref[...]
        o_ref[rows, :] = (yn + bv_scr[0, rows, :] + bv_scr[1, rows, :]) * gr_scr[rows, :]
        return carry

    lax.fori_loop(0, total_rows // FINISH_ROWS, finish_body, 0)
    if has_state_out:
        for g in range(group):
            for d, p in chains:
                s_pair = sio_scr[g, d, p]
                so_ref[g, d, 2 * p] = s_pair[0:N, 0:N]
                so_ref[g, d, 2 * p + 1] = s_pair[N:P, N:P]


def _rwkv(rw2d, p, batch, seq_len, state_in=None, want_state=False):
    has_state_in = state_in is not None
    assert not (has_state_in and want_state)
    group = max(1, RWKV_GROUP_ROWS // seq_len)
    assert batch % group == 0
    rows = group * seq_len
    const2 = lambda b: (0, 0)
    const3 = lambda b: (0, 0, 0)
    state_spec = pl.BlockSpec((group, 2, RW_HEADS, RW_HEAD_DIM, RW_HEAD_DIM), lambda b: (b, 0, 0, 0, 0))
    in_specs = [pl.BlockSpec((rows, RW_COLS), lambda b: (b, 0))]
    args = [rw2d]
    if has_state_in:
        in_specs.append(state_spec)
        args.append(state_in)
    in_specs += [
        pl.BlockSpec((2, RW_COLS), const2),
        pl.BlockSpec((2, RW_WIDTH), const2),
        pl.BlockSpec((2, DECAY_LORA, RW_WIDTH), const3),
        pl.BlockSpec((2, RW_WIDTH), const2),
        pl.BlockSpec((2, AAA_LORA, RW_WIDTH), const3),
        pl.BlockSpec((GATE_LORA, RW_WIDTH), const2),
        pl.BlockSpec((1, RW_WIDTH), const2),
        pl.BlockSpec((1, RW_WIDTH), const2),
        pl.BlockSpec((1, RW_WIDTH), const2),
        pl.BlockSpec((1, RW_WIDTH), const2),
        pl.BlockSpec((1, RW_WIDTH), const2),
        pl.BlockSpec((PAIR, PAIR), const2),
        pl.BlockSpec((2, CHUNK, CHUNK), const3),
        pl.BlockSpec((9, PAIR, PAIR), const3),
    ]
    args += [p['rw_mu'], p['rw_w0'], p['rw_w_up'].astype(BF16), p['rw_a0'], p['rw_a_up'].astype(BF16),
             p['rw_g_up'].astype(BF16), p['rw_k_k'][None], p['rw_k_a'][None],
             p['rw_r_k'].reshape(1, RW_WIDTH), p['rw_lnx_g'][None], p['rw_lnx_b'][None],
             _pair_sum_matrix(), _cumsum_matrices(), _pair_masks()]
    out_specs = [pl.BlockSpec((rows, RW_WIDTH), lambda b: (b, 0))]
    out_shape = [jax.ShapeDtypeStruct((batch * seq_len, RW_WIDTH), F32)]
    if want_state:
        out_specs.append(state_spec)
        out_shape.append(jax.ShapeDtypeStruct((batch, 2, RW_HEADS, RW_HEAD_DIM, RW_HEAD_DIM), F32))
    per_chain = (2, 2, N_PAIRS)
    outs = pl.pallas_call(
        functools.partial(_rwkv_kernel, seq_len=seq_len, group=group, has_state_in=has_state_in,
                          has_state_out=want_state),
        grid=(batch // group,),
        in_specs=in_specs,
        out_specs=out_specs,
        out_shape=out_shape,
        scratch_shapes=[
            pltpu.VMEM((2, rows, RW_WIDTH), F32),
            pltpu.VMEM((2, rows, RW_WIDTH), F32),
            pltpu.VMEM((rows, RW_WIDTH), F32),
            pltpu.VMEM((2, N_PAIRS, PAIR, PAIR), F32),
            pltpu.VMEM((group, 2, N_PAIRS, PAIR, PAIR), F32),
            pltpu.VMEM(per_chain + (2 * PAIR, PAIR), BF16),
            pltpu.VMEM(per_chain + (2 * PAIR, PAIR), BF16),
            pltpu.VMEM(per_chain + (PAIR, PAIR), BF16),
            pltpu.VMEM((2, 2, 1, RW_WIDTH), F32),
            pltpu.VMEM(per_chain + (PAIR, PAIR), BF16),
            pltpu.VMEM(per_chain + (PAIR, 2 * PAIR), BF16),
            pltpu.VMEM(per_chain + (PAIR, PAIR), F32),
        ],
        compiler_params=pltpu.CompilerParams(
            dimension_semantics=("parallel",), vmem_limit_bytes=VMEM_LIMIT),
        name="rwkv",
    )(*args)
    return outs if want_state else (outs[0], None)


def _merge_kernel(x_ref, mod_ref, oa_ref, yg_ref, wg_ref, woa_ref, wor_ref, wout_ref,
                  g_ref, b_ref, o_ref):
    mod = mod_ref[...]
    sh1 = mod[:, 0:D_MODEL]
    sc1 = mod[:, D_MODEL:2 * D_MODEL]
    g1 = mod[:, 2 * D_MODEL:3 * D_MODEL]
    x = x_ref[...]
    h = (x * (1.0 + sc1) + sh1).astype(BF16)
    gates = jax.nn.sigmoid(_dot(h, wg_ref[...]))
    att = _dot(oa_ref[...].astype(BF16), woa_ref[...])
    rwk = _dot(yg_ref[...].astype(BF16), wor_ref[...])
    merged = gates[:, 0:D_MODEL] * att + gates[:, D_MODEL:2 * D_MODEL] * rwk
    mix = _dot(merged.astype(BF16), wout_ref[...])
    o_ref[...] = _layer_norm(ALPHA * x + g1 * mix, g_ref[...], b_ref[...], LN_EPS)


def _merge(x2d, mod3, o_att, yg, w_gates, p, seq_len, fixed_row):
    m = x2d.shape[0]
    row = lambda i: (i, 0)
    const = lambda i: (0, 0)
    return pl.pallas_call(
        _merge_kernel,
        grid=(m // ROW_TILE,),
        in_specs=[
            pl.BlockSpec((ROW_TILE, D_MODEL), row),
            pl.BlockSpec((None, 1, 6 * D_MODEL), _mod_row_map(seq_len, fixed_row)),
            pl.BlockSpec((ROW_TILE, DA_WIDTH), row),
            pl.BlockSpec((ROW_TILE, RW_WIDTH), row),
            pl.BlockSpec((D_MODEL, 2 * D_MODEL), const),
            pl.BlockSpec((DA_WIDTH, D_MODEL), const),
            pl.BlockSpec((RW_WIDTH, D_MODEL), const),
            pl.BlockSpec((D_MODEL, D_MODEL), const),
            pl.BlockSpec((1, D_MODEL), const),
            pl.BlockSpec((1, D_MODEL), const),
        ],
        out_specs=pl.BlockSpec((ROW_TILE, D_MODEL), row),
        out_shape=jax.ShapeDtypeStruct((m, D_MODEL), F32),
        compiler_params=pltpu.CompilerParams(
            dimension_semantics=("parallel",), vmem_limit_bytes=VMEM_LIMIT),
        name="merge",
    )(x2d, mod3, o_att, yg, w_gates, p['w_o_attn'].astype(BF16), p['w_o_rwkv'].astype(BF16),
      p['w_out'].astype(BF16), p['ln1_g'][None], p['ln1_b'][None])


def _mlp_kernel(x_ref, mod_ref, wup_ref, cw_ref, cb_ref, wd_ref, g_ref, b_ref, o_ref,
                h_scr, act_scr, *, seq_len):
    mod = mod_ref[...]
    sh2 = mod[:, 3 * D_MODEL:4 * D_MODEL]
    sc2 = mod[:, 4 * D_MODEL:5 * D_MODEL]
    g2 = mod[:, 5 * D_MODEL:6 * D_MODEL]
    h_scr[...] = (x_ref[...] * (1.0 + sc2) + sh2).astype(BF16)
    rows = x_ref.shape[0]
    pos = lax.broadcasted_iota(jnp.int32, (rows, 1), 0) & (seq_len - 1)
    first = pos == 0
    last = pos == seq_len - 1
    for j in range(D_FF // FF_TILE):
        cols = slice(j * FF_TILE, (j + 1) * FF_TILE)
        h = h_scr[...]
        u = _dot(h, wup_ref[:, cols])
        val = _dot(h, wup_ref[:, D_FF + j * FF_TILE:D_FF + (j + 1) * FF_TILE])
        prev = jnp.where(first, 0.0, pltpu.roll(u, 1, 0))
        nxt = jnp.where(last, 0.0, pltpu.roll(u, rows - 1, 0))
        cw = cw_ref[:, cols]
        u = prev * cw[0:1, :] + u * cw[1:2, :] + nxt * cw[2:3, :] + cb_ref[:, cols]
        act_scr[:, cols] = (jax.nn.gelu(u) * val).astype(BF16)
    f = _dot(act_scr[...], wd_ref[...])
    o_ref[...] = _layer_norm(ALPHA * x_ref[...] + g2 * f, g_ref[...], b_ref[...], LN_EPS)


def _mlp(x2d, mod3, p, seq_len, fixed_row):
    m = x2d.shape[0]
    assert seq_len & (seq_len - 1) == 0 and MLP_ROW_TILE % seq_len == 0
    if fixed_row is not None:
        mod_map = lambda i: (fixed_row, 0, 0)
    else:
        mod_map = lambda i: (i * MLP_ROW_TILE // seq_len, 0, 0)
    row = lambda i: (i, 0)
    const = lambda i: (0, 0)
    resident = pl.Buffered(1)
    return pl.pallas_call(
        functools.partial(_mlp_kernel, seq_len=seq_len),
        grid=(m // MLP_ROW_TILE,),
        in_specs=[
            pl.BlockSpec((MLP_ROW_TILE, D_MODEL), row),
            pl.BlockSpec((None, 1, 6 * D_MODEL), mod_map),
            pl.BlockSpec((D_MODEL, 2 * D_FF), const, pipeline_mode=resident),
            pl.BlockSpec((3, D_FF), const),
            pl.BlockSpec((1, D_FF), const),
            pl.BlockSpec((D_FF, D_MODEL), const, pipeline_mode=resident),
            pl.BlockSpec((1, D_MODEL), const),
            pl.BlockSpec((1, D_MODEL), const),
        ],
        out_specs=pl.BlockSpec((MLP_ROW_TILE, D_MODEL), row),
        out_shape=jax.ShapeDtypeStruct((m, D_MODEL), F32),
        scratch_shapes=[
            pltpu.VMEM((MLP_ROW_TILE, D_MODEL), BF16),
            pltpu.VMEM((MLP_ROW_TILE, D_FF), BF16),
        ],
        compiler_params=pltpu.CompilerParams(
            dimension_semantics=("parallel",), vmem_limit_bytes=MLP_VMEM_LIMIT),
        name="mlp",
    )(x2d, mod3, p['w_up'].astype(BF16), p['conv_w'], p['conv_b'][None], p['w_down'].astype(BF16),
      p['ln2_g'][None], p['ln2_b'][None])


def _trunk_layer(x, mod3, fixed_row, p, w_qkvr, w_gates, layer, ctx=None):
    batch, seq_len, _ = x.shape
    x2d = x.reshape(batch * seq_len, D_MODEL)
    q, k, v, rw = _input_projection(x2d, mod3, w_qkvr, seq_len, fixed_row)
    if ctx is None:
        o_att = _attention(q, k, v, p['da_lambda'], p['da_subln_g'][None], batch, seq_len, layer)
        yg, state = _rwkv(rw, p, batch, seq_len, want_state=True)
    else:
        k_ctx, v_ctx, s_ctx = ctx
        past = k_ctx.shape[1]
        o_att = _attention(q, k, v, p['da_lambda'], p['da_subln_g'][None], batch, seq_len, layer,
                           ctx=(k_ctx.reshape(batch, past, DA_WIDTH), v_ctx.reshape(batch, past, DA_WIDTH)))
        yg, state = _rwkv(rw, p, batch, seq_len, state_in=s_ctx)
    x1 = _merge(x2d, mod3, o_att, yg, w_gates, p, seq_len, fixed_row)
    y = _mlp(x1, mod3, p, seq_len, fixed_row)
    new_ctx = None
    if ctx is None:
        new_ctx = (k.reshape(batch, seq_len, DA_HEADS, 2, DA_HEAD_DIM),
                   v.reshape(batch, seq_len, DA_HEADS, 2 * DA_HEAD_DIM), state)
    return y.reshape(batch, seq_len, D_MODEL), new_ctx


def kernel(x_prompt, x_sample, cache_k, cache_v, state_rwkv, c, c_ctx, w_ada, b_ada, w_in, rw_mu, rw_w0, rw_w_up, rw_a0, rw_a_up, rw_g_up, rw_k_k, rw_k_a, rw_r_k, rw_lnx_g, rw_lnx_b, da_lambda, da_subln_g, w_o_attn, w_o_rwkv, w_out, ln1_g, ln1_b, w_up, conv_w, conv_b, w_down, ln2_g, ln2_b):
    dec_batch = x_sample.shape[0]
    assert dec_batch < MOD_ROWS
    y_prompt, y_sample = x_prompt, x_sample
    new_k, new_v, new_s = [], [], []
    for l in range(DEPTH):
        p = {
            'rw_mu': rw_mu[l], 'rw_w0': rw_w0[l], 'rw_w_up': rw_w_up[l], 'rw_a0': rw_a0[l],
            'rw_a_up': rw_a_up[l], 'rw_g_up': rw_g_up[l], 'rw_k_k': rw_k_k[l], 'rw_k_a': rw_k_a[l],
            'rw_r_k': rw_r_k[l], 'rw_lnx_g': rw_lnx_g[l], 'rw_lnx_b': rw_lnx_b[l],
            'da_lambda': da_lambda[l], 'da_subln_g': da_subln_g[l], 'w_o_attn': w_o_attn[l],
            'w_o_rwkv': w_o_rwkv[l], 'w_out': w_out[l], 'ln1_g': ln1_g[l], 'ln1_b': ln1_b[l],
            'w_up': w_up[l], 'conv_w': conv_w[l], 'conv_b': conv_b[l], 'w_down': w_down[l],
            'ln2_g': ln2_g[l], 'ln2_b': ln2_b[l],
        }
        cvec = jnp.concatenate(
            [c, c_ctx[None], jnp.zeros((MOD_ROWS - dec_batch - 1, D_MODEL), F32)], axis=0)
        mod3 = _modulation(cvec, w_ada[l], b_ada[l][None]).reshape(MOD_ROWS, 1, 6 * D_MODEL)
        w_qkvr = w_in[l][:, :QKVR_COLS].astype(BF16)
        w_gates = w_in[l][:, QKVR_COLS:].astype(BF16)
        y_prompt, ctx_l = _trunk_layer(y_prompt, mod3, dec_batch, p, w_qkvr, w_gates, l)
        new_k.append(ctx_l[0])
        new_v.append(ctx_l[1])
        new_s.append(ctx_l[2])
        y_sample, _ = _trunk_layer(y_sample, mod3, None, p, w_qkvr, w_gates, l,
                                   ctx=(cache_k[:, l], cache_v[:, l], state_rwkv[:, l]))
    return (y_prompt, y_sample, jnp.stack(new_k, axis=1), jnp.stack(new_v, axis=1),
            jnp.stack(new_s, axis=1))
```

```python
import functools
import math

import jax
import jax.numpy as jnp
from jax import lax
from jax.experimental import pallas as pl
from jax.experimental.pallas import tpu as pltpu

F32 = jnp.float32
BF16 = jnp.bfloat16
HIGHEST = lax.Precision.HIGHEST

D_MODEL = 1024
GRID_W = 64
DA_HEADS = 4
DA_HEAD_DIM = 64
DA_WIDTH = DA_HEADS * 2 * DA_HEAD_DIM
ROPE_PAIRS_PER_AXIS = DA_HEAD_DIM // 4
ROPE_BASE = 10000.0
RW_HEADS = 8
RW_HEAD_DIM = 64
RW_WIDTH = RW_HEADS * RW_HEAD_DIM
DECAY_LORA = 64
AAA_LORA = 64
GATE_LORA = 128
RW_COLS = 3 * RW_WIDTH + DECAY_LORA + AAA_LORA + GATE_LORA
RW_LNX_EPS = 64e-5
QKVR_COLS = 3 * DA_WIDTH + RW_COLS
D_FF = 2816
LN_EPS = 1e-5
DEPTH = 1
ALPHA = (2.0 * DEPTH) ** 0.25

CHUNK = 64
ATTN_Q_BLOCK = 256
ROW_TILE = 512
INPROJ_ROW_TILE = 1024
SHIFT_COL_TILE = 256
MLP_ROW_TILE = 1024
FF_TILE = 256
MOD_COL_TILE = 768
MOD_ROWS = 16
VMEM_LIMIT = 48 * 1024 * 1024
MLP_VMEM_LIMIT = 56 * 1024 * 1024

_NN = (((1,), (0,)), ((), ()))
_NT = (((1,), (1,)), ((), ()))
_TN = (((0,), (0,)), ((), ()))


def _dot(a, b, dims=_NN, precision=None):
    return lax.dot_general(a, b, dims, precision=precision, preferred_element_type=F32)


def _layer_norm(z, g, b, eps):
    mu = jnp.mean(z, axis=-1, keepdims=True)
    zc = z - mu
    var = jnp.mean(zc * zc, axis=-1, keepdims=True)
    return zc * lax.rsqrt(var + eps) * g + b


def _mod_kernel(c_ref, w_ref, b_ref, o_ref):
    cv = c_ref[...]
    s = cv * jax.nn.sigmoid(cv)
    o_ref[...] = _dot(s, w_ref[...], precision=HIGHEST) + b_ref[...]


def _modulation(cvec, w_ada, b_ada):
    n = w_ada.shape[1]
    return pl.pallas_call(
        _mod_kernel,
        grid=(n // MOD_COL_TILE,),
        in_specs=[
            pl.BlockSpec((MOD_ROWS, D_MODEL), lambda j: (0, 0)),
            pl.BlockSpec((D_MODEL, MOD_COL_TILE), lambda j: (0, j)),
            pl.BlockSpec((1, MOD_COL_TILE), lambda j: (0, j)),
        ],
        out_specs=pl.BlockSpec((MOD_ROWS, MOD_COL_TILE), lambda j: (0, j)),
        out_shape=jax.ShapeDtypeStruct((MOD_ROWS, n), F32),
        compiler_params=pltpu.CompilerParams(
            dimension_semantics=("parallel",), vmem_limit_bytes=VMEM_LIMIT),
        name="mod",
    )(cvec, w_ada, b_ada)


def _mod_row_map(rows_per_batch, fixed_row):
    if fixed_row is not None:
        return lambda i: (fixed_row, 0, 0)
    tiles = rows_per_batch // ROW_TILE
    return lambda i: (i // tiles, 0, 0)


def _inproj_kernel(x_ref, mod_ref, w_ref, mu_ref, q_ref, k_ref, v_ref, rw_ref, *, seq_len):
    mod = mod_ref[...]
    sh1 = mod[:, 0:D_MODEL]
    sc1 = mod[:, D_MODEL:2 * D_MODEL]
    h = (x_ref[...] * (1.0 + sc1) + sh1).astype(BF16)
    q_ref[...] = _dot(h, w_ref[:, 0:DA_WIDTH])
    k_ref[...] = _dot(h, w_ref[:, DA_WIDTH:2 * DA_WIDTH])
    v_ref[...] = _dot(h, w_ref[:, 2 * DA_WIDTH:3 * DA_WIDTH])
    rows = x_ref.shape[0]
    pos = lax.broadcasted_iota(jnp.int32, (rows, 1), 0) & (seq_len - 1)
    first = pos == 0
    last = pos == seq_len - 1
    for j in range(RW_COLS // SHIFT_COL_TILE):
        cols = slice(j * SHIFT_COL_TILE, (j + 1) * SHIFT_COL_TILE)
        rw = _dot(h, w_ref[:, 3 * DA_WIDTH + j * SHIFT_COL_TILE:3 * DA_WIDTH + (j + 1) * SHIFT_COL_TILE])
        prev = jnp.where(first, 0.0, pltpu.roll(rw, 1, 0))
        nxt = jnp.where(last, 0.0, pltpu.roll(rw, rows - 1, 0))
        rw_ref[:, cols] = rw + mu_ref[0:1, cols] * (prev - rw) + mu_ref[1:2, cols] * (nxt - rw)


def _input_projection(x2d, mod3, w_qkvr, rw_mu, seq_len, fixed_row):
    m = x2d.shape[0]
    assert seq_len & (seq_len - 1) == 0 and INPROJ_ROW_TILE % seq_len == 0
    if fixed_row is not None:
        mod_map = lambda i: (fixed_row, 0, 0)
    else:
        mod_map = lambda i: (i * INPROJ_ROW_TILE // seq_len, 0, 0)
    row = lambda i: (i, 0)
    return pl.pallas_call(
        functools.partial(_inproj_kernel, seq_len=seq_len),
        grid=(m // INPROJ_ROW_TILE,),
        in_specs=[
            pl.BlockSpec((INPROJ_ROW_TILE, D_MODEL), row),
            pl.BlockSpec((None, 1, 6 * D_MODEL), mod_map),
            pl.BlockSpec((D_MODEL, QKVR_COLS), lambda i: (0, 0), pipeline_mode=pl.Buffered(1)),
            pl.BlockSpec((2, RW_COLS), lambda i: (0, 0)),
        ],
        out_specs=[
            pl.BlockSpec((INPROJ_ROW_TILE, DA_WIDTH), row),
            pl.BlockSpec((INPROJ_ROW_TILE, DA_WIDTH), row),
            pl.BlockSpec((INPROJ_ROW_TILE, DA_WIDTH), row),
            pl.BlockSpec((INPROJ_ROW_TILE, RW_COLS), row),
        ],
        out_shape=[
            jax.ShapeDtypeStruct((m, DA_WIDTH), F32),
            jax.ShapeDtypeStruct((m, DA_WIDTH), F32),
            jax.ShapeDtypeStruct((m, DA_WIDTH), F32),
            jax.ShapeDtypeStruct((m, RW_COLS), F32),
        ],
        compiler_params=pltpu.CompilerParams(
            dimension_semantics=("parallel",), vmem_limit_bytes=VMEM_LIMIT),
        name="inproj",
    )(x2d, mod3, w_qkvr, rw_mu)


def _rope(x, cos, sin_signed):
    lane = lax.broadcasted_iota(jnp.int32, x.shape, 1)
    partner = jnp.where((lane & 63) < 32, pltpu.roll(x, 96, 1), pltpu.roll(x, 32, 1))
    return x * cos + partner * sin_signed


def _softmax_av(qm, keys, vals):
    scores = [_dot(qm, kg, _NT) for kg in keys]
    mx = scores[0].max(axis=-1, keepdims=True)
    for s in scores[1:]:
        mx = jnp.maximum(mx, s.max(axis=-1, keepdims=True))
    den = None
    out = None
    for s, vg in zip(scores, vals):
        p = jnp.exp(s - mx)
        d = jnp.sum(p, axis=-1, keepdims=True)
        o = _dot(p.astype(BF16), vg)
        den = d if den is None else den + d
        out = o if out is None else out + o
    return out / den


def _attn_kernel(*refs, has_ctx, seq_len, lam_init):
    if has_ctx:
        q_ref, k_ref, v_ref, kc_ref, vc_ref, cos_ref, sin_ref, lq_ref, g_ref, o_ref = refs
    else:
        q_ref, k_ref, v_ref, lq_ref, g_ref, o_ref = refs
    d = DA_HEAD_DIM
    lq = lq_ref[...]
    lam = (jnp.exp(jnp.sum(lq[0:1] * lq[1:2], axis=-1, keepdims=True))
           - jnp.exp(jnp.sum(lq[2:3] * lq[3:4], axis=-1, keepdims=True)) + lam_init)
    k = k_ref[...]
    if has_ctx:
        k = _rope(k, cos_ref[...], sin_ref[...])
    keys = [[k[:, m * d:(m + 1) * d].astype(BF16)] for m in range(2)]
    vals = [v_ref[...].astype(BF16)]
    if has_ctx:
        kc = kc_ref[...]
        for m in range(2):
            keys[m].append(kc[:, m * d:(m + 1) * d].astype(BF16))
        vals.append(vc_ref[...].astype(BF16))
    g = g_ref[...]
    for qb in range(seq_len // ATTN_Q_BLOCK):
        rows = slice(qb * ATTN_Q_BLOCK, (qb + 1) * ATTN_Q_BLOCK)
        q = q_ref[rows, :]
        if has_ctx:
            q = _rope(q, cos_ref[rows, :], sin_ref[rows, :])
        q = q * (d ** -0.5)
        o1 = _softmax_av(q[:, 0:d].astype(BF16), keys[0], vals)
        o2 = _softmax_av(q[:, d:2 * d].astype(BF16), keys[1], vals)
        o = o1 - lam * o2
        ms = jnp.mean(o * o, axis=-1, keepdims=True)
        o_ref[rows, :] = o * lax.rsqrt(ms + LN_EPS) * g * (1.0 - lam_init)


def _rope_tables(n):
    rows = n // GRID_W
    row = jnp.repeat(jnp.arange(rows, dtype=F32), GRID_W)
    col = jnp.tile(jnp.arange(GRID_W, dtype=F32), rows)
    inv = ROPE_BASE ** (-jnp.arange(ROPE_PAIRS_PER_AXIS, dtype=F32) / ROPE_PAIRS_PER_AXIS)
    ang = jnp.concatenate([row[:, None] * inv, col[:, None] * inv], -1)
    cos, sin = jnp.cos(ang), jnp.sin(ang)
    return jnp.tile(cos, (1, 4)), jnp.tile(jnp.concatenate([-sin, sin], -1), (1, 2))


def _attention(q2d, k2d, v2d, da_lambda, subln_g, batch, seq_len, layer, ctx=None):
    has_ctx = ctx is not None
    w = 2 * DA_HEAD_DIM
    head = lambda b, h: (b, h)
    const = lambda b, h: (0, 0)
    in_specs = [pl.BlockSpec((seq_len, w), head)] * 3
    args = [q2d, k2d, v2d]
    if has_ctx:
        kc, vc = ctx
        past = kc.shape[1]
        in_specs += [pl.BlockSpec((None, past, w), lambda b, h: (b, 0, h))] * 2
        in_specs += [pl.BlockSpec((seq_len, w), const)] * 2
        args += [kc, vc, *_rope_tables(seq_len)]
    in_specs += [pl.BlockSpec((4, DA_HEAD_DIM), const), pl.BlockSpec((1, w), const)]
    args += [da_lambda, subln_g]
    lam_init = 0.8 - 0.6 * math.exp(-0.3 * layer)
    return pl.pallas_call(
        functools.partial(_attn_kernel, has_ctx=has_ctx, seq_len=seq_len, lam_init=lam_init),
        grid=(batch, DA_HEADS),
        in_specs=in_specs,
        out_specs=pl.BlockSpec((seq_len, w), head),
        out_shape=jax.ShapeDtypeStruct((batch * seq_len, DA_WIDTH), F32),
        compiler_params=pltpu.CompilerParams(
            dimension_semantics=("parallel", "parallel"), vmem_limit_bytes=VMEM_LIMIT),
        name="attn",
    )(*args)


PAIR = 2 * RW_HEAD_DIM
N_PAIRS = RW_HEADS // 2
FINISH_ROWS = 256
RWKV_GROUP_ROWS = 1024

_M_STRICT = (0, 2)
_M_INCL = (1, 3)
_M_BLOCK8, _M_OFF16, _M_OFF32, _M_OFF64, _M_EYE = 4, 5, 6, 7, 8


def _pair_masks():
    t = jnp.arange(PAIR)[:, None]
    s = jnp.arange(PAIR)[None, :]
    same = lambda n: (t // n) == (s // n)
    head = same(CHUNK)
    masks = [head & (t > s), head & (t >= s), head & (t < s), head & (t <= s),
             same(8), same(16) & ~same(8), same(32) & ~same(16), head & ~same(32), t == s]
    return jnp.stack(masks).astype(BF16)


def _cumsum_matrices():
    t = jnp.arange(CHUNK)[:, None]
    s = jnp.arange(CHUNK)[None, :]
    return jnp.stack([t >= s, t <= s]).astype(BF16)


def _pair_sum_matrix():
    i = jnp.arange(PAIR)
    return ((i[:, None] // RW_HEAD_DIM) == (i[None, :] // RW_HEAD_DIM)).astype(BF16)


def _split3(x):
    hi = x.astype(BF16)
    r1 = x - hi.astype(F32)
    mid = r1.astype(BF16)
    lo = (r1 - mid.astype(F32)).astype(BF16)
    return hi, mid, lo


def _head_sums(x, e_ref):
    rows = x.shape[0]
    xs = jnp.concatenate([x[:, p * PAIR:(p + 1) * PAIR] for p in range(N_PAIRS)], axis=0)
    s = _dot(xs.astype(BF16), e_ref[...])
    return jnp.concatenate([s[p * rows:(p + 1) * rows] for p in range(N_PAIRS)], axis=1)


def _run_schedule(order, **stages):
    for name in order:
        next(stages[name], None)
    for gen in stages.values():
        for _ in gen:
            pass


_STEP_ORDER = "C B A B C B A B C B B A B B B B".split()


def _rwkv_kernel(*refs, seq_len, group, has_state_in, has_state_out):
    refs = list(refs)
    rw_ref = refs.pop(0)
    s0_ref = refs.pop(0) if has_state_in else None
    (w0_ref, wup_ref, a0_ref, aup_ref, gup_ref, kk_ref, ka_ref, rk_ref,
     lng_ref, lnb_ref, e_ref, tri_ref, m_ref) = refs[:13]
    refs = refs[13:]
    o_ref = refs.pop(0)
    so_ref = refs.pop(0) if has_state_out else None
    (y_scr, bv_scr, gr_scr, s_scr, sio_scr, ar_scr, bk_scr, v_scr, ge_scr,
     t_scr, wy_scr, rl_scr) = refs

    C = CHUNK
    N = RW_HEAD_DIM
    W = RW_WIDTH
    P = PAIR
    nc = seq_len // C
    n_steps = group * nc
    total_rows = group * seq_len
    assert nc & (nc - 1) == 0 and n_steps % 2 == 0
    mm = lambda x, y: _dot(x.astype(BF16), y.astype(BF16))
    mmb = lambda x, y: _dot(x, y).astype(BF16)
    zeros_nn = jnp.zeros((N, N), F32)
    chains = [(d, p) for d in range(2) for p in range(N_PAIRS)]
    n = range(len(chains))

    if has_state_in:
        for g in range(group):
            for d, p in chains:
                top = jnp.concatenate([s0_ref[g, d, 2 * p], zeros_nn], axis=1)
                bot = jnp.concatenate([zeros_nn, s0_ref[g, d, 2 * p + 1]], axis=1)
                sio_scr[g, d, p] = jnp.concatenate([top, bot], axis=0)
    for d, p in chains:
        s_scr[d, p] = jnp.zeros((P, P), F32)

    for scr in (ar_scr, bk_scr, v_scr):
        scr[...] = jnp.zeros(scr.shape, scr.dtype)

    def put_block_diag(scr, slot, d, row0, x):
        xb = x.astype(BF16)
        for p in range(N_PAIRS):
            for hh in range(2):
                lanes = slice(hh * N, (hh + 1) * N)
                scr[slot, d, p, row0 + hh * C:row0 + (hh + 1) * C, lanes] = (
                    xb[:, p * P + hh * N:p * P + (hh + 1) * N])

    def step_chunks(i):
        i = jnp.minimum(i, n_steps - 1)
        g = i // nc
        il = i % nc
        return g, il, (il, nc - 1 - il)

    def prep_stages(i, slot):
        g, _, cs = step_chunks(i)
        st = []
        for d, c in enumerate(cs):
            r0 = pl.multiple_of(g * seq_len + c * C, C)
            rows = pl.ds(r0, C)
            xm = rw_ref[rows, :]
            w_lo = xm[:, 3 * W:3 * W + DECAY_LORA]
            a_lo = xm[:, 3 * W + DECAY_LORA:3 * W + DECAY_LORA + AAA_LORA]
            st.append(dict(rows=rows, r=xm[:, 0:W], kr=xm[:, W:2 * W], vr=xm[:, 2 * W:3 * W],
                           g_lo=xm[:, 3 * W + DECAY_LORA + AAA_LORA:RW_COLS],
                           w_up=_dot(jnp.tanh(w_lo).astype(BF16), wup_ref[d]),
                           a_up=_dot(a_lo.astype(BF16), aup_ref[d])))
        yield
        for d, x in enumerate(st):
            wlog = -jax.nn.softplus(-(w0_ref[d:d + 1, :] + x['w_up'])) - 0.5
            x['log_decay'] = -jnp.exp(wlog)
            x['a'] = jax.nn.sigmoid(a0_ref[d:d + 1, :] + x['a_up'])
            x['keff'] = x['kr'] * (1.0 + (x['a'] - 1.0) * ka_ref[...])
            x['kk'] = x['kr'] * kk_ref[...]
            x['sums'] = _head_sums(
                jnp.concatenate([x['kk'] * x['kk'], x['r'] * x['keff'] * rk_ref[...]], axis=0), e_ref)
            if d == 0:
                gr_scr[x['rows'], :] = _dot(jax.nn.sigmoid(x['g_lo']).astype(BF16), gup_ref[...])
            cum = _dot(tri_ref[d], jnp.concatenate(_split3(x['log_decay']), axis=1))
            x['cum'] = cum[:, 0:W] + cum[:, W:2 * W] + cum[:, 2 * W:3 * W]
        yield
        for d, x in enumerate(st):
            kk = x['kk'] / jnp.maximum(jnp.sqrt(x['sums'][0:C]), 1e-12)
            bv_scr[d, x['rows'], :] = x['sums'][C:2 * C] * x['vr']
            cum = x['cum']
            g_in = jnp.exp(cum)
            g_inv = jnp.exp(-cum)
            a_t = -kk * jnp.exp(cum - x['log_decay'])
            r_t = x['r'] * g_in
            b_t = kk * x['a'] * g_inv
            k_t = x['keff'] * g_inv
            ge_scr[slot, d] = g_in[C - 1:C, :] if d == 0 else g_in[0:1, :]
            put_block_diag(ar_scr, slot, d, 0, a_t)
            put_block_diag(ar_scr, slot, d, P, r_t)
            put_block_diag(bk_scr, slot, d, 0, b_t)
            put_block_diag(bk_scr, slot, d, P, k_t)
            put_block_diag(v_scr, slot, d, 0, x['vr'])
        yield

    def local_stages(slot):
        ar = [ar_scr[slot, d, p] for d, p in chains]
        bk = [bk_scr[slot, d, p] for d, p in chains]
        vb = [v_scr[slot, d, p] for d, p in chains]
        strict = [m_ref[_M_STRICT[d]] for d, _ in chains]
        incl = [m_ref[_M_INCL[d]] for d, _ in chains]
        x = [_dot(ar[i], bk[i], _NT).astype(BF16) for i in n]
        yield
        amat = [x[i][0:P, 0:P] * strict[i] for i in n]
        a8 = [a * m_ref[_M_BLOCK8] for a in amat]
        a2 = [mmb(a, a) for a in a8]
        for i, (d, p) in enumerate(chains):
            rl_scr[slot, d, p] = _dot(x[i][0:P, P:2 * P] * strict[i], vb[i])
            wy_scr[slot, d, p] = jnp.concatenate(
                [x[i][P:2 * P, 0:P] * incl[i], x[i][P:2 * P, P:2 * P] * incl[i]], axis=1)
        yield
        a4 = [mmb(a, a) for a in a2]
        ps = [m_ref[_M_EYE] + a for a in a8]
        ps = [t + mmb(t, a) for t, a in zip(ps, a2)]
        yield
        ps = [t + mmb(t, a) for t, a in zip(ps, a4)]
        yield
        for off in (_M_OFF16, _M_OFF32, _M_OFF64):
            ts = [mmb(t, a * m_ref[off]) for t, a in zip(ps, amat)]
            yield
            ps = [t + mmb(q, t) for t, q in zip(ps, ts)]
            if off == _M_OFF64:
                for i, (d, p) in enumerate(chains):
                    t_scr[slot, d, p] = ps[i]
            yield

    def state_stages(i, slot):
        g, il, cs = step_chunks(i)
        ar = [ar_scr[slot, d, p] for d, p in chains]
        bk = [bk_scr[slot, d, p] for d, p in chains]
        vb = [v_scr[slot, d, p] for d, p in chains]
        g_end = [ge_scr[slot, d] for d in range(2)]
        if has_state_in:
            s_old = [jnp.where(il == 0, sio_scr[g, d, p], s_scr[d, p]) for d, p in chains]
        else:
            s_old = [jnp.where(il == 0, 0.0, s_scr[d, p]) for d, p in chains]
        xs = [_dot(ar[i], s_old[i].astype(BF16), _NT) for i in n]
        yield
        u = [mm(t_scr[slot, d, p], xs[i][0:P] + rl_scr[slot, d, p]).astype(BF16)
             for i, (d, p) in enumerate(chains)]
        yield
        uv = [jnp.concatenate([u[i], vb[i]], axis=0) for i in n]
        y = [xs[i][P:2 * P] + _dot(wy_scr[slot, d, p], uv[i]) for i, (d, p) in enumerate(chains)]
        s_new = [s_old[i] + _dot(uv[i], bk[i], _TN) for i in n]
        for i, (d, p) in enumerate(chains):
            sl = slice(p * P, (p + 1) * P)
            rows = pl.ds(pl.multiple_of(g * seq_len + cs[d] * C, C), C)
            y_scr[d, rows, sl] = y[i][0:C] + y[i][C:2 * C]
            s_end = s_new[i] * g_end[d][:, sl]
            s_scr[d, p] = s_end
            if has_state_out:
                sio_scr[g, d, p] = s_end
        yield

    for _ in prep_stages(0, 0):
        pass
    _run_schedule("B A B A B A".split(), A=prep_stages(1, 1), B=local_stages(0))

    def scan_body(j, carry):
        i = 2 * j
        _run_schedule(_STEP_ORDER, C=state_stages(i, 0), B=local_stages(1), A=prep_stages(i + 2, 0))
        _run_schedule(_STEP_ORDER, C=state_stages(i + 1, 1), B=local_stages(0), A=prep_stages(i + 3, 1))
        return carry

    lax.fori_loop(0, n_steps // 2, scan_body, 0)

    def finish_body(i, carry):
        rows = pl.ds(pl.multiple_of(i * FINISH_ROWS, FINISH_ROWS), FINISH_ROWS)
        y = y_scr[0, rows, :] + y_scr[1, rows, :]
        yc = y - _head_sums(y, e_ref) * (1.0 / N)
        var = _head_sums(yc * yc, e_ref) * (1.0 / N)
        yn = yc * lax.rsqrt(var + RW_LNX_EPS) * lng_ref[...] + lnb_ref[...]
        o_ref[rows, :] = (yn + bv_scr[0, rows, :] + bv_scr[1, rows, :]) * gr_scr[rows, :]
        return carry

    lax.fori_loop(0, total_rows // FINISH_ROWS, finish_body, 0)
    if has_state_out:
        for g in range(group):
            for d, p in chains:
                s_pair = sio_scr[g, d, p]
                so_ref[g, d, 2 * p] = s_pair[0:N, 0:N]
                so_ref[g, d, 2 * p + 1] = s_pair[N:P, N:P]


def _rwkv(rw2d, p, batch, seq_len, state_in=None, want_state=False):
    has_state_in = state_in is not None
    assert not (has_state_in and want_state)
    group = max(1, RWKV_GROUP_ROWS // seq_len)
    assert batch % group == 0
    rows = group * seq_len
    const2 = lambda b: (0, 0)
    const3 = lambda b: (0, 0, 0)
    state_spec = pl.BlockSpec((group, 2, RW_HEADS, RW_HEAD_DIM, RW_HEAD_DIM), lambda b: (b, 0, 0, 0, 0))
    in_specs = [pl.BlockSpec((rows, RW_COLS), lambda b: (b, 0))]
    args = [rw2d]
    if has_state_in:
        in_specs.append(state_spec)
        args.append(state_in)
    in_specs += [
        pl.BlockSpec((2, RW_WIDTH), const2),
        pl.BlockSpec((2, DECAY_LORA, RW_WIDTH), const3),
        pl.BlockSpec((2, RW_WIDTH), const2),
        pl.BlockSpec((2, AAA_LORA, RW_WIDTH), const3),
        pl.BlockSpec((GATE_LORA, RW_WIDTH), const2),
        pl.BlockSpec((1, RW_WIDTH), const2),
        pl.BlockSpec((1, RW_WIDTH), const2),
        pl.BlockSpec((1, RW_WIDTH), const2),
        pl.BlockSpec((1, RW_WIDTH), const2),
        pl.BlockSpec((1, RW_WIDTH), const2),
        pl.BlockSpec((PAIR, PAIR), const2),
        pl.BlockSpec((2, CHUNK, CHUNK), const3),
        pl.BlockSpec((9, PAIR, PAIR), const3),
    ]
    args += [p['rw_w0'], p['rw_w_up'].astype(BF16), p['rw_a0'], p['rw_a_up'].astype(BF16),
             p['rw_g_up'].astype(BF16), p['rw_k_k'][None], p['rw_k_a'][None],
             p['rw_r_k'].reshape(1, RW_WIDTH), p['rw_lnx_g'][None], p['rw_lnx_b'][None],
             _pair_sum_matrix(), _cumsum_matrices(), _pair_masks()]
    out_specs = [pl.BlockSpec((rows, RW_WIDTH), lambda b: (b, 0))]
    out_shape = [jax.ShapeDtypeStruct((batch * seq_len, RW_WIDTH), F32)]
    if want_state:
        out_specs.append(state_spec)
        out_shape.append(jax.ShapeDtypeStruct((batch, 2, RW_HEADS, RW_HEAD_DIM, RW_HEAD_DIM), F32))
    per_chain = (2, 2, N_PAIRS)
    outs = pl.pallas_call(
        functools.partial(_rwkv_kernel, seq_len=seq_len, group=group, has_state_in=has_state_in,
                          has_state_out=want_state),
        grid=(batch // group,),
        in_specs=in_specs,
        out_specs=out_specs,
        out_shape=out_shape,
        scratch_shapes=[
            pltpu.VMEM((2, rows, RW_WIDTH), F32),
            pltpu.VMEM((2, rows, RW_WIDTH), F32),
            pltpu.VMEM((rows, RW_WIDTH), F32),
            pltpu.VMEM((2, N_PAIRS, PAIR, PAIR), F32),
            pltpu.VMEM((group, 2, N_PAIRS, PAIR, PAIR), F32),
            pltpu.VMEM(per_chain + (2 * PAIR, PAIR), BF16),
            pltpu.VMEM(per_chain + (2 * PAIR, PAIR), BF16),
            pltpu.VMEM(per_chain + (PAIR, PAIR), BF16),
            pltpu.VMEM((2, 2, 1, RW_WIDTH), F32),
            pltpu.VMEM(per_chain + (PAIR, PAIR), BF16),
            pltpu.VMEM(per_chain + (PAIR, 2 * PAIR), BF16),
            pltpu.VMEM(per_chain + (PAIR, PAIR), F32),
        ],
        compiler_params=pltpu.CompilerParams(
            dimension_semantics=("parallel",), vmem_limit_bytes=VMEM_LIMIT),
        name="rwkv",
    )(*args)
    return outs if want_state else (outs[0], None)


def _merge_kernel(x_ref, mod_ref, oa_ref, yg_ref, wg_ref, woa_ref, wor_ref, wout_ref,
                  g_ref, b_ref, o_ref):
    mod = mod_ref[...]
    sh1 = mod[:, 0:D_MODEL]
    sc1 = mod[:, D_MODEL:2 * D_MODEL]
    g1 = mod[:, 2 * D_MODEL:3 * D_MODEL]
    x = x_ref[...]
    h = (x * (1.0 + sc1) + sh1).astype(BF16)
    gates = jax.nn.sigmoid(_dot(h, wg_ref[...]))
    att = _dot(oa_ref[...].astype(BF16), woa_ref[...])
    rwk = _dot(yg_ref[...].astype(BF16), wor_ref[...])
    merged = gates[:, 0:D_MODEL] * att + gates[:, D_MODEL:2 * D_MODEL] * rwk
    mix = _dot(merged.astype(BF16), wout_ref[...])
    o_ref[...] = _layer_norm(ALPHA * x + g1 * mix, g_ref[...], b_ref[...], LN_EPS)


def _merge(x2d, mod3, o_att, yg, w_gates, p, seq_len, fixed_row):
    m = x2d.shape[0]
    row = lambda i: (i, 0)
    const = lambda i: (0, 0)
    return pl.pallas_call(
        _merge_kernel,
        grid=(m // ROW_TILE,),
        in_specs=[
            pl.BlockSpec((ROW_TILE, D_MODEL), row),
            pl.BlockSpec((None, 1, 6 * D_MODEL), _mod_row_map(seq_len, fixed_row)),
            pl.BlockSpec((ROW_TILE, DA_WIDTH), row),
            pl.BlockSpec((ROW_TILE, RW_WIDTH), row),
            pl.BlockSpec((D_MODEL, 2 * D_MODEL), const),
            pl.BlockSpec((DA_WIDTH, D_MODEL), const),
            pl.BlockSpec((RW_WIDTH, D_MODEL), const),
            pl.BlockSpec((D_MODEL, D_MODEL), const),
            pl.BlockSpec((1, D_MODEL), const),
            pl.BlockSpec((1, D_MODEL), const),
        ],
        out_specs=pl.BlockSpec((ROW_TILE, D_MODEL), row),
        out_shape=jax.ShapeDtypeStruct((m, D_MODEL), F32),
        compiler_params=pltpu.CompilerParams(
            dimension_semantics=("parallel",), vmem_limit_bytes=VMEM_LIMIT),
        name="merge",
    )(x2d, mod3, o_att, yg, w_gates, p['w_o_attn'].astype(BF16), p['w_o_rwkv'].astype(BF16),
      p['w_out'].astype(BF16), p['ln1_g'][None], p['ln1_b'][None])


def _mlp_kernel(x_ref, mod_ref, wup_ref, cw_ref, cb_ref, wd_ref, g_ref, b_ref, o_ref,
                h_scr, act_scr, *, seq_len):
    mod = mod_ref[...]
    sh2 = mod[:, 3 * D_MODEL:4 * D_MODEL]
    sc2 = mod[:, 4 * D_MODEL:5 * D_MODEL]
    g2 = mod[:, 5 * D_MODEL:6 * D_MODEL]
    h_scr[...] = (x_ref[...] * (1.0 + sc2) + sh2).astype(BF16)
    rows = x_ref.shape[0]
    pos = lax.broadcasted_iota(jnp.int32, (rows, 1), 0) & (seq_len - 1)
    first = pos == 0
    last = pos == seq_len - 1
    for j in range(D_FF // FF_TILE):
        cols = slice(j * FF_TILE, (j + 1) * FF_TILE)
        h = h_scr[...]
        u = _dot(h, wup_ref[:, cols])
        val = _dot(h, wup_ref[:, D_FF + j * FF_TILE:D_FF + (j + 1) * FF_TILE])
        prev = jnp.where(first, 0.0, pltpu.roll(u, 1, 0))
        nxt = jnp.where(last, 0.0, pltpu.roll(u, rows - 1, 0))
        cw = cw_ref[:, cols]
        u = prev * cw[0:1, :] + u * cw[1:2, :] + nxt * cw[2:3, :] + cb_ref[:, cols]
        act_scr[:, cols] = (jax.nn.gelu(u) * val).astype(BF16)
    f = _dot(act_scr[...], wd_ref[...])
    o_ref[...] = _layer_norm(ALPHA * x_ref[...] + g2 * f, g_ref[...], b_ref[...], LN_EPS)


def _mlp(x2d, mod3, p, seq_len, fixed_row):
    m = x2d.shape[0]
    assert seq_len & (seq_len - 1) == 0 and MLP_ROW_TILE % seq_len == 0
    if fixed_row is not None:
        mod_map = lambda i: (fixed_row, 0, 0)
    else:
        mod_map = lambda i: (i * MLP_ROW_TILE // seq_len, 0, 0)
    row = lambda i: (i, 0)
    const = lambda i: (0, 0)
    resident = pl.Buffered(1)
    return pl.pallas_call(
        functools.partial(_mlp_kernel, seq_len=seq_len),
        grid=(m // MLP_ROW_TILE,),
        in_specs=[
            pl.BlockSpec((MLP_ROW_TILE, D_MODEL), row),
            pl.BlockSpec((None, 1, 6 * D_MODEL), mod_map),
            pl.BlockSpec((D_MODEL, 2 * D_FF), const, pipeline_mode=resident),
            pl.BlockSpec((3, D_FF), const),
            pl.BlockSpec((1, D_FF), const),
            pl.BlockSpec((D_FF, D_MODEL), const, pipeline_mode=resident),
            pl.BlockSpec((1, D_MODEL), const),
            pl.BlockSpec((1, D_MODEL), const),
        ],
        out_specs=pl.BlockSpec((MLP_ROW_TILE, D_MODEL), row),
        out_shape=jax.ShapeDtypeStruct((m, D_MODEL), F32),
        scratch_shapes=[
            pltpu.VMEM((MLP_ROW_TILE, D_MODEL), BF16),
            pltpu.VMEM((MLP_ROW_TILE, D_FF), BF16),
        ],
        compiler_params=pltpu.CompilerParams(
            dimension_semantics=("parallel",), vmem_limit_bytes=MLP_VMEM_LIMIT),
        name="mlp",
    )(x2d, mod3, p['w_up'].astype(BF16), p['conv_w'], p['conv_b'][None], p['w_down'].astype(BF16),
      p['ln2_g'][None], p['ln2_b'][None])


def _trunk_layer(x, mod3, fixed_row, p, w_qkvr, w_gates, layer, ctx=None):
    batch, seq_len, _ = x.shape
    x2d = x.reshape(batch * seq_len, D_MODEL)
    q, k, v, rw = _input_projection(x2d, mod3, w_qkvr, p['rw_mu'], seq_len, fixed_row)
    if ctx is None:
        o_att = _attention(q, k, v, p['da_lambda'], p['da_subln_g'][None], batch, seq_len, layer)
        yg, state = _rwkv(rw, p, batch, seq_len, want_state=True)
    else:
        k_ctx, v_ctx, s_ctx = ctx
        past = k_ctx.shape[1]
        o_att = _attention(q, k, v, p['da_lambda'], p['da_subln_g'][None], batch, seq_len, layer,
                           ctx=(k_ctx.reshape(batch, past, DA_WIDTH), v_ctx.reshape(batch, past, DA_WIDTH)))
        yg, state = _rwkv(rw, p, batch, seq_len, state_in=s_ctx)
    x1 = _merge(x2d, mod3, o_att, yg, w_gates, p, seq_len, fixed_row)
    y = _mlp(x1, mod3, p, seq_len, fixed_row)
    new_ctx = None
    if ctx is None:
        new_ctx = (k.reshape(batch, seq_len, DA_HEADS, 2, DA_HEAD_DIM),
                   v.reshape(batch, seq_len, DA_HEADS, 2 * DA_HEAD_DIM), state)
    return y.reshape(batch, seq_len, D_MODEL), new_ctx


def kernel(x_prompt, x_sample, cache_k, cache_v, state_rwkv, c, c_ctx, w_ada, b_ada, w_in, rw_mu, rw_w0, rw_w_up, rw_a0, rw_a_up, rw_g_up, rw_k_k, rw_k_a, rw_r_k, rw_lnx_g, rw_lnx_b, da_lambda, da_subln_g, w_o_attn, w_o_rwkv, w_out, ln1_g, ln1_b, w_up, conv_w, conv_b, w_down, ln2_g, ln2_b):
    dec_batch = x_sample.shape[0]
    assert dec_batch < MOD_ROWS
    y_prompt, y_sample = x_prompt, x_sample
    new_k, new_v, new_s = [], [], []
    for l in range(DEPTH):
        p = {
            'rw_mu': rw_mu[l], 'rw_w0': rw_w0[l], 'rw_w_up': rw_w_up[l], 'rw_a0': rw_a0[l],
            'rw_a_up': rw_a_up[l], 'rw_g_up': rw_g_up[l], 'rw_k_k': rw_k_k[l], 'rw_k_a': rw_k_a[l],
            'rw_r_k': rw_r_k[l], 'rw_lnx_g': rw_lnx_g[l], 'rw_lnx_b': rw_lnx_b[l],
            'da_lambda': da_lambda[l], 'da_subln_g': da_subln_g[l], 'w_o_attn': w_o_attn[l],
            'w_o_rwkv': w_o_rwkv[l], 'w_out': w_out[l], 'ln1_g': ln1_g[l], 'ln1_b': ln1_b[l],
            'w_up': w_up[l], 'conv_w': conv_w[l], 'conv_b': conv_b[l], 'w_down': w_down[l],
            'ln2_g': ln2_g[l], 'ln2_b': ln2_b[l],
        }
        cvec = jnp.concatenate(
            [c, c_ctx[None], jnp.zeros((MOD_ROWS - dec_batch - 1, D_MODEL), F32)], axis=0)
        mod3 = _modulation(cvec, w_ada[l], b_ada[l][None]).reshape(MOD_ROWS, 1, 6 * D_MODEL)
        w_qkvr = w_in[l][:, :QKVR_COLS].astype(BF16)
        w_gates = w_in[l][:, QKVR_COLS:].astype(BF16)
        y_prompt, ctx_l = _trunk_layer(y_prompt, mod3, dec_batch, p, w_qkvr, w_gates, l)
        new_k.append(ctx_l[0])
        new_v.append(ctx_l[1])
        new_s.append(ctx_l[2])
        y_sample, _ = _trunk_layer(y_sample, mod3, None, p, w_qkvr, w_gates, l,
                                   ctx=(cache_k[:, l], cache_v[:, l], state_rwkv[:, l]))
    return (y_prompt, y_sample, jnp.stack(new_k, axis=1), jnp.stack(new_v, axis=1),
            jnp.stack(new_s, axis=1))
```

```python
import functools
import math

import jax
import jax.numpy as jnp
from jax import lax
from jax.experimental import pallas as pl
from jax.experimental.pallas import tpu as pltpu

F32 = jnp.float32
BF16 = jnp.bfloat16
HIGHEST = lax.Precision.HIGHEST

D_MODEL = 1024
GRID_W = 64
DA_HEADS = 4
DA_HEAD_DIM = 64
DA_WIDTH = DA_HEADS * 2 * DA_HEAD_DIM
ROPE_PAIRS_PER_AXIS = DA_HEAD_DIM // 4
ROPE_BASE = 10000.0
RW_HEADS = 8
RW_HEAD_DIM = 64
RW_WIDTH = RW_HEADS * RW_HEAD_DIM
DECAY_LORA = 64
AAA_LORA = 64
GATE_LORA = 128
RW_COLS = 3 * RW_WIDTH + DECAY_LORA + AAA_LORA + GATE_LORA
RW_LNX_EPS = 64e-5
QKVR_COLS = 3 * DA_WIDTH + RW_COLS
D_FF = 2816
LN_EPS = 1e-5
DEPTH = 1
ALPHA = (2.0 * DEPTH) ** 0.25
LOG2_E = math.log2(math.e)

CHUNK = 64
ATTN_Q_BLOCK = 256
ROW_TILE = 512
INPROJ_ROW_TILE = 1024
SHIFT_COL_TILE = 256
MLP_ROW_TILE = 1024
FF_TILE = 256
MOD_COL_TILE = 768
MOD_ROWS = 16
VMEM_LIMIT = 48 * 1024 * 1024
MLP_VMEM_LIMIT = 56 * 1024 * 1024

_NN = (((1,), (0,)), ((), ()))
_NT = (((1,), (1,)), ((), ()))
_TN = (((0,), (0,)), ((), ()))


def _dot(a, b, dims=_NN, precision=None):
    return lax.dot_general(a, b, dims, precision=precision, preferred_element_type=F32)


def _layer_norm(z, g, b, eps):
    mu = jnp.mean(z, axis=-1, keepdims=True)
    zc = z - mu
    var = jnp.mean(zc * zc, axis=-1, keepdims=True)
    return zc * lax.rsqrt(var + eps) * g + b


def _mod_kernel(c_ref, w_ref, b_ref, o_ref):
    cv = c_ref[...]
    s = cv * jax.nn.sigmoid(cv)
    o_ref[...] = _dot(s, w_ref[...], precision=HIGHEST) + b_ref[...]


def _modulation(cvec, w_ada, b_ada):
    n = w_ada.shape[1]
    return pl.pallas_call(
        _mod_kernel,
        grid=(n // MOD_COL_TILE,),
        in_specs=[
            pl.BlockSpec((MOD_ROWS, D_MODEL), lambda j: (0, 0)),
            pl.BlockSpec((D_MODEL, MOD_COL_TILE), lambda j: (0, j)),
            pl.BlockSpec((1, MOD_COL_TILE), lambda j: (0, j)),
        ],
        out_specs=pl.BlockSpec((MOD_ROWS, MOD_COL_TILE), lambda j: (0, j)),
        out_shape=jax.ShapeDtypeStruct((MOD_ROWS, n), F32),
        compiler_params=pltpu.CompilerParams(
            dimension_semantics=("parallel",), vmem_limit_bytes=VMEM_LIMIT),
        name="mod",
    )(cvec, w_ada, b_ada)


def _mod_row_map(rows_per_batch, fixed_row):
    if fixed_row is not None:
        return lambda i: (fixed_row, 0, 0)
    tiles = rows_per_batch // ROW_TILE
    return lambda i: (i // tiles, 0, 0)


def _inproj_kernel(x_ref, mod_ref, w_ref, mu_ref, q_ref, k_ref, v_ref, rw_ref, *, seq_len):
    mod = mod_ref[...]
    sh1 = mod[:, 0:D_MODEL]
    sc1 = mod[:, D_MODEL:2 * D_MODEL]
    h = (x_ref[...] * (1.0 + sc1) + sh1).astype(BF16)
    q_ref[...] = _dot(h, w_ref[:, 0:DA_WIDTH])
    k_ref[...] = _dot(h, w_ref[:, DA_WIDTH:2 * DA_WIDTH])
    v_ref[...] = _dot(h, w_ref[:, 2 * DA_WIDTH:3 * DA_WIDTH])
    rows = x_ref.shape[0]
    pos = lax.broadcasted_iota(jnp.int32, (rows, 1), 0) & (seq_len - 1)
    first = pos == 0
    last = pos == seq_len - 1
    for j in range(RW_COLS // SHIFT_COL_TILE):
        cols = slice(j * SHIFT_COL_TILE, (j + 1) * SHIFT_COL_TILE)
        rw = _dot(h, w_ref[:, 3 * DA_WIDTH + j * SHIFT_COL_TILE:3 * DA_WIDTH + (j + 1) * SHIFT_COL_TILE])
        prev = jnp.where(first, 0.0, pltpu.roll(rw, 1, 0))
        nxt = jnp.where(last, 0.0, pltpu.roll(rw, rows - 1, 0))
        rw_ref[:, cols] = rw + mu_ref[0:1, cols] * (prev - rw) + mu_ref[1:2, cols] * (nxt - rw)


def _input_projection(x2d, mod3, w_qkvr, rw_mu, seq_len, fixed_row):
    m = x2d.shape[0]
    assert seq_len & (seq_len - 1) == 0 and INPROJ_ROW_TILE % seq_len == 0
    if fixed_row is not None:
        mod_map = lambda i: (fixed_row, 0, 0)
    else:
        mod_map = lambda i: (i * INPROJ_ROW_TILE // seq_len, 0, 0)
    row = lambda i: (i, 0)
    return pl.pallas_call(
        functools.partial(_inproj_kernel, seq_len=seq_len),
        grid=(m // INPROJ_ROW_TILE,),
        in_specs=[
            pl.BlockSpec((INPROJ_ROW_TILE, D_MODEL), row),
            pl.BlockSpec((None, 1, 6 * D_MODEL), mod_map),
            pl.BlockSpec((D_MODEL, QKVR_COLS), lambda i: (0, 0), pipeline_mode=pl.Buffered(1)),
            pl.BlockSpec((2, RW_COLS), lambda i: (0, 0)),
        ],
        out_specs=[
            pl.BlockSpec((INPROJ_ROW_TILE, DA_WIDTH), row),
            pl.BlockSpec((INPROJ_ROW_TILE, DA_WIDTH), row),
            pl.BlockSpec((INPROJ_ROW_TILE, DA_WIDTH), row),
            pl.BlockSpec((INPROJ_ROW_TILE, RW_COLS), row),
        ],
        out_shape=[
            jax.ShapeDtypeStruct((m, DA_WIDTH), F32),
            jax.ShapeDtypeStruct((m, DA_WIDTH), F32),
            jax.ShapeDtypeStruct((m, DA_WIDTH), F32),
            jax.ShapeDtypeStruct((m, RW_COLS), F32),
        ],
        compiler_params=pltpu.CompilerParams(
            dimension_semantics=("parallel",), vmem_limit_bytes=VMEM_LIMIT),
        name="inproj",
    )(x2d, mod3, w_qkvr, rw_mu)


def _rope(x, cos, sin_signed):
    lane = lax.broadcasted_iota(jnp.int32, x.shape, 1)
    partner = jnp.where((lane & 63) < 32, pltpu.roll(x, 96, 1), pltpu.roll(x, 32, 1))
    return x * cos + partner * sin_signed


def _attn_kernel(*refs, has_ctx, seq_len, lam_init):
    if has_ctx:
        q_ref, k_ref, v_ref, kc_ref, vc_ref, cos_ref, sin_ref, lq_ref, g_ref, o_ref = refs
    else:
        q_ref, k_ref, v_ref, lq_ref, g_ref, o_ref = refs
    d = DA_HEAD_DIM
    lq = lq_ref[...]
    lam = (jnp.exp(jnp.sum(lq[0:1] * lq[1:2], axis=-1, keepdims=True))
           - jnp.exp(jnp.sum(lq[2:3] * lq[3:4], axis=-1, keepdims=True)) + lam_init)

    def with_ones(v):
        return jnp.concatenate([v.astype(BF16), jnp.ones(v.shape, BF16)], axis=1)

    k = k_ref[...]
    if has_ctx:
        k = _rope(k, cos_ref[...], sin_ref[...])
    keys = [[k[:, m * d:(m + 1) * d].astype(BF16)] for m in range(2)]
    vals = [with_ones(v_ref[...])]
    if has_ctx:
        kc = kc_ref[...]
        for m in range(2):
            keys[m].append(kc[:, m * d:(m + 1) * d].astype(BF16))
        vals.append(with_ones(vc_ref[...]))
    g = g_ref[...]

    def scores(qb, m):
        rows = slice(qb * ATTN_Q_BLOCK, (qb + 1) * ATTN_Q_BLOCK)
        q = q_ref[rows, :]
        if has_ctx:
            q = _rope(q, cos_ref[rows, :], sin_ref[rows, :])
        qm = (q[:, m * d:(m + 1) * d] * (d ** -0.5 * LOG2_E)).astype(BF16)
        return [_dot(qm, kg, _NT) for kg in keys[m]]

    def attend(ss):
        mx = ss[0].max(axis=-1, keepdims=True)
        for s in ss[1:]:
            mx = jnp.maximum(mx, s.max(axis=-1, keepdims=True))
        acc = None
        for s, vg in zip(ss, vals):
            o = _dot(jnp.exp2(s - mx).astype(BF16), vg)
            acc = o if acc is None else acc + o
        return acc[:, 0:2 * d] / acc[:, 2 * d:4 * d]

    units = [(qb, m) for qb in range(seq_len // ATTN_Q_BLOCK) for m in range(2)]
    pending = scores(*units[0])
    o1 = None
    for i, (qb, m) in enumerate(units):
        following = scores(*units[i + 1]) if i + 1 < len(units) else None
        o = attend(pending)
        pending = following
        if m == 0:
            o1 = o
            continue
        o = o1 - lam * o
        ms = jnp.mean(o * o, axis=-1, keepdims=True)
        rows = slice(qb * ATTN_Q_BLOCK, (qb + 1) * ATTN_Q_BLOCK)
        o_ref[rows, :] = o * lax.rsqrt(ms + LN_EPS) * g * (1.0 - lam_init)


def _rope_tables(n):
    rows = n // GRID_W
    row = jnp.repeat(jnp.arange(rows, dtype=F32), GRID_W)
    col = jnp.tile(jnp.arange(GRID_W, dtype=F32), rows)
    inv = ROPE_BASE ** (-jnp.arange(ROPE_PAIRS_PER_AXIS, dtype=F32) / ROPE_PAIRS_PER_AXIS)
    ang = jnp.concatenate([row[:, None] * inv, col[:, None] * inv], -1)
    cos, sin = jnp.cos(ang), jnp.sin(ang)
    return jnp.tile(cos, (1, 4)), jnp.tile(jnp.concatenate([-sin, sin], -1), (1, 2))


def _attention(q2d, k2d, v2d, da_lambda, subln_g, batch, seq_len, layer, ctx=None):
    has_ctx = ctx is not None
    w = 2 * DA_HEAD_DIM
    head = lambda b, h: (b, h)
    const = lambda b, h: (0, 0)
    in_specs = [pl.BlockSpec((seq_len, w), head)] * 3
    args = [q2d, k2d, v2d]
    if has_ctx:
        kc, vc = ctx
        past = kc.shape[1]
        in_specs += [pl.BlockSpec((None, past, w), lambda b, h: (b, 0, h))] * 2
        in_specs += [pl.BlockSpec((seq_len, w), const)] * 2
        args += [kc, vc, *_rope_tables(seq_len)]
    in_specs += [pl.BlockSpec((4, DA_HEAD_DIM), const), pl.BlockSpec((1, w), const)]
    args += [da_lambda, subln_g]
    lam_init = 0.8 - 0.6 * math.exp(-0.3 * layer)
    return pl.pallas_call(
        functools.partial(_attn_kernel, has_ctx=has_ctx, seq_len=seq_len, lam_init=lam_init),
        grid=(batch, DA_HEADS),
        in_specs=in_specs,
        out_specs=pl.BlockSpec((seq_len, w), head),
        out_shape=jax.ShapeDtypeStruct((batch * seq_len, DA_WIDTH), F32),
        compiler_params=pltpu.CompilerParams(
            dimension_semantics=("parallel", "parallel"), vmem_limit_bytes=VMEM_LIMIT),
        name="attn",
    )(*args)


PAIR = 2 * RW_HEAD_DIM
N_PAIRS = RW_HEADS // 2
FINISH_ROWS = 256
RWKV_GROUP_ROWS = 1024

_M_STRICT = (0, 2)
_M_INCL = (1, 3)
_M_BLOCK8, _M_OFF16, _M_OFF32, _M_OFF64, _M_EYE = 4, 5, 6, 7, 8


def _pair_masks():
    t = jnp.arange(PAIR)[:, None]
    s = jnp.arange(PAIR)[None, :]
    same = lambda n: (t // n) == (s // n)
    head = same(CHUNK)
    masks = [head & (t > s), head & (t >= s), head & (t < s), head & (t <= s),
             same(8), same(16) & ~same(8), same(32) & ~same(16), head & ~same(32), t == s]
    return jnp.stack(masks).astype(BF16)


def _cumsum_matrices():
    t = jnp.arange(CHUNK)[:, None]
    s = jnp.arange(CHUNK)[None, :]
    return jnp.stack([t >= s, t <= s]).astype(BF16)


def _pair_sum_matrix():
    i = jnp.arange(PAIR)
    return ((i[:, None] // RW_HEAD_DIM) == (i[None, :] // RW_HEAD_DIM)).astype(BF16)


def _split3(x):
    hi = x.astype(BF16)
    r1 = x - hi.astype(F32)
    mid = r1.astype(BF16)
    lo = (r1 - mid.astype(F32)).astype(BF16)
    return hi, mid, lo


def _head_sums(x, e_ref):
    rows = x.shape[0]
    xs = jnp.concatenate([x[:, p * PAIR:(p + 1) * PAIR] for p in range(N_PAIRS)], axis=0)
    s = _dot(xs.astype(BF16), e_ref[...])
    return jnp.concatenate([s[p * rows:(p + 1) * rows] for p in range(N_PAIRS)], axis=1)


def _run_schedule(order, **stages):
    for name in order:
        next(stages[name], None)
    for gen in stages.values():
        for _ in gen:
            pass


_STEP_ORDER = "C B A B C B A B C B B A B B B B".split()


def _rwkv_kernel(*refs, seq_len, group, has_state_in, has_state_out):
    refs = list(refs)
    rw_ref = refs.pop(0)
    s0_ref = refs.pop(0) if has_state_in else None
    (w0_ref, wup_ref, a0_ref, aup_ref, gup_ref, kk_ref, ka_ref, rk_ref,
     lng_ref, lnb_ref, e_ref, tri_ref, m_ref) = refs[:13]
    refs = refs[13:]
    o_ref = refs.pop(0)
    so_ref = refs.pop(0) if has_state_out else None
    (y_scr, bv_scr, gr_scr, s_scr, sio_scr, ar_scr, bk_scr, v_scr, ge_scr,
     t_scr, wy_scr, rl_scr) = refs

    C = CHUNK
    N = RW_HEAD_DIM
    W = RW_WIDTH
    P = PAIR
    nc = seq_len // C
    n_steps = group * nc
    total_rows = group * seq_len
    assert nc & (nc - 1) == 0 and n_steps % 2 == 0
    mm = lambda x, y: _dot(x.astype(BF16), y.astype(BF16))
    mmb = lambda x, y: _dot(x, y).astype(BF16)
    zeros_nn = jnp.zeros((N, N), F32)
    chains = [(d, p) for d in range(2) for p in range(N_PAIRS)]
    n = range(len(chains))

    if has_state_in:
        for g in range(group):
            for d, p in chains:
                top = jnp.concatenate([s0_ref[g, d, 2 * p], zeros_nn], axis=1)
                bot = jnp.concatenate([zeros_nn, s0_ref[g, d, 2 * p + 1]], axis=1)
                sio_scr[g, d, p] = jnp.concatenate([top, bot], axis=0)
    for d, p in chains:
        s_scr[d, p] = jnp.zeros((P, P), F32)

    for scr in (ar_scr, bk_scr, v_scr):
        scr[...] = jnp.zeros(scr.shape, scr.dtype)

    def put_block_diag(scr, slot, d, row0, x):
        xb = x.astype(BF16)
        for p in range(N_PAIRS):
            for hh in range(2):
                lanes = slice(hh * N, (hh + 1) * N)
                scr[slot, d, p, row0 + hh * C:row0 + (hh + 1) * C, lanes] = (
                    xb[:, p * P + hh * N:p * P + (hh + 1) * N])

    def step_chunks(i):
        i = jnp.minimum(i, n_steps - 1)
        g = i // nc
        il = i % nc
        return g, il, (il, nc - 1 - il)

    def prep_stages(i, slot):
        g, _, cs = step_chunks(i)
        st = []
        for d, c in enumerate(cs):
            r0 = pl.multiple_of(g * seq_len + c * C, C)
            rows = pl.ds(r0, C)
            xm = rw_ref[rows, :]
            w_lo = xm[:, 3 * W:3 * W + DECAY_LORA]
            a_lo = xm[:, 3 * W + DECAY_LORA:3 * W + DECAY_LORA + AAA_LORA]
            st.append(dict(rows=rows, r=xm[:, 0:W], kr=xm[:, W:2 * W], vr=xm[:, 2 * W:3 * W],
                           g_lo=xm[:, 3 * W + DECAY_LORA + AAA_LORA:RW_COLS],
                           w_up=_dot(jnp.tanh(w_lo).astype(BF16), wup_ref[d]),
                           a_up=_dot(a_lo.astype(BF16), aup_ref[d])))
        yield
        for d, x in enumerate(st):
            wlog = -jax.nn.softplus(-(w0_ref[d:d + 1, :] + x['w_up'])) - 0.5
            x['log_decay'] = -jnp.exp(wlog)
            x['a'] = jax.nn.sigmoid(a0_ref[d:d + 1, :] + x['a_up'])
            x['keff'] = x['kr'] * (1.0 + (x['a'] - 1.0) * ka_ref[...])
            x['kk'] = x['kr'] * kk_ref[...]
            x['sums'] = _head_sums(
                jnp.concatenate([x['kk'] * x['kk'], x['r'] * x['keff'] * rk_ref[...]], axis=0), e_ref)
            if d == 0:
                gr_scr[x['rows'], :] = _dot(jax.nn.sigmoid(x['g_lo']).astype(BF16), gup_ref[...])
            cum = _dot(tri_ref[d], jnp.concatenate(_split3(x['log_decay']), axis=1))
            x['cum'] = cum[:, 0:W] + cum[:, W:2 * W] + cum[:, 2 * W:3 * W]
        yield
        for d, x in enumerate(st):
            kk = x['kk'] / jnp.maximum(jnp.sqrt(x['sums'][0:C]), 1e-12)
            bv_scr[d, x['rows'], :] = x['sums'][C:2 * C] * x['vr']
            cum = x['cum']
            g_in = jnp.exp(cum)
            g_inv = jnp.exp(-cum)
            a_t = -kk * jnp.exp(cum - x['log_decay'])
            r_t = x['r'] * g_in
            b_t = kk * x['a'] * g_inv
            k_t = x['keff'] * g_inv
            ge_scr[slot, d] = g_in[C - 1:C, :] if d == 0 else g_in[0:1, :]
            put_block_diag(ar_scr, slot, d, 0, a_t)
            put_block_diag(ar_scr, slot, d, P, r_t)
            put_block_diag(bk_scr, slot, d, 0, b_t)
            put_block_diag(bk_scr, slot, d, P, k_t)
            put_block_diag(v_scr, slot, d, 0, x['vr'])
        yield

    def local_stages(slot):
        ar = [ar_scr[slot, d, p] for d, p in chains]
        bk = [bk_scr[slot, d, p] for d, p in chains]
        vb = [v_scr[slot, d, p] for d, p in chains]
        strict = [m_ref[_M_STRICT[d]] for d, _ in chains]
        incl = [m_ref[_M_INCL[d]] for d, _ in chains]
        x = [_dot(ar[i], bk[i], _NT).astype(BF16) for i in n]
        yield
        amat = [x[i][0:P, 0:P] * strict[i] for i in n]
        a8 = [a * m_ref[_M_BLOCK8] for a in amat]
        a2 = [mmb(a, a) for a in a8]
        for i, (d, p) in enumerate(chains):
            rl_scr[slot, d, p] = _dot(x[i][0:P, P:2 * P] * strict[i], vb[i])
            wy_scr[slot, d, p] = jnp.concatenate(
                [x[i][P:2 * P, 0:P] * incl[i], x[i][P:2 * P, P:2 * P] * incl[i]], axis=1)
        yield
        a4 = [mmb(a, a) for a in a2]
        ps = [m_ref[_M_EYE] + a for a in a8]
        ps = [t + mmb(t, a) for t, a in zip(ps, a2)]
        yield
        ps = [t + mmb(t, a) for t, a in zip(ps, a4)]
        yield
        for off in (_M_OFF16, _M_OFF32, _M_OFF64):
            ts = [mmb(t, a * m_ref[off]) for t, a in zip(ps, amat)]
            yield
            ps = [t + mmb(q, t) for t, q in zip(ps, ts)]
            if off == _M_OFF64:
                for i, (d, p) in enumerate(chains):
                    t_scr[slot, d, p] = ps[i]
            yield

    def state_stages(i, slot):
        g, il, cs = step_chunks(i)
        ar = [ar_scr[slot, d, p] for d, p in chains]
        bk = [bk_scr[slot, d, p] for d, p in chains]
        vb = [v_scr[slot, d, p] for d, p in chains]
        g_end = [ge_scr[slot, d] for d in range(2)]
        if has_state_in:
            s_old = [jnp.where(il == 0, sio_scr[g, d, p], s_scr[d, p]) for d, p in chains]
        else:
            s_old = [jnp.where(il == 0, 0.0, s_scr[d, p]) for d, p in chains]
        xs = [_dot(ar[i], s_old[i].astype(BF16), _NT) for i in n]
        yield
        u = [mm(t_scr[slot, d, p], xs[i][0:P] + rl_scr[slot, d, p]).astype(BF16)
             for i, (d, p) in enumerate(chains)]
        yield
        uv = [jnp.concatenate([u[i], vb[i]], axis=0) for i in n]
        y = [xs[i][P:2 * P] + _dot(wy_scr[slot, d, p], uv[i]) for i, (d, p) in enumerate(chains)]
        s_new = [s_old[i] + _dot(uv[i], bk[i], _TN) for i in n]
        for i, (d, p) in enumerate(chains):
            sl = slice(p * P, (p + 1) * P)
            rows = pl.ds(pl.multiple_of(g * seq_len + cs[d] * C, C), C)
            y_scr[d, rows, sl] = y[i][0:C] + y[i][C:2 * C]
            s_end = s_new[i] * g_end[d][:, sl]
            s_scr[d, p] = s_end
            if has_state_out:
                sio_scr[g, d, p] = s_end
        yield

    for _ in prep_stages(0, 0):
        pass
    _run_schedule("B A B A B A".split(), A=prep_stages(1, 1), B=local_stages(0))

    def scan_body(j, carry):
        i = 2 * j
        _run_schedule(_STEP_ORDER, C=state_stages(i, 0), B=local_stages(1), A=prep_stages(i + 2, 0))
        _run_schedule(_STEP_ORDER, C=state_stages(i + 1, 1), B=local_stages(0), A=prep_stages(i + 3, 1))
        return carry

    lax.fori_loop(0, n_steps // 2, scan_body, 0)

    def finish_body(i, carry):
        rows = pl.ds(pl.multiple_of(i * FINISH_ROWS, FINISH_ROWS), FINISH_ROWS)
        y = y_scr[0, rows, :] + y_scr[1, rows, :]
        yc = y - _head_sums(y, e_ref) * (1.0 / N)
        var = _head_sums(yc * yc, e_ref) * (1.0 / N)
        yn = yc * lax.rsqrt(var + RW_LNX_EPS) * lng_ref[...] + lnb_ref[...]
        o_ref[rows, :] = (yn + bv_scr[0, rows, :] + bv_scr[1, rows, :]) * gr_scr[rows, :]
        return carry

    lax.fori_loop(0, total_rows // FINISH_ROWS, finish_body, 0)
    if has_state_out:
        for g in range(group):
            for d, p in chains:
                s_pair = sio_scr[g, d, p]
                so_ref[g, d, 2 * p] = s_pair[0:N, 0:N]
                so_ref[g, d, 2 * p + 1] = s_pair[N:P, N:P]


def _rwkv(rw2d, p, batch, seq_len, state_in=None, want_state=False):
    has_state_in = state_in is not None
    assert not (has_state_in and want_state)
    group = max(1, RWKV_GROUP_ROWS // seq_len)
    assert batch % group == 0
    rows = group * seq_len
    const2 = lambda b: (0, 0)
    const3 = lambda b: (0, 0, 0)
    state_spec = pl.BlockSpec((group, 2, RW_HEADS, RW_HEAD_DIM, RW_HEAD_DIM), lambda b: (b, 0, 0, 0, 0))
    in_specs = [pl.BlockSpec((rows, RW_COLS), lambda b: (b, 0))]
    args = [rw2d]
    if has_state_in:
        in_specs.append(state_spec)
        args.append(state_in)
    in_specs += [
        pl.BlockSpec((2, RW_WIDTH), const2),
        pl.BlockSpec((2, DECAY_LORA, RW_WIDTH), const3),
        pl.BlockSpec((2, RW_WIDTH), const2),
        pl.BlockSpec((2, AAA_LORA, RW_WIDTH), const3),
        pl.BlockSpec((GATE_LORA, RW_WIDTH), const2),
        pl.BlockSpec((1, RW_WIDTH), const2),
        pl.BlockSpec((1, RW_WIDTH), const2),
        pl.BlockSpec((1, RW_WIDTH), const2),
        pl.BlockSpec((1, RW_WIDTH), const2),
        pl.BlockSpec((1, RW_WIDTH), const2),
        pl.BlockSpec((PAIR, PAIR), const2),
        pl.BlockSpec((2, CHUNK, CHUNK), const3),
        pl.BlockSpec((9, PAIR, PAIR), const3),
    ]
    args += [p['rw_w0'], p['rw_w_up'].astype(BF16), p['rw_a0'], p['rw_a_up'].astype(BF16),
             p['rw_g_up'].astype(BF16), p['rw_k_k'][None], p['rw_k_a'][None],
             p['rw_r_k'].reshape(1, RW_WIDTH), p['rw_lnx_g'][None], p['rw_lnx_b'][None],
             _pair_sum_matrix(), _cumsum_matrices(), _pair_masks()]
    out_specs = [pl.BlockSpec((rows, RW_WIDTH), lambda b: (b, 0))]
    out_shape = [jax.ShapeDtypeStruct((batch * seq_len, RW_WIDTH), F32)]
    if want_state:
        out_specs.append(state_spec)
        out_shape.append(jax.ShapeDtypeStruct((batch, 2, RW_HEADS, RW_HEAD_DIM, RW_HEAD_DIM), F32))
    per_chain = (2, 2, N_PAIRS)
    outs = pl.pallas_call(
        functools.partial(_rwkv_kernel, seq_len=seq_len, group=group, has_state_in=has_state_in,
                          has_state_out=want_state),
        grid=(batch // group,),
        in_specs=in_specs,
        out_specs=out_specs,
        out_shape=out_shape,
        scratch_shapes=[
            pltpu.VMEM((2, rows, RW_WIDTH), F32),
            pltpu.VMEM((2, rows, RW_WIDTH), F32),
            pltpu.VMEM((rows, RW_WIDTH), F32),
            pltpu.VMEM((2, N_PAIRS, PAIR, PAIR), F32),
            pltpu.VMEM((group, 2, N_PAIRS, PAIR, PAIR), F32),
            pltpu.VMEM(per_chain + (2 * PAIR, PAIR), BF16),
            pltpu.VMEM(per_chain + (2 * PAIR, PAIR), BF16),
            pltpu.VMEM(per_chain + (PAIR, PAIR), BF16),
            pltpu.VMEM((2, 2, 1, RW_WIDTH), F32),
            pltpu.VMEM(per_chain + (PAIR, PAIR), BF16),
            pltpu.VMEM(per_chain + (PAIR, 2 * PAIR), BF16),
            pltpu.VMEM(per_chain + (PAIR, PAIR), F32),
        ],
        compiler_params=pltpu.CompilerParams(
            dimension_semantics=("parallel",), vmem_limit_bytes=VMEM_LIMIT),
        name="rwkv",
    )(*args)
    return outs if want_state else (outs[0], None)


def _merge_kernel(x_ref, mod_ref, oa_ref, yg_ref, wg_ref, woa_ref, wor_ref, wout_ref,
                  g_ref, b_ref, o_ref):
    mod = mod_ref[...]
    sh1 = mod[:, 0:D_MODEL]
    sc1 = mod[:, D_MODEL:2 * D_MODEL]
    g1 = mod[:, 2 * D_MODEL:3 * D_MODEL]
    x = x_ref[...]
    h = (x * (1.0 + sc1) + sh1).astype(BF16)
    gates = jax.nn.sigmoid(_dot(h, wg_ref[...]))
    att = _dot(oa_ref[...].astype(BF16), woa_ref[...])
    rwk = _dot(yg_ref[...].astype(BF16), wor_ref[...])
    merged = gates[:, 0:D_MODEL] * att + gates[:, D_MODEL:2 * D_MODEL] * rwk
    mix = _dot(merged.astype(BF16), wout_ref[...])
    o_ref[...] = _layer_norm(ALPHA * x + g1 * mix, g_ref[...], b_ref[...], LN_EPS)


def _merge(x2d, mod3, o_att, yg, w_gates, p, seq_len, fixed_row):
    m = x2d.shape[0]
    row = lambda i: (i, 0)
    const = lambda i: (0, 0)
    return pl.pallas_call(
        _merge_kernel,
        grid=(m // ROW_TILE,),
        in_specs=[
            pl.BlockSpec((ROW_TILE, D_MODEL), row),
            pl.BlockSpec((None, 1, 6 * D_MODEL), _mod_row_map(seq_len, fixed_row)),
            pl.BlockSpec((ROW_TILE, DA_WIDTH), row),
            pl.BlockSpec((ROW_TILE, RW_WIDTH), row),
            pl.BlockSpec((D_MODEL, 2 * D_MODEL), const),
            pl.BlockSpec((DA_WIDTH, D_MODEL), const),
            pl.BlockSpec((RW_WIDTH, D_MODEL), const),
            pl.BlockSpec((D_MODEL, D_MODEL), const),
            pl.BlockSpec((1, D_MODEL), const),
            pl.BlockSpec((1, D_MODEL), const),
        ],
        out_specs=pl.BlockSpec((ROW_TILE, D_MODEL), row),
        out_shape=jax.ShapeDtypeStruct((m, D_MODEL), F32),
        compiler_params=pltpu.CompilerParams(
            dimension_semantics=("parallel",), vmem_limit_bytes=VMEM_LIMIT),
        name="merge",
    )(x2d, mod3, o_att, yg, w_gates, p['w_o_attn'].astype(BF16), p['w_o_rwkv'].astype(BF16),
      p['w_out'].astype(BF16), p['ln1_g'][None], p['ln1_b'][None])


def _mlp_kernel(x_ref, mod_ref, wup_ref, cw_ref, cb_ref, wd_ref, g_ref, b_ref, o_ref,
                h_scr, act_scr, *, seq_len):
    mod = mod_ref[...]
    sh2 = mod[:, 3 * D_MODEL:4 * D_MODEL]
    sc2 = mod[:, 4 * D_MODEL:5 * D_MODEL]
    g2 = mod[:, 5 * D_MODEL:6 * D_MODEL]
    h_scr[...] = (x_ref[...] * (1.0 + sc2) + sh2).astype(BF16)
    rows = x_ref.shape[0]
    pos = lax.broadcasted_iota(jnp.int32, (rows, 1), 0) & (seq_len - 1)
    first = pos == 0
    last = pos == seq_len - 1
    for j in range(D_FF // FF_TILE):
        cols = slice(j * FF_TILE, (j + 1) * FF_TILE)
        h = h_scr[...]
        u = _dot(h, wup_ref[:, cols])
        val = _dot(h, wup_ref[:, D_FF + j * FF_TILE:D_FF + (j + 1) * FF_TILE])
        prev = jnp.where(first, 0.0, pltpu.roll(u, 1, 0))
        nxt = jnp.where(last, 0.0, pltpu.roll(u, rows - 1, 0))
        cw = cw_ref[:, cols]
        u = prev * cw[0:1, :] + u * cw[1:2, :] + nxt * cw[2:3, :] + cb_ref[:, cols]
        act_scr[:, cols] = (jax.nn.gelu(u) * val).astype(BF16)
    f = _dot(act_scr[...], wd_ref[...])
    o_ref[...] = _layer_norm(ALPHA * x_ref[...] + g2 * f, g_ref[...], b_ref[...], LN_EPS)


def _mlp(x2d, mod3, p, seq_len, fixed_row):
    m = x2d.shape[0]
    assert seq_len & (seq_len - 1) == 0 and MLP_ROW_TILE % seq_len == 0
    if fixed_row is not None:
        mod_map = lambda i: (fixed_row, 0, 0)
    else:
        mod_map = lambda i: (i * MLP_ROW_TILE // seq_len, 0, 0)
    row = lambda i: (i, 0)
    const = lambda i: (0, 0)
    resident = pl.Buffered(1)
    return pl.pallas_call(
        functools.partial(_mlp_kernel, seq_len=seq_len),
        grid=(m // MLP_ROW_TILE,),
        in_specs=[
            pl.BlockSpec((MLP_ROW_TILE, D_MODEL), row),
            pl.BlockSpec((None, 1, 6 * D_MODEL), mod_map),
            pl.BlockSpec((D_MODEL, 2 * D_FF), const, pipeline_mode=resident),
            pl.BlockSpec((3, D_FF), const),
            pl.BlockSpec((1, D_FF), const),
            pl.BlockSpec((D_FF, D_MODEL), const, pipeline_mode=resident),
            pl.BlockSpec((1, D_MODEL), const),
            pl.BlockSpec((1, D_MODEL), const),
        ],
        out_specs=pl.BlockSpec((MLP_ROW_TILE, D_MODEL), row),
        out_shape=jax.ShapeDtypeStruct((m, D_MODEL), F32),
        scratch_shapes=[
            pltpu.VMEM((MLP_ROW_TILE, D_MODEL), BF16),
            pltpu.VMEM((MLP_ROW_TILE, D_FF), BF16),
        ],
        compiler_params=pltpu.CompilerParams(
            dimension_semantics=("parallel",), vmem_limit_bytes=MLP_VMEM_LIMIT),
        name="mlp",
    )(x2d, mod3, p['w_up'].astype(BF16), p['conv_w'], p['conv_b'][None], p['w_down'].astype(BF16),
      p['ln2_g'][None], p['ln2_b'][None])


def _trunk_layer(x, mod3, fixed_row, p, w_qkvr, w_gates, layer, ctx=None):
    batch, seq_len, _ = x.shape
    x2d = x.reshape(batch * seq_len, D_MODEL)
    q, k, v, rw = _input_projection(x2d, mod3, w_qkvr, p['rw_mu'], seq_len, fixed_row)
    if ctx is None:
        o_att = _attention(q, k, v, p['da_lambda'], p['da_subln_g'][None], batch, seq_len, layer)
        yg, state = _rwkv(rw, p, batch, seq_len, want_state=True)
    else:
        k_ctx, v_ctx, s_ctx = ctx
        past = k_ctx.shape[1]
        o_att = _attention(q, k, v, p['da_lambda'], p['da_subln_g'][None], batch, seq_len, layer,
                           ctx=(k_ctx.reshape(batch, past, DA_WIDTH), v_ctx.reshape(batch, past, DA_WIDTH)))
        yg, state = _rwkv(rw, p, batch, seq_len, state_in=s_ctx)
    x1 = _merge(x2d, mod3, o_att, yg, w_gates, p, seq_len, fixed_row)
    y = _mlp(x1, mod3, p, seq_len, fixed_row)
    new_ctx = None
    if ctx is None:
        new_ctx = (k.reshape(batch, seq_len, DA_HEADS, 2, DA_HEAD_DIM),
                   v.reshape(batch, seq_len, DA_HEADS, 2 * DA_HEAD_DIM), state)
    return y.reshape(batch, seq_len, D_MODEL), new_ctx


def kernel(x_prompt, x_sample, cache_k, cache_v, state_rwkv, c, c_ctx, w_ada, b_ada, w_in, rw_mu, rw_w0, rw_w_up, rw_a0, rw_a_up, rw_g_up, rw_k_k, rw_k_a, rw_r_k, rw_lnx_g, rw_lnx_b, da_lambda, da_subln_g, w_o_attn, w_o_rwkv, w_out, ln1_g, ln1_b, w_up, conv_w, conv_b, w_down, ln2_g, ln2_b):
    dec_batch = x_sample.shape[0]
    assert dec_batch < MOD_ROWS
    y_prompt, y_sample = x_prompt, x_sample
    new_k, new_v, new_s = [], [], []
    for l in range(DEPTH):
        p = {
            'rw_mu': rw_mu[l], 'rw_w0': rw_w0[l], 'rw_w_up': rw_w_up[l], 'rw_a0': rw_a0[l],
            'rw_a_up': rw_a_up[l], 'rw_g_up': rw_g_up[l], 'rw_k_k': rw_k_k[l], 'rw_k_a': rw_k_a[l],
            'rw_r_k': rw_r_k[l], 'rw_lnx_g': rw_lnx_g[l], 'rw_lnx_b': rw_lnx_b[l],
            'da_lambda': da_lambda[l], 'da_subln_g': da_subln_g[l], 'w_o_attn': w_o_attn[l],
            'w_o_rwkv': w_o_rwkv[l], 'w_out': w_out[l], 'ln1_g': ln1_g[l], 'ln1_b': ln1_b[l],
            'w_up': w_up[l], 'conv_w': conv_w[l], 'conv_b': conv_b[l], 'w_down': w_down[l],
            'ln2_g': ln2_g[l], 'ln2_b': ln2_b[l],
        }
        cvec = jnp.concatenate(
            [c, c_ctx[None], jnp.zeros((MOD_ROWS - dec_batch - 1, D_MODEL), F32)], axis=0)
        mod3 = _modulation(cvec, w_ada[l], b_ada[l][None]).reshape(MOD_ROWS, 1, 6 * D_MODEL)
        w_qkvr = w_in[l][:, :QKVR_COLS].astype(BF16)
        w_gates = w_in[l][:, QKVR_COLS:].astype(BF16)
        y_prompt, ctx_l = _trunk_layer(y_prompt, mod3, dec_batch, p, w_qkvr, w_gates, l)
        new_k.append(ctx_l[0])
        new_v.append(ctx_l[1])
        new_s.append(ctx_l[2])
        y_sample, _ = _trunk_layer(y_sample, mod3, None, p, w_qkvr, w_gates, l,
                                   ctx=(cache_k[:, l], cache_v[:, l], state_rwkv[:, l]))
    return (y_prompt, y_sample, jnp.stack(new_k, axis=1), jnp.stack(new_v, axis=1),
            jnp.stack(new_s, axis=1))
```

```python
import functools
import math

import jax
import jax.numpy as jnp
from jax import lax
from jax.experimental import pallas as pl
from jax.experimental.pallas import tpu as pltpu

F32 = jnp.float32
BF16 = jnp.bfloat16
HIGHEST = lax.Precision.HIGHEST

D_MODEL = 1024
GRID_W = 64
DA_HEADS = 4
DA_HEAD_DIM = 64
DA_WIDTH = DA_HEADS * 2 * DA_HEAD_DIM
ROPE_PAIRS_PER_AXIS = DA_HEAD_DIM // 4
ROPE_BASE = 10000.0
RW_HEADS = 8
RW_HEAD_DIM = 64
RW_WIDTH = RW_HEADS * RW_HEAD_DIM
DECAY_LORA = 64
AAA_LORA = 64
GATE_LORA = 128
RW_COLS = 3 * RW_WIDTH + DECAY_LORA + AAA_LORA + GATE_LORA
RW_LNX_EPS = 64e-5
QKVR_COLS = 3 * DA_WIDTH + RW_COLS
N_IN = QKVR_COLS + 2 * D_MODEL
D_FF = 2816
LN_EPS = 1e-5
DEPTH = 1
ALPHA = (2.0 * DEPTH) ** 0.25
LOG2_E = math.log2(math.e)

CHUNK = 64
ATTN_Q_BLOCK = 256
ATTN_MAX_ROWS = 1024
ROW_TILE = 512
SHIFT_COL_TILE = 256
MLP_ROW_TILE = 1024
FF_TILE = 256
MOD_COL_TILE = 768
MOD_ROWS = 16
VMEM_LIMIT = 48 * 1024 * 1024
MLP_VMEM_LIMIT = 56 * 1024 * 1024

_NN = (((1,), (0,)), ((), ()))
_NT = (((1,), (1,)), ((), ()))
_TN = (((0,), (0,)), ((), ()))


def _dot(a, b, dims=_NN, precision=None):
    return lax.dot_general(a, b, dims, precision=precision, preferred_element_type=F32)


def _layer_norm(z, g, b, eps):
    mu = jnp.mean(z, axis=-1, keepdims=True)
    zc = z - mu
    var = jnp.mean(zc * zc, axis=-1, keepdims=True)
    return zc * lax.rsqrt(var + eps) * g + b


def _mod_kernel(c_ref, w_ref, b_ref, o_ref):
    cv = c_ref[...]
    s = cv * jax.nn.sigmoid(cv)
    o_ref[...] = _dot(s, w_ref[...], precision=HIGHEST) + b_ref[...]


def _modulation(cvec, w_ada, b_ada):
    n = w_ada.shape[1]
    return pl.pallas_call(
        _mod_kernel,
        grid=(n // MOD_COL_TILE,),
        in_specs=[
            pl.BlockSpec((MOD_ROWS, D_MODEL), lambda j: (0, 0)),
            pl.BlockSpec((D_MODEL, MOD_COL_TILE), lambda j: (0, j)),
            pl.BlockSpec((1, MOD_COL_TILE), lambda j: (0, j)),
        ],
        out_specs=pl.BlockSpec((MOD_ROWS, MOD_COL_TILE), lambda j: (0, j)),
        out_shape=jax.ShapeDtypeStruct((MOD_ROWS, n), F32),
        compiler_params=pltpu.CompilerParams(
            dimension_semantics=("parallel",), vmem_limit_bytes=VMEM_LIMIT),
        name="mod",
    )(cvec, w_ada, b_ada)


def _mod_row_map(rows_per_batch, fixed_row):
    if fixed_row is not None:
        return lambda i: (fixed_row, 0, 0)
    tiles = rows_per_batch // ROW_TILE
    return lambda i: (i // tiles, 0, 0)


def _inproj_kernel(x_ref, mod_ref, w_ref, mu_ref, q_ref, k_ref, v_ref, rw_ref, *, seq_len):
    mod = mod_ref[...]
    sh1 = mod[:, 0:D_MODEL]
    sc1 = mod[:, D_MODEL:2 * D_MODEL]
    h = (x_ref[...] * (1.0 + sc1) + sh1).astype(BF16)
    q_ref[...] = _dot(h, w_ref[:, 0:DA_WIDTH])
    k_ref[...] = _dot(h, w_ref[:, DA_WIDTH:2 * DA_WIDTH])
    v_ref[...] = _dot(h, w_ref[:, 2 * DA_WIDTH:3 * DA_WIDTH])
    rows = x_ref.shape[0]
    pos = lax.broadcasted_iota(jnp.int32, (rows, 1), 0) & (seq_len - 1)
    first = pos == 0
    last = pos == seq_len - 1
    for j in range(RW_COLS // SHIFT_COL_TILE):
        cols = slice(j * SHIFT_COL_TILE, (j + 1) * SHIFT_COL_TILE)
        rw = _dot(h, w_ref[:, 3 * DA_WIDTH + j * SHIFT_COL_TILE:3 * DA_WIDTH + (j + 1) * SHIFT_COL_TILE])
        prev = jnp.where(first, 0.0, pltpu.roll(rw, 1, 0))
        nxt = jnp.where(last, 0.0, pltpu.roll(rw, rows - 1, 0))
        rw_ref[:, cols] = rw + mu_ref[0:1, cols] * (prev - rw) + mu_ref[1:2, cols] * (nxt - rw)


def _input_projection(x2d, mod3, w_in, rw_mu, seq_len, fixed_row):
    m = x2d.shape[0]
    tile = max(seq_len, ROW_TILE)
    assert seq_len & (seq_len - 1) == 0 and tile % seq_len == 0
    if fixed_row is not None:
        mod_map = lambda i: (fixed_row, 0, 0)
    else:
        mod_map = lambda i: (i * tile // seq_len, 0, 0)
    row = lambda i: (i, 0)
    return pl.pallas_call(
        functools.partial(_inproj_kernel, seq_len=seq_len),
        grid=(m // tile,),
        in_specs=[
            pl.BlockSpec((tile, D_MODEL), row),
            pl.BlockSpec((None, 1, 6 * D_MODEL), mod_map),
            pl.BlockSpec((D_MODEL, N_IN), lambda i: (0, 0), pipeline_mode=pl.Buffered(1)),
            pl.BlockSpec((2, RW_COLS), lambda i: (0, 0)),
        ],
        out_specs=[
            pl.BlockSpec((tile, DA_WIDTH), row),
            pl.BlockSpec((tile, DA_WIDTH), row),
            pl.BlockSpec((tile, DA_WIDTH), row),
            pl.BlockSpec((tile, RW_COLS), row),
        ],
        out_shape=[
            jax.ShapeDtypeStruct((m, DA_WIDTH), F32),
            jax.ShapeDtypeStruct((m, DA_WIDTH), F32),
            jax.ShapeDtypeStruct((m, DA_WIDTH), F32),
            jax.ShapeDtypeStruct((m, RW_COLS), F32),
        ],
        compiler_params=pltpu.CompilerParams(
            dimension_semantics=("parallel",), vmem_limit_bytes=VMEM_LIMIT),
        name="inproj",
    )(x2d, mod3, w_in, rw_mu)


def _rope(x, cos, sin_signed):
    lane = lax.broadcasted_iota(jnp.int32, x.shape, 1)
    partner = jnp.where((lane & 63) < 32, pltpu.roll(x, 96, 1), pltpu.roll(x, 32, 1))
    return x * cos + partner * sin_signed


def _attn_kernel(*refs, has_ctx, seq_len, heads, lam_init):
    if has_ctx:
        q_ref, k_ref, v_ref, kc_ref, vc_ref, cos_ref, sin_ref, lq_ref, g_ref, o_ref = refs
    else:
        q_ref, k_ref, v_ref, lq_ref, g_ref, o_ref = refs
    d = DA_HEAD_DIM
    lq = lq_ref[...]
    lam = (jnp.exp(jnp.sum(lq[0:1] * lq[1:2], axis=-1, keepdims=True))
           - jnp.exp(jnp.sum(lq[2:3] * lq[3:4], axis=-1, keepdims=True)) + lam_init)

    def with_ones(v):
        return jnp.concatenate([v.astype(BF16), jnp.ones(v.shape, BF16)], axis=1)

    keys, vals = [], []
    for h in range(heads):
        lanes = slice(h * 2 * d, (h + 1) * 2 * d)
        k = k_ref[:, lanes]
        if has_ctx:
            k = _rope(k, cos_ref[...], sin_ref[...])
        keys.append([[k[:, m * d:(m + 1) * d].astype(BF16)] for m in range(2)])
        vals.append([with_ones(v_ref[:, lanes])])
        if has_ctx:
            kc = kc_ref[:, lanes]
            for m in range(2):
                keys[h][m].append(kc[:, m * d:(m + 1) * d].astype(BF16))
            vals[h].append(with_ones(vc_ref[:, lanes]))
    g = g_ref[...]

    def scores(h, qb, m):
        rows = slice(qb * ATTN_Q_BLOCK, (qb + 1) * ATTN_Q_BLOCK)
        q = q_ref[rows, h * 2 * d:(h + 1) * 2 * d]
        if has_ctx:
            q = _rope(q, cos_ref[rows, :], sin_ref[rows, :])
        qm = (q[:, m * d:(m + 1) * d] * (d ** -0.5 * LOG2_E)).astype(BF16)
        return [_dot(qm, kg, _NT) for kg in keys[h][m]]

    def attend(h, ss):
        mx = ss[0].max(axis=-1, keepdims=True)
        for s in ss[1:]:
            mx = jnp.maximum(mx, s.max(axis=-1, keepdims=True))
        acc = None
        for s, vg in zip(ss, vals[h]):
            o = _dot(jnp.exp2(s - mx).astype(BF16), vg)
            acc = o if acc is None else acc + o
        return acc[:, 0:2 * d] / acc[:, 2 * d:4 * d]

    units = [(h, qb, m) for h in range(heads) for qb in range(seq_len // ATTN_Q_BLOCK)
             for m in range(2)]
    pending = scores(*units[0])
    o1 = None
    for i, (h, qb, m) in enumerate(units):
        following = scores(*units[i + 1]) if i + 1 < len(units) else None
        o = attend(h, pending)
        pending = following
        if m == 0:
            o1 = o
            continue
        o = o1 - lam * o
        ms = jnp.mean(o * o, axis=-1, keepdims=True)
        rows = slice(qb * ATTN_Q_BLOCK, (qb + 1) * ATTN_Q_BLOCK)
        o_ref[rows, h * 2 * d:(h + 1) * 2 * d] = o * lax.rsqrt(ms + LN_EPS) * g * (1.0 - lam_init)


def _rope_tables(n):
    rows = n // GRID_W
    row = jnp.repeat(jnp.arange(rows, dtype=F32), GRID_W)
    col = jnp.tile(jnp.arange(GRID_W, dtype=F32), rows)
    inv = ROPE_BASE ** (-jnp.arange(ROPE_PAIRS_PER_AXIS, dtype=F32) / ROPE_PAIRS_PER_AXIS)
    ang = jnp.concatenate([row[:, None] * inv, col[:, None] * inv], -1)
    cos, sin = jnp.cos(ang), jnp.sin(ang)
    return jnp.tile(cos, (1, 4)), jnp.tile(jnp.concatenate([-sin, sin], -1), (1, 2))


def _attention(q2d, k2d, v2d, da_lambda, subln_g, batch, seq_len, layer, ctx=None):
    has_ctx = ctx is not None
    w = 2 * DA_HEAD_DIM
    heads = DA_HEADS if seq_len * DA_HEADS <= ATTN_MAX_ROWS else 1
    head = lambda b, h: (b, h)
    const = lambda b, h: (0, 0)
    in_specs = [pl.BlockSpec((seq_len, heads * w), head)] * 3
    args = [q2d, k2d, v2d]
    if has_ctx:
        kc, vc = ctx
        past = kc.shape[1]
        in_specs += [pl.BlockSpec((None, past, heads * w), lambda b, h: (b, 0, h))] * 2
        in_specs += [pl.BlockSpec((seq_len, w), const)] * 2
        args += [kc, vc, *_rope_tables(seq_len)]
    in_specs += [pl.BlockSpec((4, DA_HEAD_DIM), const), pl.BlockSpec((1, w), const)]
    args += [da_lambda, subln_g]
    lam_init = 0.8 - 0.6 * math.exp(-0.3 * layer)
    return pl.pallas_call(
        functools.partial(_attn_kernel, has_ctx=has_ctx, seq_len=seq_len, heads=heads,
                          lam_init=lam_init),
        grid=(batch, DA_HEADS // heads),
        in_specs=in_specs,
        out_specs=pl.BlockSpec((seq_len, heads * w), head),
        out_shape=jax.ShapeDtypeStruct((batch * seq_len, DA_WIDTH), F32),
        compiler_params=pltpu.CompilerParams(
            dimension_semantics=("parallel", "parallel"), vmem_limit_bytes=VMEM_LIMIT),
        name="attn",
    )(*args)


PAIR = 2 * RW_HEAD_DIM
N_PAIRS = RW_HEADS // 2
FINISH_ROWS = 256
RWKV_GROUP_ROWS = 1024

_M_STRICT = (0, 2)
_M_INCL = (1, 3)
_M_BLOCK8, _M_OFF16, _M_OFF32, _M_OFF64, _M_EYE = 4, 5, 6, 7, 8


def _pair_masks():
    t = jnp.arange(PAIR)[:, None]
    s = jnp.arange(PAIR)[None, :]
    same = lambda n: (t // n) == (s // n)
    head = same(CHUNK)
    masks = [head & (t > s), head & (t >= s), head & (t < s), head & (t <= s),
             same(8), same(16) & ~same(8), same(32) & ~same(16), head & ~same(32), t == s]
    return jnp.stack(masks).astype(BF16)


def _cumsum_matrices():
    t = jnp.arange(CHUNK)[:, None]
    s = jnp.arange(CHUNK)[None, :]
    return jnp.stack([t >= s, t <= s]).astype(BF16)


def _pair_sum_matrix():
    i = jnp.arange(PAIR)
    return ((i[:, None] // RW_HEAD_DIM) == (i[None, :] // RW_HEAD_DIM)).astype(BF16)


def _split2(x):
    hi = x.astype(BF16)
    lo = (x - hi.astype(F32)).astype(BF16)
    return hi, lo


def _head_sums(x, e_ref):
    rows = x.shape[0]
    xs = jnp.concatenate([x[:, p * PAIR:(p + 1) * PAIR] for p in range(N_PAIRS)], axis=0)
    s = _dot(xs.astype(BF16), e_ref[...])
    return jnp.concatenate([s[p * rows:(p + 1) * rows] for p in range(N_PAIRS)], axis=1)


def _run_schedule(order, **stages):
    for name in order:
        next(stages[name], None)
    for gen in stages.values():
        for _ in gen:
            pass


_STEP_ORDER = "B B B B C B A B C B A B C B A B".split()


def _rwkv_kernel(*refs, seq_len, group, has_state_in, has_state_out):
    refs = list(refs)
    rw_ref = refs.pop(0)
    s0_ref = refs.pop(0) if has_state_in else None
    (w0_ref, wup_ref, a0_ref, aup_ref, gup_ref, kk_ref, ka_ref, rk_ref,
     lng_ref, lnb_ref, e_ref, tri_ref, m_ref) = refs[:13]
    refs = refs[13:]
    o_ref = refs.pop(0)
    so_ref = refs.pop(0) if has_state_out else None
    (y_scr, bv_scr, gr_scr, s_scr, sio_scr, ar_scr, bk_scr, v_scr, ge_scr,
     t_scr, wy_scr, rl_scr) = refs

    C = CHUNK
    N = RW_HEAD_DIM
    W = RW_WIDTH
    P = PAIR
    nc = seq_len // C
    n_steps = group * nc
    total_rows = group * seq_len
    assert nc & (nc - 1) == 0 and n_steps % 2 == 0
    mm = lambda x, y: _dot(x.astype(BF16), y.astype(BF16))
    mmb = lambda x, y: _dot(x, y).astype(BF16)
    zeros_nn = jnp.zeros((N, N), F32)
    chains = [(d, p) for d in range(2) for p in range(N_PAIRS)]
    n = range(len(chains))

    if has_state_in:
        for g in range(group):
            for d, p in chains:
                top = jnp.concatenate([s0_ref[g, d, 2 * p], zeros_nn], axis=1)
                bot = jnp.concatenate([zeros_nn, s0_ref[g, d, 2 * p + 1]], axis=1)
                sio_scr[g, d, p] = jnp.concatenate([top, bot], axis=0)
    for d, p in chains:
        s_scr[d, p] = jnp.zeros((P, P), F32)

    for scr in (ar_scr, bk_scr, v_scr):
        scr[...] = jnp.zeros(scr.shape, scr.dtype)

    def put_block_diag(scr, slot, d, row0, x):
        xb = x.astype(BF16)
        for p in range(N_PAIRS):
            for hh in range(2):
                lanes = slice(hh * N, (hh + 1) * N)
                scr[slot, d, p, row0 + hh * C:row0 + (hh + 1) * C, lanes] = (
                    xb[:, p * P + hh * N:p * P + (hh + 1) * N])

    def step_chunks(i):
        i = jnp.minimum(i, n_steps - 1)
        g = i // nc
        il = i % nc
        return g, il, (il, nc - 1 - il)

    def prep_stages(i, slot):
        g, _, cs = step_chunks(i)
        st = []
        for d, c in enumerate(cs):
            r0 = pl.multiple_of(g * seq_len + c * C, C)
            rows = pl.ds(r0, C)
            xm = rw_ref[rows, :]
            w_lo = xm[:, 3 * W:3 * W + DECAY_LORA]
            a_lo = xm[:, 3 * W + DECAY_LORA:3 * W + DECAY_LORA + AAA_LORA]
            st.append(dict(rows=rows, r=xm[:, 0:W], kr=xm[:, W:2 * W], vr=xm[:, 2 * W:3 * W],
                           g_lo=xm[:, 3 * W + DECAY_LORA + AAA_LORA:RW_COLS],
                           w_up=_dot(jnp.tanh(w_lo).astype(BF16), wup_ref[d]),
                           a_up=_dot(a_lo.astype(BF16), aup_ref[d])))
        yield
        for d, x in enumerate(st):
            x['log_decay'] = -math.exp(-0.5) * jax.nn.sigmoid(w0_ref[d:d + 1, :] + x['w_up'])
            x['a'] = jax.nn.sigmoid(a0_ref[d:d + 1, :] + x['a_up'])
            x['keff'] = x['kr'] * (1.0 + (x['a'] - 1.0) * ka_ref[...])
            x['kk'] = x['kr'] * kk_ref[...]
            x['sums'] = _head_sums(
                jnp.concatenate([x['kk'] * x['kk'], x['r'] * x['keff'] * rk_ref[...]], axis=0), e_ref)
            if d == 0:
                gr_scr[x['rows'], :] = _dot(jax.nn.sigmoid(x['g_lo']).astype(BF16), gup_ref[...])
            cum = _dot(tri_ref[d], jnp.concatenate(_split2(x['log_decay']), axis=1))
            x['cum'] = cum[:, 0:W] + cum[:, W:2 * W]
        yield
        for d, x in enumerate(st):
            kk = x['kk'] / jnp.maximum(jnp.sqrt(x['sums'][0:C]), 1e-12)
            bv_scr[d, x['rows'], :] = x['sums'][C:2 * C] * x['vr']
            cum = x['cum']
            g_in = jnp.exp(cum)
            g_inv = jnp.exp(-cum)
            a_t = -kk * jnp.exp(cum - x['log_decay'])
            r_t = x['r'] * g_in
            b_t = kk * x['a'] * g_inv
            k_t = x['keff'] * g_inv
            ge_scr[slot, d] = g_in[C - 1:C, :] if d == 0 else g_in[0:1, :]
            put_block_diag(ar_scr, slot, d, 0, a_t)
            put_block_diag(ar_scr, slot, d, P, r_t)
            put_block_diag(bk_scr, slot, d, 0, b_t)
            put_block_diag(bk_scr, slot, d, P, k_t)
            put_block_diag(v_scr, slot, d, 0, x['vr'])
        yield

    def local_stages(slot):
        ar = [ar_scr[slot, d, p] for d, p in chains]
        bk = [bk_scr[slot, d, p] for d, p in chains]
        vb = [v_scr[slot, d, p] for d, p in chains]
        strict = [m_ref[_M_STRICT[d]] for d, _ in chains]
        incl = [m_ref[_M_INCL[d]] for d, _ in chains]
        x = [_dot(ar[i], bk[i], _NT).astype(BF16) for i in n]
        yield
        amat = [x[i][0:P, 0:P] * strict[i] for i in n]
        a8 = [a * m_ref[_M_BLOCK8] for a in amat]
        a2 = [mmb(a, a) for a in a8]
        for i, (d, p) in enumerate(chains):
            rl_scr[slot, d, p] = _dot(x[i][0:P, P:2 * P] * strict[i], vb[i])
            wy_scr[slot, d, p] = jnp.concatenate(
                [x[i][P:2 * P, 0:P] * incl[i], x[i][P:2 * P, P:2 * P] * incl[i]], axis=1)
        yield
        a4 = [mmb(a, a) for a in a2]
        ps = [m_ref[_M_EYE] + a for a in a8]
        ps = [t + mmb(t, a) for t, a in zip(ps, a2)]
        yield
        ps = [t + mmb(t, a) for t, a in zip(ps, a4)]
        yield
        for off in (_M_OFF16, _M_OFF32, _M_OFF64):
            ts = [mmb(t, a * m_ref[off]) for t, a in zip(ps, amat)]
            yield
            ps = [t + mmb(q, t) for t, q in zip(ps, ts)]
            if off == _M_OFF64:
                for i, (d, p) in enumerate(chains):
                    t_scr[slot, d, p] = ps[i]
            yield

    def state_stages(i, slot):
        g, il, cs = step_chunks(i)
        ar = [ar_scr[slot, d, p] for d, p in chains]
        bk = [bk_scr[slot, d, p] for d, p in chains]
        vb = [v_scr[slot, d, p] for d, p in chains]
        g_end = [ge_scr[slot, d] for d in range(2)]
        if has_state_in:
            s_old = [jnp.where(il == 0, sio_scr[g, d, p], s_scr[d, p]) for d, p in chains]
        else:
            s_old = [jnp.where(il == 0, 0.0, s_scr[d, p]) for d, p in chains]
        xs = [_dot(ar[i], s_old[i].astype(BF16), _NT) for i in n]
        yield
        u = [mm(t_scr[slot, d, p], xs[i][0:P] + rl_scr[slot, d, p]).astype(BF16)
             for i, (d, p) in enumerate(chains)]
        yield
        uv = [jnp.concatenate([u[i], vb[i]], axis=0) for i in n]
        y = [xs[i][P:2 * P] + _dot(wy_scr[slot, d, p], uv[i]) for i, (d, p) in enumerate(chains)]
        s_new = [s_old[i] + _dot(uv[i], bk[i], _TN) for i in n]
        for i, (d, p) in enumerate(chains):
            sl = slice(p * P, (p + 1) * P)
            rows = pl.ds(pl.multiple_of(g * seq_len + cs[d] * C, C), C)
            y_scr[d, rows, sl] = y[i][0:C] + y[i][C:2 * C]
            s_end = s_new[i] * g_end[d][:, sl]
            s_scr[d, p] = s_end
            if has_state_out:
                sio_scr[g, d, p] = s_end
        yield

    for _ in prep_stages(0, 0):
        pass
    _run_schedule("B A B A B A".split(), A=prep_stages(1, 1), B=local_stages(0))

    def scan_body(j, carry):
        i = 2 * j
        _run_schedule(_STEP_ORDER, C=state_stages(i, 0), B=local_stages(1), A=prep_stages(i + 2, 0))
        _run_schedule(_STEP_ORDER, C=state_stages(i + 1, 1), B=local_stages(0), A=prep_stages(i + 3, 1))
        return carry

    lax.fori_loop(0, n_steps // 2, scan_body, 0)

    def finish_body(i, carry):
        rows = pl.ds(pl.multiple_of(i * FINISH_ROWS, FINISH_ROWS), FINISH_ROWS)
        y = y_scr[0, rows, :] + y_scr[1, rows, :]
        yc = y - _head_sums(y, e_ref) * (1.0 / N)
        var = _head_sums(yc * yc, e_ref) * (1.0 / N)
        yn = yc * lax.rsqrt(var + RW_LNX_EPS) * lng_ref[...] + lnb_ref[...]
        o_ref[rows, :] = (yn + bv_scr[0, rows, :] + bv_scr[1, rows, :]) * gr_scr[rows, :]
        return carry

    lax.fori_loop(0, total_rows // FINISH_ROWS, finish_body, 0)
    if has_state_out:
        for g in range(group):
            for d, p in chains:
                s_pair = sio_scr[g, d, p]
                so_ref[g, d, 2 * p] = s_pair[0:N, 0:N]
                so_ref[g, d, 2 * p + 1] = s_pair[N:P, N:P]


def _rwkv(rw2d, p, batch, seq_len, state_in=None, want_state=False):
    has_state_in = state_in is not None
    assert not (has_state_in and want_state)
    group = max(1, RWKV_GROUP_ROWS // seq_len)
    assert batch % group == 0
    rows = group * seq_len
    const2 = lambda b: (0, 0)
    const3 = lambda b: (0, 0, 0)
    state_spec = pl.BlockSpec((group, 2, RW_HEADS, RW_HEAD_DIM, RW_HEAD_DIM), lambda b: (b, 0, 0, 0, 0))
    in_specs = [pl.BlockSpec((rows, RW_COLS), lambda b: (b, 0))]
    args = [rw2d]
    if has_state_in:
        in_specs.append(state_spec)
        args.append(state_in)
    in_specs += [
        pl.BlockSpec((2, RW_WIDTH), const2),
        pl.BlockSpec((2, DECAY_LORA, RW_WIDTH), const3),
        pl.BlockSpec((2, RW_WIDTH), const2),
        pl.BlockSpec((2, AAA_LORA, RW_WIDTH), const3),
        pl.BlockSpec((GATE_LORA, RW_WIDTH), const2),
        pl.BlockSpec((1, RW_WIDTH), const2),
        pl.BlockSpec((1, RW_WIDTH), const2),
        pl.BlockSpec((1, RW_WIDTH), const2),
        pl.BlockSpec((1, RW_WIDTH), const2),
        pl.BlockSpec((1, RW_WIDTH), const2),
        pl.BlockSpec((PAIR, PAIR), const2),
        pl.BlockSpec((2, CHUNK, CHUNK), const3),
        pl.BlockSpec((9, PAIR, PAIR), const3),
    ]
    args += [p['rw_w0'], p['rw_w_up'].astype(BF16), p['rw_a0'], p['rw_a_up'].astype(BF16),
             p['rw_g_up'].astype(BF16), p['rw_k_k'][None], p['rw_k_a'][None],
             p['rw_r_k'].reshape(1, RW_WIDTH), p['rw_lnx_g'][None], p['rw_lnx_b'][None],
             _pair_sum_matrix(), _cumsum_matrices(), _pair_masks()]
    out_specs = [pl.BlockSpec((rows, RW_WIDTH), lambda b: (b, 0))]
    out_shape = [jax.ShapeDtypeStruct((batch * seq_len, RW_WIDTH), F32)]
    if want_state:
        out_specs.append(state_spec)
        out_shape.append(jax.ShapeDtypeStruct((batch, 2, RW_HEADS, RW_HEAD_DIM, RW_HEAD_DIM), F32))
    per_chain = (2, 2, N_PAIRS)
    outs = pl.pallas_call(
        functools.partial(_rwkv_kernel, seq_len=seq_len, group=group, has_state_in=has_state_in,
                          has_state_out=want_state),
        grid=(batch // group,),
        in_specs=in_specs,
        out_specs=out_specs,
        out_shape=out_shape,
        scratch_shapes=[
            pltpu.VMEM((2, rows, RW_WIDTH), F32),
            pltpu.VMEM((2, rows, RW_WIDTH), F32),
            pltpu.VMEM((rows, RW_WIDTH), F32),
            pltpu.VMEM((2, N_PAIRS, PAIR, PAIR), F32),
            pltpu.VMEM((group, 2, N_PAIRS, PAIR, PAIR), F32),
            pltpu.VMEM(per_chain + (2 * PAIR, PAIR), BF16),
            pltpu.VMEM(per_chain + (2 * PAIR, PAIR), BF16),
            pltpu.VMEM(per_chain + (PAIR, PAIR), BF16),
            pltpu.VMEM((2, 2, 1, RW_WIDTH), F32),
            pltpu.VMEM(per_chain + (PAIR, PAIR), BF16),
            pltpu.VMEM(per_chain + (PAIR, 2 * PAIR), BF16),
            pltpu.VMEM(per_chain + (PAIR, PAIR), F32),
        ],
        compiler_params=pltpu.CompilerParams(
            dimension_semantics=("parallel",), vmem_limit_bytes=VMEM_LIMIT),
        name="rwkv",
    )(*args)
    return outs if want_state else (outs[0], None)


def _merge_kernel(x_ref, mod_ref, oa_ref, yg_ref, win_ref, woa_ref, wor_ref, wout_ref,
                  g_ref, b_ref, o_ref):
    mod = mod_ref[...]
    sh1 = mod[:, 0:D_MODEL]
    sc1 = mod[:, D_MODEL:2 * D_MODEL]
    g1 = mod[:, 2 * D_MODEL:3 * D_MODEL]
    x = x_ref[...]
    h = (x * (1.0 + sc1) + sh1).astype(BF16)
    gates = jax.nn.sigmoid(_dot(h, win_ref[:, QKVR_COLS:N_IN]))
    att = _dot(oa_ref[...].astype(BF16), woa_ref[...])
    rwk = _dot(yg_ref[...].astype(BF16), wor_ref[...])
    merged = gates[:, 0:D_MODEL] * att + gates[:, D_MODEL:2 * D_MODEL] * rwk
    mix = _dot(merged.astype(BF16), wout_ref[...])
    o_ref[...] = _layer_norm(ALPHA * x + g1 * mix, g_ref[...], b_ref[...], LN_EPS)


def _merge(x2d, mod3, o_att, yg, w_in, p, seq_len, fixed_row):
    m = x2d.shape[0]
    row = lambda i: (i, 0)
    const = lambda i: (0, 0)
    return pl.pallas_call(
        _merge_kernel,
        grid=(m // ROW_TILE,),
        in_specs=[
            pl.BlockSpec((ROW_TILE, D_MODEL), row),
            pl.BlockSpec((None, 1, 6 * D_MODEL), _mod_row_map(seq_len, fixed_row)),
            pl.BlockSpec((ROW_TILE, DA_WIDTH), row),
            pl.BlockSpec((ROW_TILE, RW_WIDTH), row),
            pl.BlockSpec((D_MODEL, N_IN), const, pipeline_mode=pl.Buffered(1)),
            pl.BlockSpec((DA_WIDTH, D_MODEL), const),
            pl.BlockSpec((RW_WIDTH, D_MODEL), const),
            pl.BlockSpec((D_MODEL, D_MODEL), const),
            pl.BlockSpec((1, D_MODEL), const),
            pl.BlockSpec((1, D_MODEL), const),
        ],
        out_specs=pl.BlockSpec((ROW_TILE, D_MODEL), row),
        out_shape=jax.ShapeDtypeStruct((m, D_MODEL), F32),
        compiler_params=pltpu.CompilerParams(
            dimension_semantics=("parallel",), vmem_limit_bytes=VMEM_LIMIT),
        name="merge",
    )(x2d, mod3, o_att, yg, w_in, p['w_o_attn'].astype(BF16), p['w_o_rwkv'].astype(BF16),
      p['w_out'].astype(BF16), p['ln1_g'][None], p['ln1_b'][None])


def _mlp_kernel(x_ref, mod_ref, wup_ref, cw_ref, cb_ref, wd_ref, g_ref, b_ref, o_ref,
                h_scr, act_scr, *, seq_len):
    mod = mod_ref[...]
    sh2 = mod[:, 3 * D_MODEL:4 * D_MODEL]
    sc2 = mod[:, 4 * D_MODEL:5 * D_MODEL]
    g2 = mod[:, 5 * D_MODEL:6 * D_MODEL]
    h_scr[...] = (x_ref[...] * (1.0 + sc2) + sh2).astype(BF16)
    rows = x_ref.shape[0]
    pos = lax.broadcasted_iota(jnp.int32, (rows, 1), 0) & (seq_len - 1)
    first = pos == 0
    last = pos == seq_len - 1
    for j in range(D_FF // FF_TILE):
        cols = slice(j * FF_TILE, (j + 1) * FF_TILE)
        h = h_scr[...]
        u = _dot(h, wup_ref[:, cols])
        val = _dot(h, wup_ref[:, D_FF + j * FF_TILE:D_FF + (j + 1) * FF_TILE])
        prev = jnp.where(first, 0.0, pltpu.roll(u, 1, 0))
        nxt = jnp.where(last, 0.0, pltpu.roll(u, rows - 1, 0))
        cw = cw_ref[:, cols]
        u = prev * cw[0:1, :] + u * cw[1:2, :] + nxt * cw[2:3, :] + cb_ref[:, cols]
        act_scr[:, cols] = (jax.nn.gelu(u) * val).astype(BF16)
    f = _dot(act_scr[...], wd_ref[...])
    o_ref[...] = _layer_norm(ALPHA * x_ref[...] + g2 * f, g_ref[...], b_ref[...], LN_EPS)


def _mlp(x2d, mod3, p, seq_len, fixed_row):
    m = x2d.shape[0]
    assert seq_len & (seq_len - 1) == 0 and MLP_ROW_TILE % seq_len == 0
    if fixed_row is not None:
        mod_map = lambda i: (fixed_row, 0, 0)
    else:
        mod_map = lambda i: (i * MLP_ROW_TILE // seq_len, 0, 0)
    row = lambda i: (i, 0)
    const = lambda i: (0, 0)
    resident = pl.Buffered(1)
    return pl.pallas_call(
        functools.partial(_mlp_kernel, seq_len=seq_len),
        grid=(m // MLP_ROW_TILE,),
        in_specs=[
            pl.BlockSpec((MLP_ROW_TILE, D_MODEL), row),
            pl.BlockSpec((None, 1, 6 * D_MODEL), mod_map),
            pl.BlockSpec((D_MODEL, 2 * D_FF), const, pipeline_mode=resident),
            pl.BlockSpec((3, D_FF), const),
            pl.BlockSpec((1, D_FF), const),
            pl.BlockSpec((D_FF, D_MODEL), const, pipeline_mode=resident),
            pl.BlockSpec((1, D_MODEL), const),
            pl.BlockSpec((1, D_MODEL), const),
        ],
        out_specs=pl.BlockSpec((MLP_ROW_TILE, D_MODEL), row),
        out_shape=jax.ShapeDtypeStruct((m, D_MODEL), F32),
        scratch_shapes=[
            pltpu.VMEM((MLP_ROW_TILE, D_MODEL), BF16),
            pltpu.VMEM((MLP_ROW_TILE, D_FF), BF16),
        ],
        compiler_params=pltpu.CompilerParams(
            dimension_semantics=("parallel",), vmem_limit_bytes=MLP_VMEM_LIMIT),
        name="mlp",
    )(x2d, mod3, p['w_up'].astype(BF16), p['conv_w'], p['conv_b'][None], p['w_down'].astype(BF16),
      p['ln2_g'][None], p['ln2_b'][None])


def _trunk_layer(x, mod3, fixed_row, p, w_in, layer, ctx=None):
    batch, seq_len, _ = x.shape
    x2d = x.reshape(batch * seq_len, D_MODEL)
    q, k, v, rw = _input_projection(x2d, mod3, w_in, p['rw_mu'], seq_len, fixed_row)
    if ctx is None:
        o_att = _attention(q, k, v, p['da_lambda'], p['da_subln_g'][None], batch, seq_len, layer)
        yg, state = _rwkv(rw, p, batch, seq_len, want_state=True)
    else:
        k_ctx, v_ctx, s_ctx = ctx
        past = k_ctx.shape[1]
        o_att = _attention(q, k, v, p['da_lambda'], p['da_subln_g'][None], batch, seq_len, layer,
                           ctx=(k_ctx.reshape(batch, past, DA_WIDTH), v_ctx.reshape(batch, past, DA_WIDTH)))
        yg, state = _rwkv(rw, p, batch, seq_len, state_in=s_ctx)
    x1 = _merge(x2d, mod3, o_att, yg, w_in, p, seq_len, fixed_row)
    y = _mlp(x1, mod3, p, seq_len, fixed_row)
    new_ctx = None
    if ctx is None:
        new_ctx = (k.reshape(batch, seq_len, DA_HEADS, 2, DA_HEAD_DIM),
                   v.reshape(batch, seq_len, DA_HEADS, 2 * DA_HEAD_DIM), state)
    return y.reshape(batch, seq_len, D_MODEL), new_ctx


def kernel(x_prompt, x_sample, cache_k, cache_v, state_rwkv, c, c_ctx, w_ada, b_ada, w_in, rw_mu, rw_w0, rw_w_up, rw_a0, rw_a_up, rw_g_up, rw_k_k, rw_k_a, rw_r_k, rw_lnx_g, rw_lnx_b, da_lambda, da_subln_g, w_o_attn, w_o_rwkv, w_out, ln1_g, ln1_b, w_up, conv_w, conv_b, w_down, ln2_g, ln2_b):
    dec_batch = x_sample.shape[0]
    assert dec_batch < MOD_ROWS
    y_prompt, y_sample = x_prompt, x_sample
    new_k, new_v, new_s = [], [], []
    for l in range(DEPTH):
        p = {
            'rw_mu': rw_mu[l], 'rw_w0': rw_w0[l], 'rw_w_up': rw_w_up[l], 'rw_a0': rw_a0[l],
            'rw_a_up': rw_a_up[l], 'rw_g_up': rw_g_up[l], 'rw_k_k': rw_k_k[l], 'rw_k_a': rw_k_a[l],
            'rw_r_k': rw_r_k[l], 'rw_lnx_g': rw_lnx_g[l], 'rw_lnx_b': rw_lnx_b[l],
            'da_lambda': da_lambda[l], 'da_subln_g': da_subln_g[l], 'w_o_attn': w_o_attn[l],
            'w_o_rwkv': w_o_rwkv[l], 'w_out': w_out[l], 'ln1_g': ln1_g[l], 'ln1_b': ln1_b[l],
            'w_up': w_up[l], 'conv_w': conv_w[l], 'conv_b': conv_b[l], 'w_down': w_down[l],
            'ln2_g': ln2_g[l], 'ln2_b': ln2_b[l],
        }
        cvec = jnp.concatenate(
            [c, c_ctx[None], jnp.zeros((MOD_ROWS - dec_batch - 1, D_MODEL), F32)], axis=0)
        mod3 = _modulation(cvec, w_ada[l], b_ada[l][None]).reshape(MOD_ROWS, 1, 6 * D_MODEL)
        w_in_l = w_in[l].astype(BF16)
        y_prompt, ctx_l = _trunk_layer(y_prompt, mod3, dec_batch, p, w_in_l, l)
        new_k.append(ctx_l[0])
        new_v.append(ctx_l[1])
        new_s.append(ctx_l[2])
        y_sample, _ = _trunk_layer(y_sample, mod3, None, p, w_in_l, l,
                                   ctx=(cache_k[:, l], cache_v[:, l], state_rwkv[:, l]))
    return (y_prompt, y_sample, jnp.stack(new_k, axis=1), jnp.stack(new_v, axis=1),
            jnp.stack(new_s, axis=1))
```

```python
import functools
import math

import jax
import jax.numpy as jnp
from jax import lax
from jax.experimental import pallas as pl
from jax.experimental.pallas import tpu as pltpu

F32 = jnp.float32
BF16 = jnp.bfloat16
HIGHEST = lax.Precision.HIGHEST

D_MODEL = 1024
GRID_W = 64
DA_HEADS = 4
DA_HEAD_DIM = 64
DA_WIDTH = DA_HEADS * 2 * DA_HEAD_DIM
ROPE_PAIRS_PER_AXIS = DA_HEAD_DIM // 4
ROPE_BASE = 10000.0
RW_HEADS = 8
RW_HEAD_DIM = 64
RW_WIDTH = RW_HEADS * RW_HEAD_DIM
DECAY_LORA = 64
AAA_LORA = 64
GATE_LORA = 128
RW_COLS = 3 * RW_WIDTH + DECAY_LORA + AAA_LORA + GATE_LORA
RW_LNX_EPS = 64e-5
QKVR_COLS = 3 * DA_WIDTH + RW_COLS
N_IN = QKVR_COLS + 2 * D_MODEL
D_FF = 2816
LN_EPS = 1e-5
DEPTH = 1
ALPHA = (2.0 * DEPTH) ** 0.25
LOG2_E = math.log2(math.e)

CHUNK = 64
ATTN_Q_BLOCK = 256
ATTN_MAX_ROWS = 4096
ROW_TILE = 512
SHIFT_COL_TILE = 256
MLP_ROW_TILE = 1024
FF_TILE = 256
MOD_COL_TILE = 768
MOD_ROWS = 16
VMEM_LIMIT = 48 * 1024 * 1024
MLP_VMEM_LIMIT = 56 * 1024 * 1024

_NN = (((1,), (0,)), ((), ()))
_NT = (((1,), (1,)), ((), ()))
_TN = (((0,), (0,)), ((), ()))


def _dot(a, b, dims=_NN, precision=None):
    return lax.dot_general(a, b, dims, precision=precision, preferred_element_type=F32)


def _layer_norm(z, g, b, eps):
    mu = jnp.mean(z, axis=-1, keepdims=True)
    zc = z - mu
    var = jnp.mean(zc * zc, axis=-1, keepdims=True)
    return zc * lax.rsqrt(var + eps) * g + b


def _mod_kernel(c_ref, w_ref, b_ref, o_ref):
    cv = c_ref[...]
    s = cv * jax.nn.sigmoid(cv)
    o_ref[...] = _dot(s, w_ref[...], precision=HIGHEST) + b_ref[...]


def _modulation(cvec, w_ada, b_ada):
    n = w_ada.shape[1]
    return pl.pallas_call(
        _mod_kernel,
        grid=(n // MOD_COL_TILE,),
        in_specs=[
            pl.BlockSpec((MOD_ROWS, D_MODEL), lambda j: (0, 0)),
            pl.BlockSpec((D_MODEL, MOD_COL_TILE), lambda j: (0, j)),
            pl.BlockSpec((1, MOD_COL_TILE), lambda j: (0, j)),
        ],
        out_specs=pl.BlockSpec((MOD_ROWS, MOD_COL_TILE), lambda j: (0, j)),
        out_shape=jax.ShapeDtypeStruct((MOD_ROWS, n), F32),
        compiler_params=pltpu.CompilerParams(
            dimension_semantics=("parallel",), vmem_limit_bytes=VMEM_LIMIT),
        name="mod",
    )(cvec, w_ada, b_ada)


def _mod_row_map(rows_per_batch, fixed_row):
    if fixed_row is not None:
        return lambda i: (fixed_row, 0, 0)
    tiles = rows_per_batch // ROW_TILE
    return lambda i: (i // tiles, 0, 0)


def _inproj_kernel(x_ref, mod_ref, w_ref, mu_ref, q_ref, k_ref, v_ref, rw_ref, *, seq_len):
    mod = mod_ref[...]
    sh1 = mod[:, 0:D_MODEL]
    sc1 = mod[:, D_MODEL:2 * D_MODEL]
    h = (x_ref[...] * (1.0 + sc1) + sh1).astype(BF16)
    q_ref[...] = _dot(h, w_ref[:, 0:DA_WIDTH])
    k_ref[...] = _dot(h, w_ref[:, DA_WIDTH:2 * DA_WIDTH])
    v_ref[...] = _dot(h, w_ref[:, 2 * DA_WIDTH:3 * DA_WIDTH])
    rows = x_ref.shape[0]
    pos = lax.broadcasted_iota(jnp.int32, (rows, 1), 0) & (seq_len - 1)
    first = pos == 0
    last = pos == seq_len - 1
    for j in range(RW_COLS // SHIFT_COL_TILE):
        cols = slice(j * SHIFT_COL_TILE, (j + 1) * SHIFT_COL_TILE)
        rw = _dot(h, w_ref[:, 3 * DA_WIDTH + j * SHIFT_COL_TILE:3 * DA_WIDTH + (j + 1) * SHIFT_COL_TILE])
        prev = jnp.where(first, 0.0, pltpu.roll(rw, 1, 0))
        nxt = jnp.where(last, 0.0, pltpu.roll(rw, rows - 1, 0))
        rw_ref[:, cols] = rw + mu_ref[0:1, cols] * (prev - rw) + mu_ref[1:2, cols] * (nxt - rw)


def _input_projection(x2d, mod3, w_in, rw_mu, seq_len, fixed_row):
    m = x2d.shape[0]
    tile = max(seq_len, ROW_TILE)
    assert seq_len & (seq_len - 1) == 0 and tile % seq_len == 0
    if fixed_row is not None:
        mod_map = lambda i: (fixed_row, 0, 0)
    else:
        mod_map = lambda i: (i * tile // seq_len, 0, 0)
    row = lambda i: (i, 0)
    return pl.pallas_call(
        functools.partial(_inproj_kernel, seq_len=seq_len),
        grid=(m // tile,),
        in_specs=[
            pl.BlockSpec((tile, D_MODEL), row),
            pl.BlockSpec((None, 1, 6 * D_MODEL), mod_map),
            pl.BlockSpec((D_MODEL, N_IN), lambda i: (0, 0), pipeline_mode=pl.Buffered(1)),
            pl.BlockSpec((2, RW_COLS), lambda i: (0, 0)),
        ],
        out_specs=[
            pl.BlockSpec((tile, DA_WIDTH), row),
            pl.BlockSpec((tile, DA_WIDTH), row),
            pl.BlockSpec((tile, DA_WIDTH), row),
            pl.BlockSpec((tile, RW_COLS), row),
        ],
        out_shape=[
            jax.ShapeDtypeStruct((m, DA_WIDTH), F32),
            jax.ShapeDtypeStruct((m, DA_WIDTH), F32),
            jax.ShapeDtypeStruct((m, DA_WIDTH), F32),
            jax.ShapeDtypeStruct((m, RW_COLS), F32),
        ],
        compiler_params=pltpu.CompilerParams(
            dimension_semantics=("parallel",), vmem_limit_bytes=VMEM_LIMIT),
        name="inproj",
    )(x2d, mod3, w_in, rw_mu)


def _rope(x, cos, sin_signed):
    lane = lax.broadcasted_iota(jnp.int32, x.shape, 1)
    partner = jnp.where((lane & 63) < 32, pltpu.roll(x, 96, 1), pltpu.roll(x, 32, 1))
    return x * cos + partner * sin_signed


def _attn_kernel(*refs, has_ctx, seq_len, heads, lam_init):
    if has_ctx:
        q_ref, k_ref, v_ref, kc_ref, vc_ref, cos_ref, sin_ref, lq_ref, g_ref, o_ref = refs
    else:
        q_ref, k_ref, v_ref, lq_ref, g_ref, o_ref = refs
    d = DA_HEAD_DIM
    qrows = min(ATTN_Q_BLOCK, seq_len)
    lq = lq_ref[...]
    lam = (jnp.exp(jnp.sum(lq[0:1] * lq[1:2], axis=-1, keepdims=True))
           - jnp.exp(jnp.sum(lq[2:3] * lq[3:4], axis=-1, keepdims=True)) + lam_init)

    def with_ones(v):
        return jnp.concatenate([v.astype(BF16), jnp.ones(v.shape, BF16)], axis=1)

    keys, vals = {}, {}

    def head_operands(h):
        if h not in keys:
            lanes = slice(h * 2 * d, (h + 1) * 2 * d)
            k = k_ref[:, lanes]
            if has_ctx:
                k = _rope(k, cos_ref[...], sin_ref[...])
            keys[h] = [[k[:, m * d:(m + 1) * d].astype(BF16)] for m in range(2)]
            vals[h] = [with_ones(v_ref[:, lanes])]
            if has_ctx:
                kc = kc_ref[:, lanes]
                for m in range(2):
                    keys[h][m].append(kc[:, m * d:(m + 1) * d].astype(BF16))
                vals[h].append(with_ones(vc_ref[:, lanes]))
        return keys[h], vals[h]

    g = g_ref[...]

    def scores(h, qb, m):
        rows = slice(qb * qrows, (qb + 1) * qrows)
        q = q_ref[rows, h * 2 * d:(h + 1) * 2 * d]
        if has_ctx:
            q = _rope(q, cos_ref[rows, :], sin_ref[rows, :])
        qm = (q[:, m * d:(m + 1) * d] * (d ** -0.5 * LOG2_E)).astype(BF16)
        return [_dot(qm, kg, _NT) for kg in head_operands(h)[0][m]]

    def attend(h, ss):
        mx = ss[0].max(axis=-1, keepdims=True)
        for s in ss[1:]:
            mx = jnp.maximum(mx, s.max(axis=-1, keepdims=True))
        acc = None
        for s, vg in zip(ss, head_operands(h)[1]):
            o = _dot(jnp.exp2(s - mx).astype(BF16), vg)
            acc = o if acc is None else acc + o
        return acc[:, 0:2 * d] / acc[:, 2 * d:4 * d]

    units = [(h, qb, m) for h in range(heads) for qb in range(seq_len // qrows)
             for m in range(2)]
    pending = scores(*units[0])
    o1 = None
    for i, (h, qb, m) in enumerate(units):
        following = scores(*units[i + 1]) if i + 1 < len(units) else None
        o = attend(h, pending)
        pending = following
        if m == 0:
            o1 = o
            continue
        o = o1 - lam * o
        ms = jnp.mean(o * o, axis=-1, keepdims=True)
        rows = slice(qb * qrows, (qb + 1) * qrows)
        o_ref[rows, h * 2 * d:(h + 1) * 2 * d] = o * lax.rsqrt(ms + LN_EPS) * g * (1.0 - lam_init)


def _rope_tables(n):
    rows = n // GRID_W
    row = jnp.repeat(jnp.arange(rows, dtype=F32), GRID_W)
    col = jnp.tile(jnp.arange(GRID_W, dtype=F32), rows)
    inv = ROPE_BASE ** (-jnp.arange(ROPE_PAIRS_PER_AXIS, dtype=F32) / ROPE_PAIRS_PER_AXIS)
    ang = jnp.concatenate([row[:, None] * inv, col[:, None] * inv], -1)
    cos, sin = jnp.cos(ang), jnp.sin(ang)
    return jnp.tile(cos, (1, 4)), jnp.tile(jnp.concatenate([-sin, sin], -1), (1, 2))


def _attention(q2d, k2d, v2d, da_lambda, subln_g, batch, seq_len, layer, ctx=None):
    has_ctx = ctx is not None
    w = 2 * DA_HEAD_DIM
    heads = DA_HEADS if seq_len * DA_HEADS <= ATTN_MAX_ROWS else 1
    head = lambda b, h: (b, h)
    const = lambda b, h: (0, 0)
    in_specs = [pl.BlockSpec((seq_len, heads * w), head)] * 3
    args = [q2d, k2d, v2d]
    if has_ctx:
        kc, vc = ctx
        past = kc.shape[1]
        in_specs += [pl.BlockSpec((None, past, heads * w), lambda b, h: (b, 0, h))] * 2
        in_specs += [pl.BlockSpec((seq_len, w), const)] * 2
        args += [kc, vc, *_rope_tables(seq_len)]
    in_specs += [pl.BlockSpec((4, DA_HEAD_DIM), const), pl.BlockSpec((1, w), const)]
    args += [da_lambda, subln_g]
    lam_init = 0.8 - 0.6 * math.exp(-0.3 * layer)
    return pl.pallas_call(
        functools.partial(_attn_kernel, has_ctx=has_ctx, seq_len=seq_len, heads=heads,
                          lam_init=lam_init),
        grid=(batch, DA_HEADS // heads),
        in_specs=in_specs,
        out_specs=pl.BlockSpec((seq_len, heads * w), head),
        out_shape=jax.ShapeDtypeStruct((batch * seq_len, DA_WIDTH), F32),
        compiler_params=pltpu.CompilerParams(
            dimension_semantics=("parallel", "parallel"), vmem_limit_bytes=VMEM_LIMIT),
        name="attn",
    )(*args)


PAIR = 2 * RW_HEAD_DIM
N_PAIRS = RW_HEADS // 2
FINISH_ROWS = 256
RWKV_GROUP_ROWS = 1024

_M_STRICT = (0, 2)
_M_INCL = (1, 3)
_M_BLOCK8, _M_OFF16, _M_OFF32, _M_OFF64, _M_EYE = 4, 5, 6, 7, 8


def _pair_masks():
    t = jnp.arange(PAIR)[:, None]
    s = jnp.arange(PAIR)[None, :]
    same = lambda n: (t // n) == (s // n)
    head = same(CHUNK)
    masks = [head & (t > s), head & (t >= s), head & (t < s), head & (t <= s),
             same(8), same(16) & ~same(8), same(32) & ~same(16), head & ~same(32), t == s]
    return jnp.stack(masks).astype(BF16)


def _cumsum_matrices():
    t = jnp.arange(CHUNK)[:, None]
    s = jnp.arange(CHUNK)[None, :]
    return jnp.stack([t >= s, t <= s]).astype(BF16)


def _pair_sum_matrix():
    i = jnp.arange(PAIR)
    return ((i[:, None] // RW_HEAD_DIM) == (i[None, :] // RW_HEAD_DIM)).astype(BF16)


def _split2(x):
    hi = x.astype(BF16)
    lo = (x - hi.astype(F32)).astype(BF16)
    return hi, lo


def _head_sums(x, e_ref):
    rows = x.shape[0]
    xs = jnp.concatenate([x[:, p * PAIR:(p + 1) * PAIR] for p in range(N_PAIRS)], axis=0)
    s = _dot(xs.astype(BF16), e_ref[...])
    return jnp.concatenate([s[p * rows:(p + 1) * rows] for p in range(N_PAIRS)], axis=1)


def _run_schedule(order, **stages):
    for name in order:
        next(stages[name], None)
    for gen in stages.values():
        for _ in gen:
            pass


_STEP_ORDER = "B B B B C B A B C B A B C B A B".split()


def _rwkv_kernel(*refs, seq_len, group, has_state_in, has_state_out):
    refs = list(refs)
    rw_ref = refs.pop(0)
    s0_ref = refs.pop(0) if has_state_in else None
    (w0_ref, wup_ref, a0_ref, aup_ref, gup_ref, kk_ref, ka_ref, rk_ref,
     lng_ref, lnb_ref, e_ref, tri_ref, m_ref) = refs[:13]
    refs = refs[13:]
    o_ref = refs.pop(0)
    so_ref = refs.pop(0) if has_state_out else None
    (y_scr, bv_scr, gr_scr, s_scr, sio_scr, ar_scr, bk_scr, v_scr, ge_scr,
     t_scr, wy_scr, rl_scr) = refs

    C = CHUNK
    N = RW_HEAD_DIM
    W = RW_WIDTH
    P = PAIR
    nc = seq_len // C
    n_steps = group * nc
    total_rows = group * seq_len
    assert nc & (nc - 1) == 0 and n_steps % 2 == 0
    mm = lambda x, y: _dot(x.astype(BF16), y.astype(BF16))
    mmb = lambda x, y: _dot(x, y).astype(BF16)
    zeros_nn = jnp.zeros((N, N), F32)
    chains = [(d, p) for d in range(2) for p in range(N_PAIRS)]
    n = range(len(chains))

    if has_state_in:
        for g in range(group):
            for d, p in chains:
                top = jnp.concatenate([s0_ref[g, d, 2 * p], zeros_nn], axis=1)
                bot = jnp.concatenate([zeros_nn, s0_ref[g, d, 2 * p + 1]], axis=1)
                sio_scr[g, d, p] = jnp.concatenate([top, bot], axis=0)
    for d, p in chains:
        s_scr[d, p] = jnp.zeros((P, P), F32)

    for scr in (ar_scr, bk_scr, v_scr):
        scr[...] = jnp.zeros(scr.shape, scr.dtype)

    def put_block_diag(scr, slot, d, row0, x):
        xb = x.astype(BF16)
        for p in range(N_PAIRS):
            for hh in range(2):
                lanes = slice(hh * N, (hh + 1) * N)
                scr[slot, d, p, row0 + hh * C:row0 + (hh + 1) * C, lanes] = (
                    xb[:, p * P + hh * N:p * P + (hh + 1) * N])

    def step_chunks(i):
        i = jnp.minimum(i, n_steps - 1)
        g = i // nc
        il = i % nc
        return g, il, (il, nc - 1 - il)

    def prep_stages(i, slot):
        g, _, cs = step_chunks(i)
        st = []
        for d, c in enumerate(cs):
            r0 = pl.multiple_of(g * seq_len + c * C, C)
            rows = pl.ds(r0, C)
            xm = rw_ref[rows, :]
            w_lo = xm[:, 3 * W:3 * W + DECAY_LORA]
            a_lo = xm[:, 3 * W + DECAY_LORA:3 * W + DECAY_LORA + AAA_LORA]
            st.append(dict(rows=rows, r=xm[:, 0:W], kr=xm[:, W:2 * W], vr=xm[:, 2 * W:3 * W],
                           g_lo=xm[:, 3 * W + DECAY_LORA + AAA_LORA:RW_COLS],
                           w_up=_dot(jnp.tanh(w_lo).astype(BF16), wup_ref[d]),
                           a_up=_dot(a_lo.astype(BF16), aup_ref[d])))
        yield
        for d, x in enumerate(st):
            x['log_decay'] = -math.exp(-0.5) * jax.nn.sigmoid(w0_ref[d:d + 1, :] + x['w_up'])
            x['a'] = jax.nn.sigmoid(a0_ref[d:d + 1, :] + x['a_up'])
            x['keff'] = x['kr'] * (1.0 + (x['a'] - 1.0) * ka_ref[...])
            x['kk'] = x['kr'] * kk_ref[...]
            x['sums'] = _head_sums(
                jnp.concatenate([x['kk'] * x['kk'], x['r'] * x['keff'] * rk_ref[...]], axis=0), e_ref)
            if d == 0:
                gr_scr[x['rows'], :] = _dot(jax.nn.sigmoid(x['g_lo']).astype(BF16), gup_ref[...])
            cum = _dot(tri_ref[d], jnp.concatenate(_split2(x['log_decay']), axis=1))
            x['cum'] = cum[:, 0:W] + cum[:, W:2 * W]
        yield
        for d, x in enumerate(st):
            kk = x['kk'] / jnp.maximum(jnp.sqrt(x['sums'][0:C]), 1e-12)
            bv_scr[d, x['rows'], :] = x['sums'][C:2 * C] * x['vr']
            cum = x['cum']
            g_in = jnp.exp(cum)
            g_inv = jnp.exp(-cum)
            a_t = -kk * jnp.exp(cum - x['log_decay'])
            r_t = x['r'] * g_in
            b_t = kk * x['a'] * g_inv
            k_t = x['keff'] * g_inv
            ge_scr[slot, d] = g_in[C - 1:C, :] if d == 0 else g_in[0:1, :]
            put_block_diag(ar_scr, slot, d, 0, a_t)
            put_block_diag(ar_scr, slot, d, P, r_t)
            put_block_diag(bk_scr, slot, d, 0, b_t)
            put_block_diag(bk_scr, slot, d, P, k_t)
            put_block_diag(v_scr, slot, d, 0, x['vr'])
        yield

    def local_stages(slot):
        ar = [ar_scr[slot, d, p] for d, p in chains]
        bk = [bk_scr[slot, d, p] for d, p in chains]
        vb = [v_scr[slot, d, p] for d, p in chains]
        strict = [m_ref[_M_STRICT[d]] for d, _ in chains]
        incl = [m_ref[_M_INCL[d]] for d, _ in chains]
        x = [_dot(ar[i], bk[i], _NT).astype(BF16) for i in n]
        yield
        amat = [x[i][0:P, 0:P] * strict[i] for i in n]
        a8 = [a * m_ref[_M_BLOCK8] for a in amat]
        a2 = [mmb(a, a) for a in a8]
        for i, (d, p) in enumerate(chains):
            rl_scr[slot, d, p] = _dot(x[i][0:P, P:2 * P] * strict[i], vb[i])
            wy_scr[slot, d, p] = jnp.concatenate(
                [x[i][P:2 * P, 0:P] * incl[i], x[i][P:2 * P, P:2 * P] * incl[i]], axis=1)
        yield
        a4 = [mmb(a, a) for a in a2]
        ps = [m_ref[_M_EYE] + a for a in a8]
        ps = [t + mmb(t, a) for t, a in zip(ps, a2)]
        yield
        ps = [t + mmb(t, a) for t, a in zip(ps, a4)]
        yield
        for off in (_M_OFF16, _M_OFF32, _M_OFF64):
            ts = [mmb(t, a * m_ref[off]) for t, a in zip(ps, amat)]
            yield
            ps = [t + mmb(q, t) for t, q in zip(ps, ts)]
            if off == _M_OFF64:
                for i, (d, p) in enumerate(chains):
                    t_scr[slot, d, p] = ps[i]
            yield

    def state_stages(i, slot):
        g, il, cs = step_chunks(i)
        ar = [ar_scr[slot, d, p] for d, p in chains]
        bk = [bk_scr[slot, d, p] for d, p in chains]
        vb = [v_scr[slot, d, p] for d, p in chains]
        g_end = [ge_scr[slot, d] for d in range(2)]
        if has_state_in:
            s_old = [jnp.where(il == 0, sio_scr[g, d, p], s_scr[d, p]) for d, p in chains]
        else:
            s_old = [jnp.where(il == 0, 0.0, s_scr[d, p]) for d, p in chains]
        xs = [_dot(ar[i], s_old[i].astype(BF16), _NT) for i in n]
        yield
        u = [mm(t_scr[slot, d, p], xs[i][0:P] + rl_scr[slot, d, p]).astype(BF16)
             for i, (d, p) in enumerate(chains)]
        yield
        uv = [jnp.concatenate([u[i], vb[i]], axis=0) for i in n]
        y = [xs[i][P:2 * P] + _dot(wy_scr[slot, d, p], uv[i]) for i, (d, p) in enumerate(chains)]
        s_new = [s_old[i] + _dot(uv[i], bk[i], _TN) for i in n]
        for i, (d, p) in enumerate(chains):
            sl = slice(p * P, (p + 1) * P)
            rows = pl.ds(pl.multiple_of(g * seq_len + cs[d] * C, C), C)
            y_scr[d, rows, sl] = y[i][0:C] + y[i][C:2 * C]
            s_end = s_new[i] * g_end[d][:, sl]
            s_scr[d, p] = s_end
            if has_state_out:
                sio_scr[g, d, p] = s_end
        yield

    for _ in prep_stages(0, 0):
        pass
    _run_schedule("B A B A B A".split(), A=prep_stages(1, 1), B=local_stages(0))

    def scan_body(j, carry):
        i = 2 * j
        _run_schedule(_STEP_ORDER, C=state_stages(i, 0), B=local_stages(1), A=prep_stages(i + 2, 0))
        _run_schedule(_STEP_ORDER, C=state_stages(i + 1, 1), B=local_stages(0), A=prep_stages(i + 3, 1))
        return carry

    lax.fori_loop(0, n_steps // 2, scan_body, 0)

    def finish_body(i, carry):
        rows = pl.ds(pl.multiple_of(i * FINISH_ROWS, FINISH_ROWS), FINISH_ROWS)
        y = y_scr[0, rows, :] + y_scr[1, rows, :]
        yc = y - _head_sums(y, e_ref) * (1.0 / N)
        var = _head_sums(yc * yc, e_ref) * (1.0 / N)
        yn = yc * lax.rsqrt(var + RW_LNX_EPS) * lng_ref[...] + lnb_ref[...]
        o_ref[rows, :] = (yn + bv_scr[0, rows, :] + bv_scr[1, rows, :]) * gr_scr[rows, :]
        return carry

    lax.fori_loop(0, total_rows // FINISH_ROWS, finish_body, 0)
    if has_state_out:
        for g in range(group):
            for d, p in chains:
                s_pair = sio_scr[g, d, p]
                so_ref[g, d, 2 * p] = s_pair[0:N, 0:N]
                so_ref[g, d, 2 * p + 1] = s_pair[N:P, N:P]


def _rwkv(rw2d, p, batch, seq_len, state_in=None, want_state=False):
    has_state_in = state_in is not None
    assert not (has_state_in and want_state)
    group = max(1, RWKV_GROUP_ROWS // seq_len)
    assert batch % group == 0
    rows = group * seq_len
    const2 = lambda b: (0, 0)
    const3 = lambda b: (0, 0, 0)
    state_spec = pl.BlockSpec((group, 2, RW_HEADS, RW_HEAD_DIM, RW_HEAD_DIM), lambda b: (b, 0, 0, 0, 0))
    in_specs = [pl.BlockSpec((rows, RW_COLS), lambda b: (b, 0))]
    args = [rw2d]
    if has_state_in:
        in_specs.append(state_spec)
        args.append(state_in)
    in_specs += [
        pl.BlockSpec((2, RW_WIDTH), const2),
        pl.BlockSpec((2, DECAY_LORA, RW_WIDTH), const3),
        pl.BlockSpec((2, RW_WIDTH), const2),
        pl.BlockSpec((2, AAA_LORA, RW_WIDTH), const3),
        pl.BlockSpec((GATE_LORA, RW_WIDTH), const2),
        pl.BlockSpec((1, RW_WIDTH), const2),
        pl.BlockSpec((1, RW_WIDTH), const2),
        pl.BlockSpec((1, RW_WIDTH), const2),
        pl.BlockSpec((1, RW_WIDTH), const2),
        pl.BlockSpec((1, RW_WIDTH), const2),
        pl.BlockSpec((PAIR, PAIR), const2),
        pl.BlockSpec((2, CHUNK, CHUNK), const3),
        pl.BlockSpec((9, PAIR, PAIR), const3),
    ]
    args += [p['rw_w0'], p['rw_w_up'].astype(BF16), p['rw_a0'], p['rw_a_up'].astype(BF16),
             p['rw_g_up'].astype(BF16), p['rw_k_k'][None], p['rw_k_a'][None],
             p['rw_r_k'].reshape(1, RW_WIDTH), p['rw_lnx_g'][None], p['rw_lnx_b'][None],
             _pair_sum_matrix(), _cumsum_matrices(), _pair_masks()]
    out_specs = [pl.BlockSpec((rows, RW_WIDTH), lambda b: (b, 0))]
    out_shape = [jax.ShapeDtypeStruct((batch * seq_len, RW_WIDTH), F32)]
    if want_state:
        out_specs.append(state_spec)
        out_shape.append(jax.ShapeDtypeStruct((batch, 2, RW_HEADS, RW_HEAD_DIM, RW_HEAD_DIM), F32))
    per_chain = (2, 2, N_PAIRS)
    outs = pl.pallas_call(
        functools.partial(_rwkv_kernel, seq_len=seq_len, group=group, has_state_in=has_state_in,
                          has_state_out=want_state),
        grid=(batch // group,),
        in_specs=in_specs,
        out_specs=out_specs,
        out_shape=out_shape,
        scratch_shapes=[
            pltpu.VMEM((2, rows, RW_WIDTH), F32),
            pltpu.VMEM((2, rows, RW_WIDTH), F32),
            pltpu.VMEM((rows, RW_WIDTH), F32),
            pltpu.VMEM((2, N_PAIRS, PAIR, PAIR), F32),
            pltpu.VMEM((group, 2, N_PAIRS, PAIR, PAIR), F32),
            pltpu.VMEM(per_chain + (2 * PAIR, PAIR), BF16),
            pltpu.VMEM(per_chain + (2 * PAIR, PAIR), BF16),
            pltpu.VMEM(per_chain + (PAIR, PAIR), BF16),
            pltpu.VMEM((2, 2, 1, RW_WIDTH), F32),
            pltpu.VMEM(per_chain + (PAIR, PAIR), BF16),
            pltpu.VMEM(per_chain + (PAIR, 2 * PAIR), BF16),
            pltpu.VMEM(per_chain + (PAIR, PAIR), F32),
        ],
        compiler_params=pltpu.CompilerParams(
            dimension_semantics=("parallel",), vmem_limit_bytes=VMEM_LIMIT),
        name="rwkv",
    )(*args)
    return outs if want_state else (outs[0], None)


def _merge_kernel(x_ref, mod_ref, oa_ref, yg_ref, win_ref, woa_ref, wor_ref, wout_ref,
                  g_ref, b_ref, o_ref):
    mod = mod_ref[...]
    sh1 = mod[:, 0:D_MODEL]
    sc1 = mod[:, D_MODEL:2 * D_MODEL]
    g1 = mod[:, 2 * D_MODEL:3 * D_MODEL]
    x = x_ref[...]
    h = (x * (1.0 + sc1) + sh1).astype(BF16)
    gates = jax.nn.sigmoid(_dot(h, win_ref[:, QKVR_COLS:N_IN]))
    att = _dot(oa_ref[...].astype(BF16), woa_ref[...])
    rwk = _dot(yg_ref[...].astype(BF16), wor_ref[...])
    merged = gates[:, 0:D_MODEL] * att + gates[:, D_MODEL:2 * D_MODEL] * rwk
    mix = _dot(merged.astype(BF16), wout_ref[...])
    o_ref[...] = _layer_norm(ALPHA * x + g1 * mix, g_ref[...], b_ref[...], LN_EPS)


def _merge(x2d, mod3, o_att, yg, w_in, p, seq_len, fixed_row):
    m = x2d.shape[0]
    row = lambda i: (i, 0)
    const = lambda i: (0, 0)
    return pl.pallas_call(
        _merge_kernel,
        grid=(m // ROW_TILE,),
        in_specs=[
            pl.BlockSpec((ROW_TILE, D_MODEL), row),
            pl.BlockSpec((None, 1, 6 * D_MODEL), _mod_row_map(seq_len, fixed_row)),
            pl.BlockSpec((ROW_TILE, DA_WIDTH), row),
            pl.BlockSpec((ROW_TILE, RW_WIDTH), row),
            pl.BlockSpec((D_MODEL, N_IN), const, pipeline_mode=pl.Buffered(1)),
            pl.BlockSpec((DA_WIDTH, D_MODEL), const),
            pl.BlockSpec((RW_WIDTH, D_MODEL), const),
            pl.BlockSpec((D_MODEL, D_MODEL), const),
            pl.BlockSpec((1, D_MODEL), const),
            pl.BlockSpec((1, D_MODEL), const),
        ],
        out_specs=pl.BlockSpec((ROW_TILE, D_MODEL), row),
        out_shape=jax.ShapeDtypeStruct((m, D_MODEL), F32),
        compiler_params=pltpu.CompilerParams(
            dimension_semantics=("parallel",), vmem_limit_bytes=VMEM_LIMIT),
        name="merge",
    )(x2d, mod3, o_att, yg, w_in, p['w_o_attn'].astype(BF16), p['w_o_rwkv'].astype(BF16),
      p['w_out'].astype(BF16), p['ln1_g'][None], p['ln1_b'][None])


def _mlp_kernel(x_ref, mod_ref, wup_ref, cw_ref, cb_ref, wd_ref, g_ref, b_ref, o_ref,
                h_scr, act_scr, *, seq_len):
    mod = mod_ref[...]
    sh2 = mod[:, 3 * D_MODEL:4 * D_MODEL]
    sc2 = mod[:, 4 * D_MODEL:5 * D_MODEL]
    g2 = mod[:, 5 * D_MODEL:6 * D_MODEL]
    h_scr[...] = (x_ref[...] * (1.0 + sc2) + sh2).astype(BF16)
    rows = x_ref.shape[0]
    pos = lax.broadcasted_iota(jnp.int32, (rows, 1), 0) & (seq_len - 1)
    first = pos == 0
    last = pos == seq_len - 1
    for j in range(D_FF // FF_TILE):
        cols = slice(j * FF_TILE, (j + 1) * FF_TILE)
        h = h_scr[...]
        u = _dot(h, wup_ref[:, cols])
        val = _dot(h, wup_ref[:, D_FF + j * FF_TILE:D_FF + (j + 1) * FF_TILE])
        prev = jnp.where(first, 0.0, pltpu.roll(u, 1, 0))
        nxt = jnp.where(last, 0.0, pltpu.roll(u, rows - 1, 0))
        cw = cw_ref[:, cols]
        u = prev * cw[0:1, :] + u * cw[1:2, :] + nxt * cw[2:3, :] + cb_ref[:, cols]
        act_scr[:, cols] = (jax.nn.gelu(u) * val).astype(BF16)
    f = _dot(act_scr[...], wd_ref[...])
    o_ref[...] = _layer_norm(ALPHA * x_ref[...] + g2 * f, g_ref[...], b_ref[...], LN_EPS)


def _mlp(x2d, mod3, p, seq_len, fixed_row):
    m = x2d.shape[0]
    assert seq_len & (seq_len - 1) == 0 and MLP_ROW_TILE % seq_len == 0
    if fixed_row is not None:
        mod_map = lambda i: (fixed_row, 0, 0)
    else:
        mod_map = lambda i: (i * MLP_ROW_TILE // seq_len, 0, 0)
    row = lambda i: (i, 0)
    const = lambda i: (0, 0)
    resident = pl.Buffered(1)
    return pl.pallas_call(
        functools.partial(_mlp_kernel, seq_len=seq_len),
        grid=(m // MLP_ROW_TILE,),
        in_specs=[
            pl.BlockSpec((MLP_ROW_TILE, D_MODEL), row),
            pl.BlockSpec((None, 1, 6 * D_MODEL), mod_map),
            pl.BlockSpec((D_MODEL, 2 * D_FF), const, pipeline_mode=resident),
            pl.BlockSpec((3, D_FF), const),
            pl.BlockSpec((1, D_FF), const),
            pl.BlockSpec((D_FF, D_MODEL), const, pipeline_mode=resident),
            pl.BlockSpec((1, D_MODEL), const),
            pl.BlockSpec((1, D_MODEL), const),
        ],
        out_specs=pl.BlockSpec((MLP_ROW_TILE, D_MODEL), row),
        out_shape=jax.ShapeDtypeStruct((m, D_MODEL), F32),
        scratch_shapes=[
            pltpu.VMEM((MLP_ROW_TILE, D_MODEL), BF16),
            pltpu.VMEM((MLP_ROW_TILE, D_FF), BF16),
        ],
        compiler_params=pltpu.CompilerParams(
            dimension_semantics=("parallel",), vmem_limit_bytes=MLP_VMEM_LIMIT),
        name="mlp",
    )(x2d, mod3, p['w_up'].astype(BF16), p['conv_w'], p['conv_b'][None], p['w_down'].astype(BF16),
      p['ln2_g'][None], p['ln2_b'][None])


def _trunk_layer(x, mod3, fixed_row, p, w_in, layer, ctx=None):
    batch, seq_len, _ = x.shape
    x2d = x.reshape(batch * seq_len, D_MODEL)
    q, k, v, rw = _input_projection(x2d, mod3, w_in, p['rw_mu'], seq_len, fixed_row)
    if ctx is None:
        o_att = _attention(q, k, v, p['da_lambda'], p['da_subln_g'][None], batch, seq_len, layer)
        yg, state = _rwkv(rw, p, batch, seq_len, want_state=True)
    else:
        k_ctx, v_ctx, s_ctx = ctx
        past = k_ctx.shape[1]
        o_att = _attention(q, k, v, p['da_lambda'], p['da_subln_g'][None], batch, seq_len, layer,
                           ctx=(k_ctx.reshape(batch, past, DA_WIDTH), v_ctx.reshape(batch, past, DA_WIDTH)))
        yg, state = _rwkv(rw, p, batch, seq_len, state_in=s_ctx)
    x1 = _merge(x2d, mod3, o_att, yg, w_in, p, seq_len, fixed_row)
    y = _mlp(x1, mod3, p, seq_len, fixed_row)
    new_ctx = None
    if ctx is None:
        new_ctx = (k.reshape(batch, seq_len, DA_HEADS, 2, DA_HEAD_DIM),
                   v.reshape(batch, seq_len, DA_HEADS, 2 * DA_HEAD_DIM), state)
    return y.reshape(batch, seq_len, D_MODEL), new_ctx


def kernel(x_prompt, x_sample, cache_k, cache_v, state_rwkv, c, c_ctx, w_ada, b_ada, w_in, rw_mu, rw_w0, rw_w_up, rw_a0, rw_a_up, rw_g_up, rw_k_k, rw_k_a, rw_r_k, rw_lnx_g, rw_lnx_b, da_lambda, da_subln_g, w_o_attn, w_o_rwkv, w_out, ln1_g, ln1_b, w_up, conv_w, conv_b, w_down, ln2_g, ln2_b):
    dec_batch = x_sample.shape[0]
    assert dec_batch < MOD_ROWS
    y_prompt, y_sample = x_prompt, x_sample
    new_k, new_v, new_s = [], [], []
    for l in range(DEPTH):
        p = {
            'rw_mu': rw_mu[l], 'rw_w0': rw_w0[l], 'rw_w_up': rw_w_up[l], 'rw_a0': rw_a0[l],
            'rw_a_up': rw_a_up[l], 'rw_g_up': rw_g_up[l], 'rw_k_k': rw_k_k[l], 'rw_k_a': rw_k_a[l],
            'rw_r_k': rw_r_k[l], 'rw_lnx_g': rw_lnx_g[l], 'rw_lnx_b': rw_lnx_b[l],
            'da_lambda': da_lambda[l], 'da_subln_g': da_subln_g[l], 'w_o_attn': w_o_attn[l],
            'w_o_rwkv': w_o_rwkv[l], 'w_out': w_out[l], 'ln1_g': ln1_g[l], 'ln1_b': ln1_b[l],
            'w_up': w_up[l], 'conv_w': conv_w[l], 'conv_b': conv_b[l], 'w_down': w_down[l],
            'ln2_g': ln2_g[l], 'ln2_b': ln2_b[l],
        }
        cvec = jnp.concatenate(
            [c, c_ctx[None], jnp.zeros((MOD_ROWS - dec_batch - 1, D_MODEL), F32)], axis=0)
        mod3 = _modulation(cvec, w_ada[l], b_ada[l][None]).reshape(MOD_ROWS, 1, 6 * D_MODEL)
        w_in_l = w_in[l].astype(BF16)
        y_prompt, ctx_l = _trunk_layer(y_prompt, mod3, dec_batch, p, w_in_l, l)
        new_k.append(ctx_l[0])
        new_v.append(ctx_l[1])
        new_s.append(ctx_l[2])
        y_sample, _ = _trunk_layer(y_sample, mod3, None, p, w_in_l, l,
                                   ctx=(cache_k[:, l], cache_v[:, l], state_rwkv[:, l]))
    return (y_prompt, y_sample, jnp.stack(new_k, axis=1), jnp.stack(new_v, axis=1),
            jnp.stack(new_s, axis=1))
```

```python
import functools
import math

import jax
import jax.numpy as jnp
from jax import lax
from jax.experimental import pallas as pl
from jax.experimental.pallas import tpu as pltpu

F32 = jnp.float32
BF16 = jnp.bfloat16
HIGHEST = lax.Precision.HIGHEST

D_MODEL = 1024
GRID_W = 64
DA_HEADS = 4
DA_HEAD_DIM = 64
DA_WIDTH = DA_HEADS * 2 * DA_HEAD_DIM
ROPE_PAIRS_PER_AXIS = DA_HEAD_DIM // 4
ROPE_BASE = 10000.0
RW_HEADS = 8
RW_HEAD_DIM = 64
RW_WIDTH = RW_HEADS * RW_HEAD_DIM
DECAY_LORA = 64
AAA_LORA = 64
GATE_LORA = 128
RW_COLS = 3 * RW_WIDTH + DECAY_LORA + AAA_LORA + GATE_LORA
RW_LNX_EPS = 64e-5
QKVR_COLS = 3 * DA_WIDTH + RW_COLS
N_IN = QKVR_COLS + 2 * D_MODEL
D_FF = 2816
LN_EPS = 1e-5
DEPTH = 1
ALPHA = (2.0 * DEPTH) ** 0.25
LOG2_E = math.log2(math.e)

CHUNK = 64
ATTN_Q_BLOCK = 256
ATTN_MAX_ROWS = 4096
ROW_TILE = 512
SHIFT_COL_TILE = 256
MLP_ROW_TILE = 1024
FF_TILE = 256
MOD_COL_TILE = 768
MOD_ROWS = 16
VMEM_LIMIT = 48 * 1024 * 1024
MLP_VMEM_LIMIT = 56 * 1024 * 1024

_NN = (((1,), (0,)), ((), ()))
_NT = (((1,), (1,)), ((), ()))
_TN = (((0,), (0,)), ((), ()))


def _dot(a, b, dims=_NN, precision=None):
    return lax.dot_general(a, b, dims, precision=precision, preferred_element_type=F32)


def _layer_norm(z, g, b, eps):
    mu = jnp.mean(z, axis=-1, keepdims=True)
    zc = z - mu
    var = jnp.mean(zc * zc, axis=-1, keepdims=True)
    return zc * lax.rsqrt(var + eps) * g + b


def _mod_kernel(c_ref, w_ref, b_ref, o_ref):
    cv = c_ref[...]
    s = cv * jax.nn.sigmoid(cv)
    o_ref[...] = _dot(s, w_ref[...], precision=HIGHEST) + b_ref[...]


def _modulation(cvec, w_ada, b_ada):
    n = w_ada.shape[1]
    return pl.pallas_call(
        _mod_kernel,
        grid=(n // MOD_COL_TILE,),
        in_specs=[
            pl.BlockSpec((MOD_ROWS, D_MODEL), lambda j: (0, 0)),
            pl.BlockSpec((D_MODEL, MOD_COL_TILE), lambda j: (0, j)),
            pl.BlockSpec((1, MOD_COL_TILE), lambda j: (0, j)),
        ],
        out_specs=pl.BlockSpec((MOD_ROWS, MOD_COL_TILE), lambda j: (0, j)),
        out_shape=jax.ShapeDtypeStruct((MOD_ROWS, n), F32),
        compiler_params=pltpu.CompilerParams(
            dimension_semantics=("parallel",), vmem_limit_bytes=VMEM_LIMIT),
        name="mod",
    )(cvec, w_ada, b_ada)


def _mod_row_map(rows_per_batch, fixed_row):
    if fixed_row is not None:
        return lambda i: (fixed_row, 0, 0)
    tiles = rows_per_batch // ROW_TILE
    return lambda i: (i // tiles, 0, 0)


def _inproj_kernel(x_ref, mod_ref, w_ref, mu_ref, q_ref, k_ref, v_ref, rw_ref, *, seq_len):
    mod = mod_ref[...]
    sh1 = mod[:, 0:D_MODEL]
    sc1 = mod[:, D_MODEL:2 * D_MODEL]
    h = (x_ref[...] * (1.0 + sc1) + sh1).astype(BF16)
    q_ref[...] = _dot(h, w_ref[:, 0:DA_WIDTH])
    k_ref[...] = _dot(h, w_ref[:, DA_WIDTH:2 * DA_WIDTH])
    v_ref[...] = _dot(h, w_ref[:, 2 * DA_WIDTH:3 * DA_WIDTH])
    rows = x_ref.shape[0]
    pos = lax.broadcasted_iota(jnp.int32, (rows, 1), 0) & (seq_len - 1)
    first = pos == 0
    last = pos == seq_len - 1
    for j in range(RW_COLS // SHIFT_COL_TILE):
        cols = slice(j * SHIFT_COL_TILE, (j + 1) * SHIFT_COL_TILE)
        rw = _dot(h, w_ref[:, 3 * DA_WIDTH + j * SHIFT_COL_TILE:3 * DA_WIDTH + (j + 1) * SHIFT_COL_TILE])
        prev = jnp.where(first, 0.0, pltpu.roll(rw, 1, 0))
        nxt = jnp.where(last, 0.0, pltpu.roll(rw, rows - 1, 0))
        rw_ref[:, cols] = rw + mu_ref[0:1, cols] * (prev - rw) + mu_ref[1:2, cols] * (nxt - rw)


def _input_projection(x2d, mod3, w_in, rw_mu, seq_len, fixed_row):
    m = x2d.shape[0]
    tile = max(seq_len, ROW_TILE)
    assert seq_len & (seq_len - 1) == 0 and tile % seq_len == 0
    if fixed_row is not None:
        mod_map = lambda i: (fixed_row, 0, 0)
    else:
        mod_map = lambda i: (i * tile // seq_len, 0, 0)
    row = lambda i: (i, 0)
    return pl.pallas_call(
        functools.partial(_inproj_kernel, seq_len=seq_len),
        grid=(m // tile,),
        in_specs=[
            pl.BlockSpec((tile, D_MODEL), row),
            pl.BlockSpec((None, 1, 6 * D_MODEL), mod_map),
            pl.BlockSpec((D_MODEL, N_IN), lambda i: (0, 0), pipeline_mode=pl.Buffered(1)),
            pl.BlockSpec((2, RW_COLS), lambda i: (0, 0)),
        ],
        out_specs=[
            pl.BlockSpec((tile, DA_WIDTH), row),
            pl.BlockSpec((tile, DA_WIDTH), row),
            pl.BlockSpec((tile, DA_WIDTH), row),
            pl.BlockSpec((tile, RW_COLS), row),
        ],
        out_shape=[
            jax.ShapeDtypeStruct((m, DA_WIDTH), F32),
            jax.ShapeDtypeStruct((m, DA_WIDTH), F32),
            jax.ShapeDtypeStruct((m, DA_WIDTH), F32),
            jax.ShapeDtypeStruct((m, RW_COLS), F32),
        ],
        compiler_params=pltpu.CompilerParams(
            dimension_semantics=("parallel",), vmem_limit_bytes=VMEM_LIMIT),
        name="inproj",
    )(x2d, mod3, w_in, rw_mu)


def _rope(x, cos, sin_signed):
    lane = lax.broadcasted_iota(jnp.int32, x.shape, 1)
    partner = jnp.where((lane & 63) < 32, pltpu.roll(x, 96, 1), pltpu.roll(x, 32, 1))
    return x * cos + partner * sin_signed


def _attn_kernel(*refs, has_ctx, seq_len, heads, lam_init):
    if has_ctx:
        q_ref, k_ref, v_ref, kc_ref, vc_ref, cos_ref, sin_ref, lq_ref, g_ref, o_ref = refs
    else:
        q_ref, k_ref, v_ref, lq_ref, g_ref, o_ref = refs
    d = DA_HEAD_DIM
    qrows = min(ATTN_Q_BLOCK, seq_len)
    lq = lq_ref[...]
    lam = (jnp.exp(jnp.sum(lq[0:1] * lq[1:2], axis=-1, keepdims=True))
           - jnp.exp(jnp.sum(lq[2:3] * lq[3:4], axis=-1, keepdims=True)) + lam_init)

    def with_ones(v):
        return jnp.concatenate([v.astype(BF16), jnp.ones(v.shape, BF16)], axis=1)

    keys, vals = {}, {}

    def head_operands(h):
        if h not in keys:
            lanes = slice(h * 2 * d, (h + 1) * 2 * d)
            k = k_ref[:, lanes]
            if has_ctx:
                k = _rope(k, cos_ref[...], sin_ref[...])
            keys[h] = [[k[:, m * d:(m + 1) * d].astype(BF16)] for m in range(2)]
            vals[h] = [with_ones(v_ref[:, lanes])]
            if has_ctx:
                kc = kc_ref[:, lanes]
                for m in range(2):
                    keys[h][m].append(kc[:, m * d:(m + 1) * d].astype(BF16))
                vals[h].append(with_ones(vc_ref[:, lanes]))
        return keys[h], vals[h]

    g = g_ref[...]

    def scores(h, qb, m):
        rows = slice(qb * qrows, (qb + 1) * qrows)
        q = q_ref[rows, h * 2 * d:(h + 1) * 2 * d]
        if has_ctx:
            q = _rope(q, cos_ref[rows, :], sin_ref[rows, :])
        qm = (q[:, m * d:(m + 1) * d] * (d ** -0.5 * LOG2_E)).astype(BF16)
        return [_dot(qm, kg, _NT) for kg in head_operands(h)[0][m]]

    def attend(h, ss):
        mx = ss[0].max(axis=-1, keepdims=True)
        for s in ss[1:]:
            mx = jnp.maximum(mx, s.max(axis=-1, keepdims=True))
        acc = None
        for s, vg in zip(ss, head_operands(h)[1]):
            o = _dot(jnp.exp2(s - mx).astype(BF16), vg)
            acc = o if acc is None else acc + o
        return acc[:, 0:2 * d] / acc[:, 2 * d:4 * d]

    units = [(h, qb, m) for h in range(heads) for qb in range(seq_len // qrows)
             for m in range(2)]
    pending = scores(*units[0])
    o1 = None
    for i, (h, qb, m) in enumerate(units):
        following = scores(*units[i + 1]) if i + 1 < len(units) else None
        o = attend(h, pending)
        pending = following
        if m == 0:
            o1 = o
            continue
        o = o1 - lam * o
        ms = jnp.mean(o * o, axis=-1, keepdims=True)
        rows = slice(qb * qrows, (qb + 1) * qrows)
        o_ref[rows, h * 2 * d:(h + 1) * 2 * d] = o * lax.rsqrt(ms + LN_EPS) * g * (1.0 - lam_init)


def _rope_tables(n):
    rows = n // GRID_W
    row = jnp.repeat(jnp.arange(rows, dtype=F32), GRID_W)
    col = jnp.tile(jnp.arange(GRID_W, dtype=F32), rows)
    inv = ROPE_BASE ** (-jnp.arange(ROPE_PAIRS_PER_AXIS, dtype=F32) / ROPE_PAIRS_PER_AXIS)
    ang = jnp.concatenate([row[:, None] * inv, col[:, None] * inv], -1)
    cos, sin = jnp.cos(ang), jnp.sin(ang)
    return jnp.tile(cos, (1, 4)), jnp.tile(jnp.concatenate([-sin, sin], -1), (1, 2))


def _attention(q2d, k2d, v2d, da_lambda, subln_g, batch, seq_len, layer, ctx=None):
    has_ctx = ctx is not None
    w = 2 * DA_HEAD_DIM
    heads = DA_HEADS if seq_len * DA_HEADS <= ATTN_MAX_ROWS else 1
    head = lambda b, h: (b, h)
    const = lambda b, h: (0, 0)
    in_specs = [pl.BlockSpec((seq_len, heads * w), head)] * 3
    args = [q2d, k2d, v2d]
    if has_ctx:
        kc, vc = ctx
        past = kc.shape[1]
        in_specs += [pl.BlockSpec((None, past, heads * w), lambda b, h: (b, 0, h))] * 2
        in_specs += [pl.BlockSpec((seq_len, w), const)] * 2
        args += [kc, vc, *_rope_tables(seq_len)]
    in_specs += [pl.BlockSpec((4, DA_HEAD_DIM), const), pl.BlockSpec((1, w), const)]
    args += [da_lambda, subln_g]
    lam_init = 0.8 - 0.6 * math.exp(-0.3 * layer)
    return pl.pallas_call(
        functools.partial(_attn_kernel, has_ctx=has_ctx, seq_len=seq_len, heads=heads,
                          lam_init=lam_init),
        grid=(batch, DA_HEADS // heads),
        in_specs=in_specs,
        out_specs=pl.BlockSpec((seq_len, heads * w), head),
        out_shape=jax.ShapeDtypeStruct((batch * seq_len, DA_WIDTH), F32),
        compiler_params=pltpu.CompilerParams(
            dimension_semantics=("parallel", "parallel"), vmem_limit_bytes=VMEM_LIMIT),
        name="attn",
    )(*args)


PAIR = 2 * RW_HEAD_DIM
N_PAIRS = RW_HEADS // 2
FINISH_ROWS = 256
RWKV_GROUP_ROWS = 1024

_M_STRICT = (0, 2)
_M_INCL = (1, 3)
_M_BLOCK8, _M_OFF16, _M_OFF32, _M_OFF64, _M_EYE = 4, 5, 6, 7, 8


def _pair_masks():
    t = jnp.arange(PAIR)[:, None]
    s = jnp.arange(PAIR)[None, :]
    same = lambda n: (t // n) == (s // n)
    head = same(CHUNK)
    masks = [head & (t > s), head & (t >= s), head & (t < s), head & (t <= s),
             same(8), same(16) & ~same(8), same(32) & ~same(16), head & ~same(32), t == s]
    return jnp.stack(masks).astype(BF16)


def _cumsum_matrices():
    t = jnp.arange(CHUNK)[:, None]
    s = jnp.arange(CHUNK)[None, :]
    return jnp.stack([t >= s, t <= s]).astype(BF16)


def _pair_sum_matrix():
    i = jnp.arange(PAIR)
    return ((i[:, None] // RW_HEAD_DIM) == (i[None, :] // RW_HEAD_DIM)).astype(BF16)


def _split2(x):
    hi = x.astype(BF16)
    lo = (x - hi.astype(F32)).astype(BF16)
    return hi, lo


def _head_sums(x, e_ref):
    rows = x.shape[0]
    xs = jnp.concatenate([x[:, p * PAIR:(p + 1) * PAIR] for p in range(N_PAIRS)], axis=0)
    s = _dot(xs.astype(BF16), e_ref[...])
    return jnp.concatenate([s[p * rows:(p + 1) * rows] for p in range(N_PAIRS)], axis=1)


def _run_schedule(order, **stages):
    for name in order:
        if name in stages:
            next(stages[name], None)
    for gen in stages.values():
        for _ in gen:
            pass


_STEP_ORDER = "B B B B C B A B C B A B C B A B".split()


def _rwkv_kernel(*refs, seq_len, group, has_state_in, has_state_out):
    refs = list(refs)
    rw_ref = refs.pop(0)
    s0_ref = refs.pop(0) if has_state_in else None
    (w0_ref, wup_ref, a0_ref, aup_ref, gup_ref, kk_ref, ka_ref, rk_ref,
     lng_ref, lnb_ref, e_ref, tri_ref, m_ref) = refs[:13]
    refs = refs[13:]
    o_ref = refs.pop(0)
    so_ref = refs.pop(0) if has_state_out else None
    (y_scr, bv_scr, gr_scr, s_scr, sio_scr, ar_scr, bk_scr, v_scr, ge_scr,
     t_scr, wy_scr, rl_scr) = refs

    C = CHUNK
    N = RW_HEAD_DIM
    W = RW_WIDTH
    P = PAIR
    nc = seq_len // C
    n_steps = group * nc
    total_rows = group * seq_len
    assert nc & (nc - 1) == 0 and n_steps % 2 == 0
    mm = lambda x, y: _dot(x.astype(BF16), y.astype(BF16))
    mmb = lambda x, y: _dot(x, y).astype(BF16)
    zeros_nn = jnp.zeros((N, N), F32)
    chains = [(d, p) for d in range(2) for p in range(N_PAIRS)]
    n = range(len(chains))

    if has_state_in:
        for g in range(group):
            for d, p in chains:
                top = jnp.concatenate([s0_ref[g, d, 2 * p], zeros_nn], axis=1)
                bot = jnp.concatenate([zeros_nn, s0_ref[g, d, 2 * p + 1]], axis=1)
                sio_scr[g, d, p] = jnp.concatenate([top, bot], axis=0)
    for d, p in chains:
        s_scr[d, p] = jnp.zeros((P, P), F32)

    for scr in (ar_scr, bk_scr, v_scr):
        scr[...] = jnp.zeros(scr.shape, scr.dtype)

    def put_block_diag(scr, slot, d, row0, x):
        xb = x.astype(BF16)
        for p in range(N_PAIRS):
            for hh in range(2):
                lanes = slice(hh * N, (hh + 1) * N)
                scr[slot, d, p, row0 + hh * C:row0 + (hh + 1) * C, lanes] = (
                    xb[:, p * P + hh * N:p * P + (hh + 1) * N])

    def step_chunks(i):
        g = i // nc
        il = i % nc
        return g, il, (il, nc - 1 - il)

    def prep_stages(i, slot):
        g, _, cs = step_chunks(i)
        st = []
        for d, c in enumerate(cs):
            r0 = pl.multiple_of(g * seq_len + c * C, C)
            rows = pl.ds(r0, C)
            xm = rw_ref[rows, :]
            w_lo = xm[:, 3 * W:3 * W + DECAY_LORA]
            a_lo = xm[:, 3 * W + DECAY_LORA:3 * W + DECAY_LORA + AAA_LORA]
            st.append(dict(rows=rows, r=xm[:, 0:W], kr=xm[:, W:2 * W], vr=xm[:, 2 * W:3 * W],
                           g_lo=xm[:, 3 * W + DECAY_LORA + AAA_LORA:RW_COLS],
                           w_up=_dot(jnp.tanh(w_lo).astype(BF16), wup_ref[d]),
                           a_up=_dot(a_lo.astype(BF16), aup_ref[d])))
        yield
        for d, x in enumerate(st):
            x['log_decay'] = -math.exp(-0.5) * jax.nn.sigmoid(w0_ref[d:d + 1, :] + x['w_up'])
            x['a'] = jax.nn.sigmoid(a0_ref[d:d + 1, :] + x['a_up'])
            x['keff'] = x['kr'] * (1.0 + (x['a'] - 1.0) * ka_ref[...])
            x['kk'] = x['kr'] * kk_ref[...]
            x['sums'] = _head_sums(
                jnp.concatenate([x['kk'] * x['kk'], x['r'] * x['keff'] * rk_ref[...]], axis=0), e_ref)
            if d == 0:
                gr_scr[x['rows'], :] = _dot(jax.nn.sigmoid(x['g_lo']).astype(BF16), gup_ref[...])
            cum = _dot(tri_ref[d], jnp.concatenate(_split2(x['log_decay']), axis=1))
            x['cum'] = cum[:, 0:W] + cum[:, W:2 * W]
        yield
        for d, x in enumerate(st):
            kk = x['kk'] / jnp.maximum(jnp.sqrt(x['sums'][0:C]), 1e-12)
            bv_scr[d, x['rows'], :] = x['sums'][C:2 * C] * x['vr']
            cum = x['cum']
            g_in = jnp.exp(cum)
            g_inv = jnp.exp(-cum)
            a_t = -kk * jnp.exp(cum - x['log_decay'])
            r_t = x['r'] * g_in
            b_t = kk * x['a'] * g_inv
            k_t = x['keff'] * g_inv
            ge_scr[slot, d] = g_in[C - 1:C, :] if d == 0 else g_in[0:1, :]
            put_block_diag(ar_scr, slot, d, 0, a_t)
            put_block_diag(ar_scr, slot, d, P, r_t)
            put_block_diag(bk_scr, slot, d, 0, b_t)
            put_block_diag(bk_scr, slot, d, P, k_t)
            put_block_diag(v_scr, slot, d, 0, x['vr'])
        yield

    def local_stages(slot):
        ar = [ar_scr[slot, d, p] for d, p in chains]
        bk = [bk_scr[slot, d, p] for d, p in chains]
        vb = [v_scr[slot, d, p] for d, p in chains]
        strict = [m_ref[_M_STRICT[d]] for d, _ in chains]
        incl = [m_ref[_M_INCL[d]] for d, _ in chains]
        x = [_dot(ar[i], bk[i], _NT).astype(BF16) for i in n]
        yield
        amat = [x[i][0:P, 0:P] * strict[i] for i in n]
        a8 = [a * m_ref[_M_BLOCK8] for a in amat]
        a2 = [mmb(a, a) for a in a8]
        for i, (d, p) in enumerate(chains):
            rl_scr[slot, d, p] = _dot(x[i][0:P, P:2 * P] * strict[i], vb[i])
            wy_scr[slot, d, p] = jnp.concatenate(
                [x[i][P:2 * P, 0:P] * incl[i], x[i][P:2 * P, P:2 * P] * incl[i]], axis=1)
        yield
        a4 = [mmb(a, a) for a in a2]
        ps = [m_ref[_M_EYE] + a for a in a8]
        ps = [t + mmb(t, a) for t, a in zip(ps, a2)]
        yield
        ps = [t + mmb(t, a) for t, a in zip(ps, a4)]
        yield
        for off in (_M_OFF16, _M_OFF32, _M_OFF64):
            ts = [mmb(t, a * m_ref[off]) for t, a in zip(ps, amat)]
            yield
            ps = [t + mmb(q, t) for t, q in zip(ps, ts)]
            if off == _M_OFF64:
                for i, (d, p) in enumerate(chains):
                    t_scr[slot, d, p] = ps[i]
            yield

    def state_stages(i, slot):
        g, il, cs = step_chunks(i)
        ar = [ar_scr[slot, d, p] for d, p in chains]
        bk = [bk_scr[slot, d, p] for d, p in chains]
        vb = [v_scr[slot, d, p] for d, p in chains]
        g_end = [ge_scr[slot, d] for d in range(2)]
        if has_state_in:
            s_old = [jnp.where(il == 0, sio_scr[g, d, p], s_scr[d, p]) for d, p in chains]
        else:
            s_old = [jnp.where(il == 0, 0.0, s_scr[d, p]) for d, p in chains]
        xs = [_dot(ar[i], s_old[i].astype(BF16), _NT) for i in n]
        yield
        u = [mm(t_scr[slot, d, p], xs[i][0:P] + rl_scr[slot, d, p]).astype(BF16)
             for i, (d, p) in enumerate(chains)]
        yield
        uv = [jnp.concatenate([u[i], vb[i]], axis=0) for i in n]
        y = [xs[i][P:2 * P] + _dot(wy_scr[slot, d, p], uv[i]) for i, (d, p) in enumerate(chains)]
        s_new = [s_old[i] + _dot(uv[i], bk[i], _TN) for i in n]
        for i, (d, p) in enumerate(chains):
            sl = slice(p * P, (p + 1) * P)
            rows = pl.ds(pl.multiple_of(g * seq_len + cs[d] * C, C), C)
            y_scr[d, rows, sl] = y[i][0:C] + y[i][C:2 * C]
            s_end = s_new[i] * g_end[d][:, sl]
            s_scr[d, p] = s_end
            if has_state_out:
                sio_scr[g, d, p] = s_end
        yield

    for _ in prep_stages(0, 0):
        pass
    _run_schedule("B A B A B A".split(), A=prep_stages(1, 1), B=local_stages(0))

    def scan_body(j, carry):
        i = 2 * j
        _run_schedule(_STEP_ORDER, C=state_stages(i, 0), B=local_stages(1), A=prep_stages(i + 2, 0))
        _run_schedule(_STEP_ORDER, C=state_stages(i + 1, 1), B=local_stages(0), A=prep_stages(i + 3, 1))
        return carry

    lax.fori_loop(0, n_steps // 2 - 1, scan_body, 0)
    _run_schedule(_STEP_ORDER, C=state_stages(jnp.int32(n_steps - 2), 0), B=local_stages(1))
    _run_schedule(_STEP_ORDER, C=state_stages(jnp.int32(n_steps - 1), 1))

    def finish_body(i, carry):
        rows = pl.ds(pl.multiple_of(i * FINISH_ROWS, FINISH_ROWS), FINISH_ROWS)
        y = y_scr[0, rows, :] + y_scr[1, rows, :]
        yc = y - _head_sums(y, e_ref) * (1.0 / N)
        var = _head_sums(yc * yc, e_ref) * (1.0 / N)
        yn = yc * lax.rsqrt(var + RW_LNX_EPS) * lng_ref[...] + lnb_ref[...]
        o_ref[rows, :] = (yn + bv_scr[0, rows, :] + bv_scr[1, rows, :]) * gr_scr[rows, :]
        return carry

    lax.fori_loop(0, total_rows // FINISH_ROWS, finish_body, 0)
    if has_state_out:
        for g in range(group):
            for d, p in chains:
                s_pair = sio_scr[g, d, p]
                so_ref[g, d, 2 * p] = s_pair[0:N, 0:N]
                so_ref[g, d, 2 * p + 1] = s_pair[N:P, N:P]


def _rwkv(rw2d, p, batch, seq_len, state_in=None, want_state=False):
    has_state_in = state_in is not None
    assert not (has_state_in and want_state)
    group = max(1, RWKV_GROUP_ROWS // seq_len)
    assert batch % group == 0
    rows = group * seq_len
    const2 = lambda b: (0, 0)
    const3 = lambda b: (0, 0, 0)
    state_spec = pl.BlockSpec((group, 2, RW_HEADS, RW_HEAD_DIM, RW_HEAD_DIM), lambda b: (b, 0, 0, 0, 0))
    in_specs = [pl.BlockSpec((rows, RW_COLS), lambda b: (b, 0))]
    args = [rw2d]
    if has_state_in:
        in_specs.append(state_spec)
        args.append(state_in)
    in_specs += [
        pl.BlockSpec((2, RW_WIDTH), const2),
        pl.BlockSpec((2, DECAY_LORA, RW_WIDTH), const3),
        pl.BlockSpec((2, RW_WIDTH), const2),
        pl.BlockSpec((2, AAA_LORA, RW_WIDTH), const3),
        pl.BlockSpec((GATE_LORA, RW_WIDTH), const2),
        pl.BlockSpec((1, RW_WIDTH), const2),
        pl.BlockSpec((1, RW_WIDTH), const2),
        pl.BlockSpec((1, RW_WIDTH), const2),
        pl.BlockSpec((1, RW_WIDTH), const2),
        pl.BlockSpec((1, RW_WIDTH), const2),
        pl.BlockSpec((PAIR, PAIR), const2),
        pl.BlockSpec((2, CHUNK, CHUNK), const3),
        pl.BlockSpec((9, PAIR, PAIR), const3),
    ]
    args += [p['rw_w0'], p['rw_w_up'].astype(BF16), p['rw_a0'], p['rw_a_up'].astype(BF16),
             p['rw_g_up'].astype(BF16), p['rw_k_k'][None], p['rw_k_a'][None],
             p['rw_r_k'].reshape(1, RW_WIDTH), p['rw_lnx_g'][None], p['rw_lnx_b'][None],
             _pair_sum_matrix(), _cumsum_matrices(), _pair_masks()]
    out_specs = [pl.BlockSpec((rows, RW_WIDTH), lambda b: (b, 0))]
    out_shape = [jax.ShapeDtypeStruct((batch * seq_len, RW_WIDTH), F32)]
    if want_state:
        out_specs.append(state_spec)
        out_shape.append(jax.ShapeDtypeStruct((batch, 2, RW_HEADS, RW_HEAD_DIM, RW_HEAD_DIM), F32))
    per_chain = (2, 2, N_PAIRS)
    outs = pl.pallas_call(
        functools.partial(_rwkv_kernel, seq_len=seq_len, group=group, has_state_in=has_state_in,
                          has_state_out=want_state),
        grid=(batch // group,),
        in_specs=in_specs,
        out_specs=out_specs,
        out_shape=out_shape,
        scratch_shapes=[
            pltpu.VMEM((2, rows, RW_WIDTH), F32),
            pltpu.VMEM((2, rows, RW_WIDTH), F32),
            pltpu.VMEM((rows, RW_WIDTH), F32),
            pltpu.VMEM((2, N_PAIRS, PAIR, PAIR), F32),
            pltpu.VMEM((group, 2, N_PAIRS, PAIR, PAIR), F32),
            pltpu.VMEM(per_chain + (2 * PAIR, PAIR), BF16),
            pltpu.VMEM(per_chain + (2 * PAIR, PAIR), BF16),
            pltpu.VMEM(per_chain + (PAIR, PAIR), BF16),
            pltpu.VMEM((2, 2, 1, RW_WIDTH), F32),
            pltpu.VMEM(per_chain + (PAIR, PAIR), BF16),
            pltpu.VMEM(per_chain + (PAIR, 2 * PAIR), BF16),
            pltpu.VMEM(per_chain + (PAIR, PAIR), F32),
        ],
        compiler_params=pltpu.CompilerParams(
            dimension_semantics=("parallel",), vmem_limit_bytes=VMEM_LIMIT),
        name="rwkv",
    )(*args)
    return outs if want_state else (outs[0], None)


def _merge_kernel(x_ref, mod_ref, oa_ref, yg_ref, win_ref, woa_ref, wor_ref, wout_ref,
                  g_ref, b_ref, o_ref):
    mod = mod_ref[...]
    sh1 = mod[:, 0:D_MODEL]
    sc1 = mod[:, D_MODEL:2 * D_MODEL]
    g1 = mod[:, 2 * D_MODEL:3 * D_MODEL]
    x = x_ref[...]
    h = (x * (1.0 + sc1) + sh1).astype(BF16)
    gates = jax.nn.sigmoid(_dot(h, win_ref[:, QKVR_COLS:N_IN]))
    att = _dot(oa_ref[...].astype(BF16), woa_ref[...])
    rwk = _dot(yg_ref[...].astype(BF16), wor_ref[...])
    merged = gates[:, 0:D_MODEL] * att + gates[:, D_MODEL:2 * D_MODEL] * rwk
    mix = _dot(merged.astype(BF16), wout_ref[...])
    o_ref[...] = _layer_norm(ALPHA * x + g1 * mix, g_ref[...], b_ref[...], LN_EPS)


def _merge(x2d, mod3, o_att, yg, w_in, p, seq_len, fixed_row):
    m = x2d.shape[0]
    row = lambda i: (i, 0)
    const = lambda i: (0, 0)
    return pl.pallas_call(
        _merge_kernel,
        grid=(m // ROW_TILE,),
        in_specs=[
            pl.BlockSpec((ROW_TILE, D_MODEL), row),
            pl.BlockSpec((None, 1, 6 * D_MODEL), _mod_row_map(seq_len, fixed_row)),
            pl.BlockSpec((ROW_TILE, DA_WIDTH), row),
            pl.BlockSpec((ROW_TILE, RW_WIDTH), row),
            pl.BlockSpec((D_MODEL, N_IN), const, pipeline_mode=pl.Buffered(1)),
            pl.BlockSpec((DA_WIDTH, D_MODEL), const),
            pl.BlockSpec((RW_WIDTH, D_MODEL), const),
            pl.BlockSpec((D_MODEL, D_MODEL), const),
            pl.BlockSpec((1, D_MODEL), const),
            pl.BlockSpec((1, D_MODEL), const),
        ],
        out_specs=pl.BlockSpec((ROW_TILE, D_MODEL), row),
        out_shape=jax.ShapeDtypeStruct((m, D_MODEL), F32),
        compiler_params=pltpu.CompilerParams(
            dimension_semantics=("parallel",), vmem_limit_bytes=VMEM_LIMIT),
        name="merge",
    )(x2d, mod3, o_att, yg, w_in, p['w_o_attn'].astype(BF16), p['w_o_rwkv'].astype(BF16),
      p['w_out'].astype(BF16), p['ln1_g'][None], p['ln1_b'][None])


def _mlp_kernel(x_ref, mod_ref, wup_ref, cw_ref, cb_ref, wd_ref, g_ref, b_ref, o_ref,
                h_scr, act_scr, *, seq_len):
    mod = mod_ref[...]
    sh2 = mod[:, 3 * D_MODEL:4 * D_MODEL]
    sc2 = mod[:, 4 * D_MODEL:5 * D_MODEL]
    g2 = mod[:, 5 * D_MODEL:6 * D_MODEL]
    h_scr[...] = (x_ref[...] * (1.0 + sc2) + sh2).astype(BF16)
    rows = x_ref.shape[0]
    pos = lax.broadcasted_iota(jnp.int32, (rows, 1), 0) & (seq_len - 1)
    first = pos == 0
    last = pos == seq_len - 1
    for j in range(D_FF // FF_TILE):
        cols = slice(j * FF_TILE, (j + 1) * FF_TILE)
        h = h_scr[...]
        u = _dot(h, wup_ref[:, cols])
        val = _dot(h, wup_ref[:, D_FF + j * FF_TILE:D_FF + (j + 1) * FF_TILE])
        prev = jnp.where(first, 0.0, pltpu.roll(u, 1, 0))
        nxt = jnp.where(last, 0.0, pltpu.roll(u, rows - 1, 0))
        cw = cw_ref[:, cols]
        u = prev * cw[0:1, :] + u * cw[1:2, :] + nxt * cw[2:3, :] + cb_ref[:, cols]
        act_scr[:, cols] = (jax.nn.gelu(u) * val).astype(BF16)
    f = _dot(act_scr[...], wd_ref[...])
    o_ref[...] = _layer_norm(ALPHA * x_ref[...] + g2 * f, g_ref[...], b_ref[...], LN_EPS)


def _mlp(x2d, mod3, p, seq_len, fixed_row):
    m = x2d.shape[0]
    assert seq_len & (seq_len - 1) == 0 and MLP_ROW_TILE % seq_len == 0
    if fixed_row is not None:
        mod_map = lambda i: (fixed_row, 0, 0)
    else:
        mod_map = lambda i: (i * MLP_ROW_TILE // seq_len, 0, 0)
    row = lambda i: (i, 0)
    const = lambda i: (0, 0)
    resident = pl.Buffered(1)
    return pl.pallas_call(
        functools.partial(_mlp_kernel, seq_len=seq_len),
        grid=(m // MLP_ROW_TILE,),
        in_specs=[
            pl.BlockSpec((MLP_ROW_TILE, D_MODEL), row),
            pl.BlockSpec((None, 1, 6 * D_MODEL), mod_map),
            pl.BlockSpec((D_MODEL, 2 * D_FF), const, pipeline_mode=resident),
            pl.BlockSpec((3, D_FF), const),
            pl.BlockSpec((1, D_FF), const),
            pl.BlockSpec((D_FF, D_MODEL), const, pipeline_mode=resident),
            pl.BlockSpec((1, D_MODEL), const),
            pl.BlockSpec((1, D_MODEL), const),
        ],
        out_specs=pl.BlockSpec((MLP_ROW_TILE, D_MODEL), row),
        out_shape=jax.ShapeDtypeStruct((m, D_MODEL), F32),
        scratch_shapes=[
            pltpu.VMEM((MLP_ROW_TILE, D_MODEL), BF16),
            pltpu.VMEM((MLP_ROW_TILE, D_FF), BF16),
        ],
        compiler_params=pltpu.CompilerParams(
            dimension_semantics=("parallel",), vmem_limit_bytes=MLP_VMEM_LIMIT),
        name="mlp",
    )(x2d, mod3, p['w_up'].astype(BF16), p['conv_w'], p['conv_b'][None], p['w_down'].astype(BF16),
      p['ln2_g'][None], p['ln2_b'][None])


def _trunk_layer(x, mod3, fixed_row, p, w_in, layer, ctx=None):
    batch, seq_len, _ = x.shape
    x2d = x.reshape(batch * seq_len, D_MODEL)
    q, k, v, rw = _input_projection(x2d, mod3, w_in, p['rw_mu'], seq_len, fixed_row)
    if ctx is None:
        o_att = _attention(q, k, v, p['da_lambda'], p['da_subln_g'][None], batch, seq_len, layer)
        yg, state = _rwkv(rw, p, batch, seq_len, want_state=True)
    else:
        k_ctx, v_ctx, s_ctx = ctx
        past = k_ctx.shape[1]
        o_att = _attention(q, k, v, p['da_lambda'], p['da_subln_g'][None], batch, seq_len, layer,
                           ctx=(k_ctx.reshape(batch, past, DA_WIDTH), v_ctx.reshape(batch, past, DA_WIDTH)))
        yg, state = _rwkv(rw, p, batch, seq_len, state_in=s_ctx)
    x1 = _merge(x2d, mod3, o_att, yg, w_in, p, seq_len, fixed_row)
    y = _mlp(x1, mod3, p, seq_len, fixed_row)
    new_ctx = None
    if ctx is None:
        new_ctx = (k.reshape(batch, seq_len, DA_HEADS, 2, DA_HEAD_DIM),
                   v.reshape(batch, seq_len, DA_HEADS, 2 * DA_HEAD_DIM), state)
    return y.reshape(batch, seq_len, D_MODEL), new_ctx


def kernel(x_prompt, x_sample, cache_k, cache_v, state_rwkv, c, c_ctx, w_ada, b_ada, w_in, rw_mu, rw_w0, rw_w_up, rw_a0, rw_a_up, rw_g_up, rw_k_k, rw_k_a, rw_r_k, rw_lnx_g, rw_lnx_b, da_lambda, da_subln_g, w_o_attn, w_o_rwkv, w_out, ln1_g, ln1_b, w_up, conv_w, conv_b, w_down, ln2_g, ln2_b):
    dec_batch = x_sample.shape[0]
    assert dec_batch < MOD_ROWS
    y_prompt, y_sample = x_prompt, x_sample
    new_k, new_v, new_s = [], [], []
    for l in range(DEPTH):
        p = {
            'rw_mu': rw_mu[l], 'rw_w0': rw_w0[l], 'rw_w_up': rw_w_up[l], 'rw_a0': rw_a0[l],
            'rw_a_up': rw_a_up[l], 'rw_g_up': rw_g_up[l], 'rw_k_k': rw_k_k[l], 'rw_k_a': rw_k_a[l],
            'rw_r_k': rw_r_k[l], 'rw_lnx_g': rw_lnx_g[l], 'rw_lnx_b': rw_lnx_b[l],
            'da_lambda': da_lambda[l], 'da_subln_g': da_subln_g[l], 'w_o_attn': w_o_attn[l],
            'w_o_rwkv': w_o_rwkv[l], 'w_out': w_out[l], 'ln1_g': ln1_g[l], 'ln1_b': ln1_b[l],
            'w_up': w_up[l], 'conv_w': conv_w[l], 'conv_b': conv_b[l], 'w_down': w_down[l],
            'ln2_g': ln2_g[l], 'ln2_b': ln2_b[l],
        }
        cvec = jnp.concatenate(
            [c, c_ctx[None], jnp.zeros((MOD_ROWS - dec_batch - 1, D_MODEL), F32)], axis=0)
        mod3 = _modulation(cvec, w_ada[l], b_ada[l][None]).reshape(MOD_ROWS, 1, 6 * D_MODEL)
        w_in_l = w_in[l].astype(BF16)
        y_prompt, ctx_l = _trunk_layer(y_prompt, mod3, dec_batch, p, w_in_l, l)
        new_k.append(ctx_l[0])
        new_v.append(ctx_l[1])
        new_s.append(ctx_l[2])
        y_sample, _ = _trunk_layer(y_sample, mod3, None, p, w_in_l, l,
                                   ctx=(cache_k[:, l], cache_v[:, l], state_rwkv[:, l]))
    return (y_prompt, y_sample, jnp.stack(new_k, axis=1), jnp.stack(new_v, axis=1),
            jnp.stack(new_s, axis=1))
```

```python
import functools
import math

import jax
import jax.numpy as jnp
from jax import lax
from jax.experimental import pallas as pl
from jax.experimental.pallas import tpu as pltpu

F32 = jnp.float32
BF16 = jnp.bfloat16
HIGHEST = lax.Precision.HIGHEST

D_MODEL = 1024
GRID_W = 64
DA_HEADS = 4
DA_HEAD_DIM = 64
DA_WIDTH = DA_HEADS * 2 * DA_HEAD_DIM
ROPE_PAIRS_PER_AXIS = DA_HEAD_DIM // 4
ROPE_BASE = 10000.0
RW_HEADS = 8
RW_HEAD_DIM = 64
RW_WIDTH = RW_HEADS * RW_HEAD_DIM
DECAY_LORA = 64
AAA_LORA = 64
GATE_LORA = 128
RW_COLS = 3 * RW_WIDTH + DECAY_LORA + AAA_LORA + GATE_LORA
RW_LNX_EPS = 64e-5
QKVR_COLS = 3 * DA_WIDTH + RW_COLS
N_IN = QKVR_COLS + 2 * D_MODEL
D_FF = 2816
LN_EPS = 1e-5
DEPTH = 1
ALPHA = (2.0 * DEPTH) ** 0.25
LOG2_E = math.log2(math.e)

CHUNK = 64
ATTN_Q_BLOCK = 256
ATTN_MAX_ROWS = 4096
ROW_TILE = 512
SHIFT_COL_TILE = 256
MLP_ROW_TILE = 1024
FF_TILE = 256
MOD_COL_TILE = 768
MOD_ROWS = 16
VMEM_LIMIT = 48 * 1024 * 1024
MLP_VMEM_LIMIT = 56 * 1024 * 1024

_NN = (((1,), (0,)), ((), ()))
_NT = (((1,), (1,)), ((), ()))
_TN = (((0,), (0,)), ((), ()))


def _dot(a, b, dims=_NN, precision=None):
    return lax.dot_general(a, b, dims, precision=precision, preferred_element_type=F32)


def _layer_norm(z, g, b, eps):
    mu = jnp.mean(z, axis=-1, keepdims=True)
    zc = z - mu
    var = jnp.mean(zc * zc, axis=-1, keepdims=True)
    return zc * lax.rsqrt(var + eps) * g + b


def _mod_kernel(c_ref, w_ref, b_ref, o_ref):
    cv = c_ref[...]
    s = cv * jax.nn.sigmoid(cv)
    o_ref[...] = _dot(s, w_ref[...], precision=HIGHEST) + b_ref[...]


def _modulation(cvec, w_ada, b_ada):
    n = w_ada.shape[1]
    return pl.pallas_call(
        _mod_kernel,
        grid=(n // MOD_COL_TILE,),
        in_specs=[
            pl.BlockSpec((MOD_ROWS, D_MODEL), lambda j: (0, 0)),
            pl.BlockSpec((D_MODEL, MOD_COL_TILE), lambda j: (0, j)),
            pl.BlockSpec((1, MOD_COL_TILE), lambda j: (0, j)),
        ],
        out_specs=pl.BlockSpec((MOD_ROWS, MOD_COL_TILE), lambda j: (0, j)),
        out_shape=jax.ShapeDtypeStruct((MOD_ROWS, n), F32),
        compiler_params=pltpu.CompilerParams(
            dimension_semantics=("parallel",), vmem_limit_bytes=VMEM_LIMIT),
        name="mod",
    )(cvec, w_ada, b_ada)


def _mod_row_map(rows_per_batch, fixed_row):
    if fixed_row is not None:
        return lambda i: (fixed_row, 0, 0)
    tiles = rows_per_batch // ROW_TILE
    return lambda i: (i // tiles, 0, 0)


def _inproj_kernel(x_ref, mod_ref, w_ref, mu_ref, q_ref, k_ref, v_ref, rw_ref, *, seq_len):
    mod = mod_ref[...]
    sh1 = mod[:, 0:D_MODEL]
    sc1 = mod[:, D_MODEL:2 * D_MODEL]
    h = (x_ref[...] * (1.0 + sc1) + sh1).astype(BF16)
    q_ref[...] = _dot(h, w_ref[:, 0:DA_WIDTH])
    k_ref[...] = _dot(h, w_ref[:, DA_WIDTH:2 * DA_WIDTH])
    v_ref[...] = _dot(h, w_ref[:, 2 * DA_WIDTH:3 * DA_WIDTH])
    rows = x_ref.shape[0]
    pos = lax.broadcasted_iota(jnp.int32, (rows, 1), 0) & (seq_len - 1)
    first = pos == 0
    last = pos == seq_len - 1
    for j in range(RW_COLS // SHIFT_COL_TILE):
        cols = slice(j * SHIFT_COL_TILE, (j + 1) * SHIFT_COL_TILE)
        rw = _dot(h, w_ref[:, 3 * DA_WIDTH + j * SHIFT_COL_TILE:3 * DA_WIDTH + (j + 1) * SHIFT_COL_TILE])
        prev = jnp.where(first, 0.0, pltpu.roll(rw, 1, 0))
        nxt = jnp.where(last, 0.0, pltpu.roll(rw, rows - 1, 0))
        rw_ref[:, cols] = rw + mu_ref[0:1, cols] * (prev - rw) + mu_ref[1:2, cols] * (nxt - rw)


def _input_projection(x2d, mod3, w_in, rw_mu, seq_len, fixed_row):
    m = x2d.shape[0]
    tile = max(seq_len, ROW_TILE)
    assert seq_len & (seq_len - 1) == 0 and tile % seq_len == 0
    if fixed_row is not None:
        mod_map = lambda i: (fixed_row, 0, 0)
    else:
        mod_map = lambda i: (i * tile // seq_len, 0, 0)
    row = lambda i: (i, 0)
    return pl.pallas_call(
        functools.partial(_inproj_kernel, seq_len=seq_len),
        grid=(m // tile,),
        in_specs=[
            pl.BlockSpec((tile, D_MODEL), row),
            pl.BlockSpec((None, 1, 6 * D_MODEL), mod_map),
            pl.BlockSpec((D_MODEL, N_IN), lambda i: (0, 0), pipeline_mode=pl.Buffered(1)),
            pl.BlockSpec((2, RW_COLS), lambda i: (0, 0)),
        ],
        out_specs=[
            pl.BlockSpec((tile, DA_WIDTH), row),
            pl.BlockSpec((tile, DA_WIDTH), row),
            pl.BlockSpec((tile, DA_WIDTH), row),
            pl.BlockSpec((tile, RW_COLS), row),
        ],
        out_shape=[
            jax.ShapeDtypeStruct((m, DA_WIDTH), F32),
            jax.ShapeDtypeStruct((m, DA_WIDTH), F32),
            jax.ShapeDtypeStruct((m, DA_WIDTH), F32),
            jax.ShapeDtypeStruct((m, RW_COLS), F32),
        ],
        compiler_params=pltpu.CompilerParams(
            dimension_semantics=("parallel",), vmem_limit_bytes=VMEM_LIMIT),
        name="inproj",
    )(x2d, mod3, w_in, rw_mu)


def _rope(x, cos, sin_signed):
    lane = lax.broadcasted_iota(jnp.int32, x.shape, 1)
    partner = jnp.where((lane & 63) < 32, pltpu.roll(x, 96, 1), pltpu.roll(x, 32, 1))
    return x * cos + partner * sin_signed


def _attn_kernel(*refs, has_ctx, seq_len, heads, lam_init):
    if has_ctx:
        q_ref, k_ref, v_ref, kc_ref, vc_ref, cos_ref, sin_ref, lq_ref, g_ref, o_ref = refs
    else:
        q_ref, k_ref, v_ref, lq_ref, g_ref, o_ref = refs
    d = DA_HEAD_DIM
    qrows = min(ATTN_Q_BLOCK, seq_len)
    lq = lq_ref[...]
    lam = (jnp.exp(jnp.sum(lq[0:1] * lq[1:2], axis=-1, keepdims=True))
           - jnp.exp(jnp.sum(lq[2:3] * lq[3:4], axis=-1, keepdims=True)) + lam_init)

    def with_ones(v):
        return jnp.concatenate([v.astype(BF16), jnp.ones(v.shape, BF16)], axis=1)

    keys, vals = {}, {}

    def head_operands(h):
        if h not in keys:
            lanes = slice(h * 2 * d, (h + 1) * 2 * d)
            k = k_ref[:, lanes]
            if has_ctx:
                k = _rope(k, cos_ref[...], sin_ref[...])
            keys[h] = [[k[:, m * d:(m + 1) * d].astype(BF16)] for m in range(2)]
            vals[h] = [with_ones(v_ref[:, lanes])]
            if has_ctx:
                kc = kc_ref[:, lanes]
                for m in range(2):
                    keys[h][m].append(kc[:, m * d:(m + 1) * d].astype(BF16))
                vals[h].append(with_ones(vc_ref[:, lanes]))
        return keys[h], vals[h]

    g = g_ref[...]

    def scores(h, qb, m):
        rows = slice(qb * qrows, (qb + 1) * qrows)
        q = q_ref[rows, h * 2 * d:(h + 1) * 2 * d]
        if has_ctx:
            q = _rope(q, cos_ref[rows, :], sin_ref[rows, :])
        qm = (q[:, m * d:(m + 1) * d] * (d ** -0.5 * LOG2_E)).astype(BF16)
        return [_dot(qm, kg, _NT) for kg in head_operands(h)[0][m]]

    def attend(h, ss):
        mx = ss[0].max(axis=-1, keepdims=True)
        for s in ss[1:]:
            mx = jnp.maximum(mx, s.max(axis=-1, keepdims=True))
        acc = None
        for s, vg in zip(ss, head_operands(h)[1]):
            o = _dot(jnp.exp2(s - mx).astype(BF16), vg)
            acc = o if acc is None else acc + o
        return acc[:, 0:2 * d] / acc[:, 2 * d:4 * d]

    units = [(h, qb, m) for h in range(heads) for qb in range(seq_len // qrows)
             for m in range(2)]
    pending = scores(*units[0])
    o1 = None
    for i, (h, qb, m) in enumerate(units):
        following = scores(*units[i + 1]) if i + 1 < len(units) else None
        o = attend(h, pending)
        pending = following
        if m == 0:
            o1 = o
            continue
        o = o1 - lam * o
        ms = jnp.mean(o * o, axis=-1, keepdims=True)
        rows = slice(qb * qrows, (qb + 1) * qrows)
        o_ref[rows, h * 2 * d:(h + 1) * 2 * d] = o * lax.rsqrt(ms + LN_EPS) * g * (1.0 - lam_init)


def _rope_tables(n):
    rows = n // GRID_W
    row = jnp.repeat(jnp.arange(rows, dtype=F32), GRID_W)
    col = jnp.tile(jnp.arange(GRID_W, dtype=F32), rows)
    inv = ROPE_BASE ** (-jnp.arange(ROPE_PAIRS_PER_AXIS, dtype=F32) / ROPE_PAIRS_PER_AXIS)
    ang = jnp.concatenate([row[:, None] * inv, col[:, None] * inv], -1)
    cos, sin = jnp.cos(ang), jnp.sin(ang)
    return jnp.tile(cos, (1, 4)), jnp.tile(jnp.concatenate([-sin, sin], -1), (1, 2))


def _attention(q2d, k2d, v2d, da_lambda, subln_g, batch, seq_len, layer, ctx=None):
    has_ctx = ctx is not None
    w = 2 * DA_HEAD_DIM
    heads = DA_HEADS if seq_len * DA_HEADS <= ATTN_MAX_ROWS else 1
    head = lambda b, h: (b, h)
    const = lambda b, h: (0, 0)
    in_specs = [pl.BlockSpec((seq_len, heads * w), head)] * 3
    args = [q2d, k2d, v2d]
    if has_ctx:
        kc, vc = ctx
        past = kc.shape[1]
        in_specs += [pl.BlockSpec((None, past, heads * w), lambda b, h: (b, 0, h))] * 2
        in_specs += [pl.BlockSpec((seq_len, w), const)] * 2
        args += [kc, vc, *_rope_tables(seq_len)]
    in_specs += [pl.BlockSpec((4, DA_HEAD_DIM), const), pl.BlockSpec((1, w), const)]
    args += [da_lambda, subln_g]
    lam_init = 0.8 - 0.6 * math.exp(-0.3 * layer)
    return pl.pallas_call(
        functools.partial(_attn_kernel, has_ctx=has_ctx, seq_len=seq_len, heads=heads,
                          lam_init=lam_init),
        grid=(batch, DA_HEADS // heads),
        in_specs=in_specs,
        out_specs=pl.BlockSpec((seq_len, heads * w), head),
        out_shape=jax.ShapeDtypeStruct((batch * seq_len, DA_WIDTH), F32),
        compiler_params=pltpu.CompilerParams(
            dimension_semantics=("parallel", "parallel"), vmem_limit_bytes=VMEM_LIMIT),
        name="attn",
    )(*args)


PAIR = 2 * RW_HEAD_DIM
N_PAIRS = RW_HEADS // 2
FINISH_ROWS = 2 * CHUNK
RWKV_GROUP_ROWS = 1024

_M_STRICT = (0, 2)
_M_INCL = (1, 3)
_M_BLOCK8, _M_OFF16, _M_OFF32, _M_OFF64, _M_EYE = 4, 5, 6, 7, 8


def _pair_masks():
    t = jnp.arange(PAIR)[:, None]
    s = jnp.arange(PAIR)[None, :]
    same = lambda n: (t // n) == (s // n)
    head = same(CHUNK)
    masks = [head & (t > s), head & (t >= s), head & (t < s), head & (t <= s),
             same(8), same(16) & ~same(8), same(32) & ~same(16), head & ~same(32), t == s]
    return jnp.stack(masks).astype(BF16)


def _cumsum_matrices():
    t = jnp.arange(CHUNK)[:, None]
    s = jnp.arange(CHUNK)[None, :]
    return jnp.stack([t >= s, t <= s]).astype(BF16)


def _pair_sum_matrix():
    i = jnp.arange(PAIR)
    return ((i[:, None] // RW_HEAD_DIM) == (i[None, :] // RW_HEAD_DIM)).astype(BF16)


def _split2(x):
    hi = x.astype(BF16)
    lo = (x - hi.astype(F32)).astype(BF16)
    return hi, lo


def _head_sums(x, e_ref):
    rows = x.shape[0]
    xs = jnp.concatenate([x[:, p * PAIR:(p + 1) * PAIR] for p in range(N_PAIRS)], axis=0)
    s = _dot(xs.astype(BF16), e_ref[...])
    return jnp.concatenate([s[p * rows:(p + 1) * rows] for p in range(N_PAIRS)], axis=1)


def _run_schedule(order, **stages):
    for name in order:
        if name in stages:
            next(stages[name], None)
    for gen in stages.values():
        for _ in gen:
            pass


_STEP_ORDER = "B B B B C B A B C B A B C B A B".split()


def _rwkv_kernel(*refs, seq_len, group, has_state_in, has_state_out):
    refs = list(refs)
    rw_ref = refs.pop(0)
    s0_ref = refs.pop(0) if has_state_in else None
    (w0_ref, wup_ref, a0_ref, aup_ref, gup_ref, kk_ref, ka_ref, rk_ref,
     lng_ref, lnb_ref, e_ref, tri_ref, m_ref) = refs[:13]
    refs = refs[13:]
    o_ref = refs.pop(0)
    so_ref = refs.pop(0) if has_state_out else None
    (y_scr, bv_scr, gr_scr, s_scr, sio_scr, ar_scr, bk_scr, v_scr, ge_scr,
     t_scr, wy_scr, rl_scr) = refs

    C = CHUNK
    N = RW_HEAD_DIM
    W = RW_WIDTH
    P = PAIR
    nc = seq_len // C
    n_steps = group * nc
    total_rows = group * seq_len
    assert nc & (nc - 1) == 0 and n_steps % 2 == 0
    mm = lambda x, y: _dot(x.astype(BF16), y.astype(BF16))
    mmb = lambda x, y: _dot(x, y).astype(BF16)
    zeros_nn = jnp.zeros((N, N), F32)
    chains = [(d, p) for d in range(2) for p in range(N_PAIRS)]
    n = range(len(chains))

    if has_state_in:
        for g in range(group):
            for d, p in chains:
                top = jnp.concatenate([s0_ref[g, d, 2 * p], zeros_nn], axis=1)
                bot = jnp.concatenate([zeros_nn, s0_ref[g, d, 2 * p + 1]], axis=1)
                sio_scr[g, d, p] = jnp.concatenate([top, bot], axis=0)
    for d, p in chains:
        s_scr[d, p] = jnp.zeros((P, P), F32)

    for scr in (ar_scr, bk_scr, v_scr):
        scr[...] = jnp.zeros(scr.shape, scr.dtype)

    def put_block_diag(scr, slot, d, row0, x):
        xb = x.astype(BF16)
        for p in range(N_PAIRS):
            for hh in range(2):
                lanes = slice(hh * N, (hh + 1) * N)
                scr[slot, d, p, row0 + hh * C:row0 + (hh + 1) * C, lanes] = (
                    xb[:, p * P + hh * N:p * P + (hh + 1) * N])

    def step_chunks(i):
        g = i // nc
        il = i % nc
        return g, il, (il, nc - 1 - il)

    def prep_stages(i, slot):
        g, _, cs = step_chunks(i)
        st = []
        for d, c in enumerate(cs):
            r0 = pl.multiple_of(g * seq_len + c * C, C)
            rows = pl.ds(r0, C)
            xm = rw_ref[rows, :]
            w_lo = xm[:, 3 * W:3 * W + DECAY_LORA]
            a_lo = xm[:, 3 * W + DECAY_LORA:3 * W + DECAY_LORA + AAA_LORA]
            st.append(dict(rows=rows, r=xm[:, 0:W], kr=xm[:, W:2 * W], vr=xm[:, 2 * W:3 * W],
                           g_lo=xm[:, 3 * W + DECAY_LORA + AAA_LORA:RW_COLS],
                           w_up=_dot(jnp.tanh(w_lo).astype(BF16), wup_ref[d]),
                           a_up=_dot(a_lo.astype(BF16), aup_ref[d])))
        yield
        for d, x in enumerate(st):
            x['log_decay'] = -math.exp(-0.5) * jax.nn.sigmoid(w0_ref[d:d + 1, :] + x['w_up'])
            x['a'] = jax.nn.sigmoid(a0_ref[d:d + 1, :] + x['a_up'])
            x['keff'] = x['kr'] * (1.0 + (x['a'] - 1.0) * ka_ref[...])
            x['kk'] = x['kr'] * kk_ref[...]
            x['sums'] = _head_sums(
                jnp.concatenate([x['kk'] * x['kk'], x['r'] * x['keff'] * rk_ref[...]], axis=0), e_ref)
            if d == 0:
                gr_scr[x['rows'], :] = _dot(jax.nn.sigmoid(x['g_lo']).astype(BF16), gup_ref[...])
            cum = _dot(tri_ref[d], jnp.concatenate(_split2(x['log_decay']), axis=1))
            x['cum'] = cum[:, 0:W] + cum[:, W:2 * W]
        yield
        for d, x in enumerate(st):
            kk = x['kk'] / jnp.maximum(jnp.sqrt(x['sums'][0:C]), 1e-12)
            bv_scr[d, x['rows'], :] = x['sums'][C:2 * C] * x['vr']
            cum = x['cum']
            g_in = jnp.exp(cum)
            g_inv = jnp.exp(-cum)
            a_t = -kk * jnp.exp(cum - x['log_decay'])
            r_t = x['r'] * g_in
            b_t = kk * x['a'] * g_inv
            k_t = x['keff'] * g_inv
            ge_scr[slot, d] = g_in[C - 1:C, :] if d == 0 else g_in[0:1, :]
            put_block_diag(ar_scr, slot, d, 0, a_t)
            put_block_diag(ar_scr, slot, d, P, r_t)
            put_block_diag(bk_scr, slot, d, 0, b_t)
            put_block_diag(bk_scr, slot, d, P, k_t)
            put_block_diag(v_scr, slot, d, 0, x['vr'])
        yield

    def local_stages(slot):
        ar = [ar_scr[slot, d, p] for d, p in chains]
        bk = [bk_scr[slot, d, p] for d, p in chains]
        vb = [v_scr[slot, d, p] for d, p in chains]
        strict = [m_ref[_M_STRICT[d]] for d, _ in chains]
        incl = [m_ref[_M_INCL[d]] for d, _ in chains]
        x = [_dot(ar[i], bk[i], _NT).astype(BF16) for i in n]
        yield
        amat = [x[i][0:P, 0:P] * strict[i] for i in n]
        a8 = [a * m_ref[_M_BLOCK8] for a in amat]
        a2 = [mmb(a, a) for a in a8]
        for i, (d, p) in enumerate(chains):
            rl_scr[slot, d, p] = _dot(x[i][0:P, P:2 * P] * strict[i], vb[i])
            wy_scr[slot, d, p] = jnp.concatenate(
                [x[i][P:2 * P, 0:P] * incl[i], x[i][P:2 * P, P:2 * P] * incl[i]], axis=1)
        yield
        a4 = [mmb(a, a) for a in a2]
        ps = [m_ref[_M_EYE] + a for a in a8]
        ps = [t + mmb(t, a) for t, a in zip(ps, a2)]
        yield
        ps = [t + mmb(t, a) for t, a in zip(ps, a4)]
        yield
        for off in (_M_OFF16, _M_OFF32, _M_OFF64):
            ts = [mmb(t, a * m_ref[off]) for t, a in zip(ps, amat)]
            yield
            ps = [t + mmb(q, t) for t, q in zip(ps, ts)]
            if off == _M_OFF64:
                for i, (d, p) in enumerate(chains):
                    t_scr[slot, d, p] = ps[i]
            yield

    def state_stages(i, slot):
        g, il, cs = step_chunks(i)
        ar = [ar_scr[slot, d, p] for d, p in chains]
        bk = [bk_scr[slot, d, p] for d, p in chains]
        vb = [v_scr[slot, d, p] for d, p in chains]
        g_end = [ge_scr[slot, d] for d in range(2)]
        if has_state_in:
            s_old = [jnp.where(il == 0, sio_scr[g, d, p], s_scr[d, p]) for d, p in chains]
        else:
            s_old = [jnp.where(il == 0, 0.0, s_scr[d, p]) for d, p in chains]
        xs = [_dot(ar[i], s_old[i].astype(BF16), _NT) for i in n]
        yield
        u = [mm(t_scr[slot, d, p], xs[i][0:P] + rl_scr[slot, d, p]).astype(BF16)
             for i, (d, p) in enumerate(chains)]
        yield
        uv = [jnp.concatenate([u[i], vb[i]], axis=0) for i in n]
        y = [xs[i][P:2 * P] + _dot(wy_scr[slot, d, p], uv[i]) for i, (d, p) in enumerate(chains)]
        s_new = [s_old[i] + _dot(uv[i], bk[i], _TN) for i in n]
        for i, (d, p) in enumerate(chains):
            sl = slice(p * P, (p + 1) * P)
            rows = pl.ds(pl.multiple_of(g * seq_len + cs[d] * C, C), C)
            y_scr[d, rows, sl] = y[i][0:C] + y[i][C:2 * C]
            s_end = s_new[i] * g_end[d][:, sl]
            s_scr[d, p] = s_end
            if has_state_out:
                sio_scr[g, d, p] = s_end
        yield

    for _ in prep_stages(0, 0):
        pass
    _run_schedule("B A B A B A".split(), A=prep_stages(1, 1), B=local_stages(0))

    def scan_body(j, carry):
        i = 2 * j
        _run_schedule(_STEP_ORDER, C=state_stages(i, 0), B=local_stages(1), A=prep_stages(i + 2, 0))
        _run_schedule(_STEP_ORDER, C=state_stages(i + 1, 1), B=local_stages(0), A=prep_stages(i + 3, 1))
        return carry

    lax.fori_loop(0, n_steps // 2 - 1, scan_body, 0)

    def finish_stages(starts):
        for row0 in starts:
            rows = pl.ds(row0, FINISH_ROWS)
            y = y_scr[0, rows, :] + y_scr[1, rows, :]
            yc = y - _head_sums(y, e_ref) * (1.0 / N)
            var = _head_sums(yc * yc, e_ref) * (1.0 / N)
            yn = yc * lax.rsqrt(var + RW_LNX_EPS) * lng_ref[...] + lnb_ref[...]
            o_ref[rows, :] = (yn + bv_scr[0, rows, :] + bv_scr[1, rows, :]) * gr_scr[rows, :]
            yield

    assert FINISH_ROWS == 2 * C and seq_len % FINISH_ROWS == 0
    last_seq = (group - 1) * seq_len
    late = sorted({last_seq, last_seq + seq_len - FINISH_ROWS})
    early = [r for r in range(0, total_rows, FINISH_ROWS) if r not in late]
    _run_schedule(_STEP_ORDER, C=state_stages(jnp.int32(n_steps - 2), 0), B=local_stages(1),
                  A=finish_stages(early[:len(early) // 2]))
    _run_schedule(_STEP_ORDER, C=state_stages(jnp.int32(n_steps - 1), 1),
                  A=finish_stages(early[len(early) // 2:]))
    for _ in finish_stages(late):
        pass
    if has_state_out:
        for g in range(group):
            for d, p in chains:
                s_pair = sio_scr[g, d, p]
                so_ref[g, d, 2 * p] = s_pair[0:N, 0:N]
                so_ref[g, d, 2 * p + 1] = s_pair[N:P, N:P]


def _rwkv(rw2d, p, batch, seq_len, state_in=None, want_state=False):
    has_state_in = state_in is not None
    assert not (has_state_in and want_state)
    group = max(1, RWKV_GROUP_ROWS // seq_len)
    assert batch % group == 0
    rows = group * seq_len
    const2 = lambda b: (0, 0)
    const3 = lambda b: (0, 0, 0)
    state_spec = pl.BlockSpec((group, 2, RW_HEADS, RW_HEAD_DIM, RW_HEAD_DIM), lambda b: (b, 0, 0, 0, 0))
    in_specs = [pl.BlockSpec((rows, RW_COLS), lambda b: (b, 0))]
    args = [rw2d]
    if has_state_in:
        in_specs.append(state_spec)
        args.append(state_in)
    in_specs += [
        pl.BlockSpec((2, RW_WIDTH), const2),
        pl.BlockSpec((2, DECAY_LORA, RW_WIDTH), const3),
        pl.BlockSpec((2, RW_WIDTH), const2),
        pl.BlockSpec((2, AAA_LORA, RW_WIDTH), const3),
        pl.BlockSpec((GATE_LORA, RW_WIDTH), const2),
        pl.BlockSpec((1, RW_WIDTH), const2),
        pl.BlockSpec((1, RW_WIDTH), const2),
        pl.BlockSpec((1, RW_WIDTH), const2),
        pl.BlockSpec((1, RW_WIDTH), const2),
        pl.BlockSpec((1, RW_WIDTH), const2),
        pl.BlockSpec((PAIR, PAIR), const2),
        pl.BlockSpec((2, CHUNK, CHUNK), const3),
        pl.BlockSpec((9, PAIR, PAIR), const3),
    ]
    args += [p['rw_w0'], p['rw_w_up'].astype(BF16), p['rw_a0'], p['rw_a_up'].astype(BF16),
             p['rw_g_up'].astype(BF16), p['rw_k_k'][None], p['rw_k_a'][None],
             p['rw_r_k'].reshape(1, RW_WIDTH), p['rw_lnx_g'][None], p['rw_lnx_b'][None],
             _pair_sum_matrix(), _cumsum_matrices(), _pair_masks()]
    out_specs = [pl.BlockSpec((rows, RW_WIDTH), lambda b: (b, 0))]
    out_shape = [jax.ShapeDtypeStruct((batch * seq_len, RW_WIDTH), F32)]
    if want_state:
        out_specs.append(state_spec)
        out_shape.append(jax.ShapeDtypeStruct((batch, 2, RW_HEADS, RW_HEAD_DIM, RW_HEAD_DIM), F32))
    per_chain = (2, 2, N_PAIRS)
    outs = pl.pallas_call(
        functools.partial(_rwkv_kernel, seq_len=seq_len, group=group, has_state_in=has_state_in,
                          has_state_out=want_state),
        grid=(batch // group,),
        in_specs=in_specs,
        out_specs=out_specs,
        out_shape=out_shape,
        scratch_shapes=[
            pltpu.VMEM((2, rows, RW_WIDTH), F32),
            pltpu.VMEM((2, rows, RW_WIDTH), F32),
            pltpu.VMEM((rows, RW_WIDTH), F32),
            pltpu.VMEM((2, N_PAIRS, PAIR, PAIR), F32),
            pltpu.VMEM((group, 2, N_PAIRS, PAIR, PAIR), F32),
            pltpu.VMEM(per_chain + (2 * PAIR, PAIR), BF16),
            pltpu.VMEM(per_chain + (2 * PAIR, PAIR), BF16),
            pltpu.VMEM(per_chain + (PAIR, PAIR), BF16),
            pltpu.VMEM((2, 2, 1, RW_WIDTH), F32),
            pltpu.VMEM(per_chain + (PAIR, PAIR), BF16),
            pltpu.VMEM(per_chain + (PAIR, 2 * PAIR), BF16),
            pltpu.VMEM(per_chain + (PAIR, PAIR), F32),
        ],
        compiler_params=pltpu.CompilerParams(
            dimension_semantics=("parallel",), vmem_limit_bytes=VMEM_LIMIT),
        name="rwkv",
    )(*args)
    return outs if want_state else (outs[0], None)


def _merge_kernel(x_ref, mod_ref, oa_ref, yg_ref, win_ref, woa_ref, wor_ref, wout_ref,
                  g_ref, b_ref, o_ref):
    mod = mod_ref[...]
    sh1 = mod[:, 0:D_MODEL]
    sc1 = mod[:, D_MODEL:2 * D_MODEL]
    g1 = mod[:, 2 * D_MODEL:3 * D_MODEL]
    x = x_ref[...]
    h = (x * (1.0 + sc1) + sh1).astype(BF16)
    gates = jax.nn.sigmoid(_dot(h, win_ref[:, QKVR_COLS:N_IN]))
    att = _dot(oa_ref[...].astype(BF16), woa_ref[...])
    rwk = _dot(yg_ref[...].astype(BF16), wor_ref[...])
    merged = gates[:, 0:D_MODEL] * att + gates[:, D_MODEL:2 * D_MODEL] * rwk
    mix = _dot(merged.astype(BF16), wout_ref[...])
    o_ref[...] = _layer_norm(ALPHA * x + g1 * mix, g_ref[...], b_ref[...], LN_EPS)


def _merge(x2d, mod3, o_att, yg, w_in, p, seq_len, fixed_row):
    m = x2d.shape[0]
    row = lambda i: (i, 0)
    const = lambda i: (0, 0)
    return pl.pallas_call(
        _merge_kernel,
        grid=(m // ROW_TILE,),
        in_specs=[
            pl.BlockSpec((ROW_TILE, D_MODEL), row),
            pl.BlockSpec((None, 1, 6 * D_MODEL), _mod_row_map(seq_len, fixed_row)),
            pl.BlockSpec((ROW_TILE, DA_WIDTH), row),
            pl.BlockSpec((ROW_TILE, RW_WIDTH), row),
            pl.BlockSpec((D_MODEL, N_IN), const, pipeline_mode=pl.Buffered(1)),
            pl.BlockSpec((DA_WIDTH, D_MODEL), const),
            pl.BlockSpec((RW_WIDTH, D_MODEL), const),
            pl.BlockSpec((D_MODEL, D_MODEL), const),
            pl.BlockSpec((1, D_MODEL), const),
            pl.BlockSpec((1, D_MODEL), const),
        ],
        out_specs=pl.BlockSpec((ROW_TILE, D_MODEL), row),
        out_shape=jax.ShapeDtypeStruct((m, D_MODEL), F32),
        compiler_params=pltpu.CompilerParams(
            dimension_semantics=("parallel",), vmem_limit_bytes=VMEM_LIMIT),
        name="merge",
    )(x2d, mod3, o_att, yg, w_in, p['w_o_attn'].astype(BF16), p['w_o_rwkv'].astype(BF16),
      p['w_out'].astype(BF16), p['ln1_g'][None], p['ln1_b'][None])


def _mlp_kernel(x_ref, mod_ref, wup_ref, cw_ref, cb_ref, wd_ref, g_ref, b_ref, o_ref,
                h_scr, act_scr, *, seq_len):
    mod = mod_ref[...]
    sh2 = mod[:, 3 * D_MODEL:4 * D_MODEL]
    sc2 = mod[:, 4 * D_MODEL:5 * D_MODEL]
    g2 = mod[:, 5 * D_MODEL:6 * D_MODEL]
    h_scr[...] = (x_ref[...] * (1.0 + sc2) + sh2).astype(BF16)
    rows = x_ref.shape[0]
    pos = lax.broadcasted_iota(jnp.int32, (rows, 1), 0) & (seq_len - 1)
    first = pos == 0
    last = pos == seq_len - 1
    for j in range(D_FF // FF_TILE):
        cols = slice(j * FF_TILE, (j + 1) * FF_TILE)
        h = h_scr[...]
        u = _dot(h, wup_ref[:, cols])
        val = _dot(h, wup_ref[:, D_FF + j * FF_TILE:D_FF + (j + 1) * FF_TILE])
        prev = jnp.where(first, 0.0, pltpu.roll(u, 1, 0))
        nxt = jnp.where(last, 0.0, pltpu.roll(u, rows - 1, 0))
        cw = cw_ref[:, cols]
        u = prev * cw[0:1, :] + u * cw[1:2, :] + nxt * cw[2:3, :] + cb_ref[:, cols]
        act_scr[:, cols] = (jax.nn.gelu(u) * val).astype(BF16)
    f = _dot(act_scr[...], wd_ref[...])
    o_ref[...] = _layer_norm(ALPHA * x_ref[...] + g2 * f, g_ref[...], b_ref[...], LN_EPS)


def _mlp(x2d, mod3, p, seq_len, fixed_row):
    m = x2d.shape[0]
    assert seq_len & (seq_len - 1) == 0 and MLP_ROW_TILE % seq_len == 0
    if fixed_row is not None:
        mod_map = lambda i: (fixed_row, 0, 0)
    else:
        mod_map = lambda i: (i * MLP_ROW_TILE // seq_len, 0, 0)
    row = lambda i: (i, 0)
    const = lambda i: (0, 0)
    resident = pl.Buffered(1)
    return pl.pallas_call(
        functools.partial(_mlp_kernel, seq_len=seq_len),
        grid=(m // MLP_ROW_TILE,),
        in_specs=[
            pl.BlockSpec((MLP_ROW_TILE, D_MODEL), row),
            pl.BlockSpec((None, 1, 6 * D_MODEL), mod_map),
            pl.BlockSpec((D_MODEL, 2 * D_FF), const, pipeline_mode=resident),
            pl.BlockSpec((3, D_FF), const),
            pl.BlockSpec((1, D_FF), const),
            pl.BlockSpec((D_FF, D_MODEL), const, pipeline_mode=resident),
            pl.BlockSpec((1, D_MODEL), const),
            pl.BlockSpec((1, D_MODEL), const),
        ],
        out_specs=pl.BlockSpec((MLP_ROW_TILE, D_MODEL), row),
        out_shape=jax.ShapeDtypeStruct((m, D_MODEL), F32),
        scratch_shapes=[
            pltpu.VMEM((MLP_ROW_TILE, D_MODEL), BF16),
            pltpu.VMEM((MLP_ROW_TILE, D_FF), BF16),
        ],
        compiler_params=pltpu.CompilerParams(
            dimension_semantics=("parallel",), vmem_limit_bytes=MLP_VMEM_LIMIT),
        name="mlp",
    )(x2d, mod3, p['w_up'].astype(BF16), p['conv_w'], p['conv_b'][None], p['w_down'].astype(BF16),
      p['ln2_g'][None], p['ln2_b'][None])


def _trunk_layer(x, mod3, fixed_row, p, w_in, layer, ctx=None):
    batch, seq_len, _ = x.shape
    x2d = x.reshape(batch * seq_len, D_MODEL)
    q, k, v, rw = _input_projection(x2d, mod3, w_in, p['rw_mu'], seq_len, fixed_row)
    if ctx is None:
        o_att = _attention(q, k, v, p['da_lambda'], p['da_subln_g'][None], batch, seq_len, layer)
        yg, state = _rwkv(rw, p, batch, seq_len, want_state=True)
    else:
        k_ctx, v_ctx, s_ctx = ctx
        past = k_ctx.shape[1]
        o_att = _attention(q, k, v, p['da_lambda'], p['da_subln_g'][None], batch, seq_len, layer,
                           ctx=(k_ctx.reshape(batch, past, DA_WIDTH), v_ctx.reshape(batch, past, DA_WIDTH)))
        yg, state = _rwkv(rw, p, batch, seq_len, state_in=s_ctx)
    x1 = _merge(x2d, mod3, o_att, yg, w_in, p, seq_len, fixed_row)
    y = _mlp(x1, mod3, p, seq_len, fixed_row)
    new_ctx = None
    if ctx is None:
        new_ctx = (k.reshape(batch, seq_len, DA_HEADS, 2, DA_HEAD_DIM),
                   v.reshape(batch, seq_len, DA_HEADS, 2 * DA_HEAD_DIM), state)
    return y.reshape(batch, seq_len, D_MODEL), new_ctx


def kernel(x_prompt, x_sample, cache_k, cache_v, state_rwkv, c, c_ctx, w_ada, b_ada, w_in, rw_mu, rw_w0, rw_w_up, rw_a0, rw_a_up, rw_g_up, rw_k_k, rw_k_a, rw_r_k, rw_lnx_g, rw_lnx_b, da_lambda, da_subln_g, w_o_attn, w_o_rwkv, w_out, ln1_g, ln1_b, w_up, conv_w, conv_b, w_down, ln2_g, ln2_b):
    dec_batch = x_sample.shape[0]
    assert dec_batch < MOD_ROWS
    y_prompt, y_sample = x_prompt, x_sample
    new_k, new_v, new_s = [], [], []
    for l in range(DEPTH):
        p = {
            'rw_mu': rw_mu[l], 'rw_w0': rw_w0[l], 'rw_w_up': rw_w_up[l], 'rw_a0': rw_a0[l],
            'rw_a_up': rw_a_up[l], 'rw_g_up': rw_g_up[l], 'rw_k_k': rw_k_k[l], 'rw_k_a': rw_k_a[l],
            'rw_r_k': rw_r_k[l], 'rw_lnx_g': rw_lnx_g[l], 'rw_lnx_b': rw_lnx_b[l],
            'da_lambda': da_lambda[l], 'da_subln_g': da_subln_g[l], 'w_o_attn': w_o_attn[l],
            'w_o_rwkv': w_o_rwkv[l], 'w_out': w_out[l], 'ln1_g': ln1_g[l], 'ln1_b': ln1_b[l],
            'w_up': w_up[l], 'conv_w': conv_w[l], 'conv_b': conv_b[l], 'w_down': w_down[l],
            'ln2_g': ln2_g[l], 'ln2_b': ln2_b[l],
        }
        cvec = jnp.concatenate(
            [c, c_ctx[None], jnp.zeros((MOD_ROWS - dec_batch - 1, D_MODEL), F32)], axis=0)
        mod3 = _modulation(cvec, w_ada[l], b_ada[l][None]).reshape(MOD_ROWS, 1, 6 * D_MODEL)
        w_in_l = w_in[l].astype(BF16)
        y_prompt, ctx_l = _trunk_layer(y_prompt, mod3, dec_batch, p, w_in_l, l)
        new_k.append(ctx_l[0])
        new_v.append(ctx_l[1])
        new_s.append(ctx_l[2])
        y_sample, _ = _trunk_layer(y_sample, mod3, None, p, w_in_l, l,
                                   ctx=(cache_k[:, l], cache_v[:, l], state_rwkv[:, l]))
    return (y_prompt, y_sample, jnp.stack(new_k, axis=1), jnp.stack(new_v, axis=1),
            jnp.stack(new_s, axis=1))
```

```python
import functools
import math

import jax
import jax.numpy as jnp
from jax import lax
from jax.experimental import pallas as pl
from jax.experimental.pallas import tpu as pltpu

F32 = jnp.float32
BF16 = jnp.bfloat16
HIGHEST = lax.Precision.HIGHEST

D_MODEL = 1024
GRID_W = 64
DA_HEADS = 4
DA_HEAD_DIM = 64
DA_WIDTH = DA_HEADS * 2 * DA_HEAD_DIM
ROPE_PAIRS_PER_AXIS = DA_HEAD_DIM // 4
ROPE_BASE = 10000.0
RW_HEADS = 8
RW_HEAD_DIM = 64
RW_WIDTH = RW_HEADS * RW_HEAD_DIM
DECAY_LORA = 64
AAA_LORA = 64
GATE_LORA = 128
RW_COLS = 3 * RW_WIDTH + DECAY_LORA + AAA_LORA + GATE_LORA
RW_LNX_EPS = 64e-5
QKVR_COLS = 3 * DA_WIDTH + RW_COLS
N_IN = QKVR_COLS + 2 * D_MODEL
D_FF = 2816
LN_EPS = 1e-5
DEPTH = 1
ALPHA = (2.0 * DEPTH) ** 0.25
LOG2_E = math.log2(math.e)

CHUNK = 64
ATTN_Q_BLOCK = 256
ATTN_MAX_ROWS = 4096
ATTN_MAX_UNITS_AHEAD = 3
ATTN_PENDING_SCORE_BYTES = 3 * 1024 * 1024
ROW_TILE = 512
SHIFT_COL_TILE = 256
MLP_ROW_TILE = 1024
FF_TILE = 256
MOD_COL_TILE = 768
MOD_ROWS = 16
VMEM_LIMIT = 48 * 1024 * 1024
MLP_VMEM_LIMIT = 56 * 1024 * 1024

_NN = (((1,), (0,)), ((), ()))
_NT = (((1,), (1,)), ((), ()))
_TN = (((0,), (0,)), ((), ()))


def _dot(a, b, dims=_NN, precision=None):
    return lax.dot_general(a, b, dims, precision=precision, preferred_element_type=F32)


def _layer_norm(z, g, b, eps):
    mu = jnp.mean(z, axis=-1, keepdims=True)
    zc = z - mu
    var = jnp.mean(zc * zc, axis=-1, keepdims=True)
    return zc * lax.rsqrt(var + eps) * g + b


def _mod_kernel(c_ref, w_ref, b_ref, o_ref):
    cv = c_ref[...]
    s = cv * jax.nn.sigmoid(cv)
    o_ref[...] = _dot(s, w_ref[...], precision=HIGHEST) + b_ref[...]


def _modulation(cvec, w_ada, b_ada):
    n = w_ada.shape[1]
    return pl.pallas_call(
        _mod_kernel,
        grid=(n // MOD_COL_TILE,),
        in_specs=[
            pl.BlockSpec((MOD_ROWS, D_MODEL), lambda j: (0, 0)),
            pl.BlockSpec((D_MODEL, MOD_COL_TILE), lambda j: (0, j)),
            pl.BlockSpec((1, MOD_COL_TILE), lambda j: (0, j)),
        ],
        out_specs=pl.BlockSpec((MOD_ROWS, MOD_COL_TILE), lambda j: (0, j)),
        out_shape=jax.ShapeDtypeStruct((MOD_ROWS, n), F32),
        compiler_params=pltpu.CompilerParams(
            dimension_semantics=("parallel",), vmem_limit_bytes=VMEM_LIMIT),
        name="mod",
    )(cvec, w_ada, b_ada)


def _mod_row_map(rows_per_batch, fixed_row):
    if fixed_row is not None:
        return lambda i: (fixed_row, 0, 0)
    tiles = rows_per_batch // ROW_TILE
    return lambda i: (i // tiles, 0, 0)


def _inproj_kernel(x_ref, mod_ref, w_ref, mu_ref, q_ref, k_ref, v_ref, rw_ref, *, seq_len):
    mod = mod_ref[...]
    sh1 = mod[:, 0:D_MODEL]
    sc1 = mod[:, D_MODEL:2 * D_MODEL]
    h = (x_ref[...] * (1.0 + sc1) + sh1).astype(BF16)
    q_ref[...] = _dot(h, w_ref[:, 0:DA_WIDTH])
    k_ref[...] = _dot(h, w_ref[:, DA_WIDTH:2 * DA_WIDTH])
    v_ref[...] = _dot(h, w_ref[:, 2 * DA_WIDTH:3 * DA_WIDTH])
    rows = x_ref.shape[0]
    pos = lax.broadcasted_iota(jnp.int32, (rows, 1), 0) & (seq_len - 1)
    first = pos == 0
    last = pos == seq_len - 1
    for j in range(RW_COLS // SHIFT_COL_TILE):
        cols = slice(j * SHIFT_COL_TILE, (j + 1) * SHIFT_COL_TILE)
        rw = _dot(h, w_ref[:, 3 * DA_WIDTH + j * SHIFT_COL_TILE:3 * DA_WIDTH + (j + 1) * SHIFT_COL_TILE])
        prev = jnp.where(first, 0.0, pltpu.roll(rw, 1, 0))
        nxt = jnp.where(last, 0.0, pltpu.roll(rw, rows - 1, 0))
        rw_ref[:, cols] = rw + mu_ref[0:1, cols] * (prev - rw) + mu_ref[1:2, cols] * (nxt - rw)


def _input_projection(x2d, mod3, w_in, rw_mu, seq_len, fixed_row):
    m = x2d.shape[0]
    tile = max(seq_len, ROW_TILE)
    assert seq_len & (seq_len - 1) == 0 and tile % seq_len == 0
    if fixed_row is not None:
        mod_map = lambda i: (fixed_row, 0, 0)
    else:
        mod_map = lambda i: (i * tile // seq_len, 0, 0)
    row = lambda i: (i, 0)
    return pl.pallas_call(
        functools.partial(_inproj_kernel, seq_len=seq_len),
        grid=(m // tile,),
        in_specs=[
            pl.BlockSpec((tile, D_MODEL), row),
            pl.BlockSpec((None, 1, 6 * D_MODEL), mod_map),
            pl.BlockSpec((D_MODEL, N_IN), lambda i: (0, 0), pipeline_mode=pl.Buffered(1)),
            pl.BlockSpec((2, RW_COLS), lambda i: (0, 0)),
        ],
        out_specs=[
            pl.BlockSpec((tile, DA_WIDTH), row),
            pl.BlockSpec((tile, DA_WIDTH), row),
            pl.BlockSpec((tile, DA_WIDTH), row),
            pl.BlockSpec((tile, RW_COLS), row),
        ],
        out_shape=[
            jax.ShapeDtypeStruct((m, DA_WIDTH), F32),
            jax.ShapeDtypeStruct((m, DA_WIDTH), F32),
            jax.ShapeDtypeStruct((m, DA_WIDTH), F32),
            jax.ShapeDtypeStruct((m, RW_COLS), F32),
        ],
        compiler_params=pltpu.CompilerParams(
            dimension_semantics=("parallel",), vmem_limit_bytes=VMEM_LIMIT),
        name="inproj",
    )(x2d, mod3, w_in, rw_mu)


def _rope(x, cos, sin_signed):
    lane = lax.broadcasted_iota(jnp.int32, x.shape, 1)
    partner = jnp.where((lane & 63) < 32, pltpu.roll(x, 96, 1), pltpu.roll(x, 32, 1))
    return x * cos + partner * sin_signed


def _attn_kernel(*refs, has_ctx, seq_len, heads, lam_init):
    if has_ctx:
        q_ref, k_ref, v_ref, kc_ref, vc_ref, cos_ref, sin_ref, lq_ref, g_ref, o_ref = refs
    else:
        q_ref, k_ref, v_ref, lq_ref, g_ref, o_ref = refs
    d = DA_HEAD_DIM
    qrows = min(ATTN_Q_BLOCK, seq_len)
    lq = lq_ref[...]
    lam = (jnp.exp(jnp.sum(lq[0:1] * lq[1:2], axis=-1, keepdims=True))
           - jnp.exp(jnp.sum(lq[2:3] * lq[3:4], axis=-1, keepdims=True)) + lam_init)

    def with_ones(v):
        return jnp.concatenate([v.astype(BF16), jnp.ones(v.shape, BF16)], axis=1)

    keys, vals = {}, {}

    def head_operands(h):
        if h not in keys:
            lanes = slice(h * 2 * d, (h + 1) * 2 * d)
            k = k_ref[:, lanes]
            if has_ctx:
                k = _rope(k, cos_ref[...], sin_ref[...])
            keys[h] = [[k[:, m * d:(m + 1) * d].astype(BF16)] for m in range(2)]
            vals[h] = [with_ones(v_ref[:, lanes])]
            if has_ctx:
                kc = kc_ref[:, lanes]
                for m in range(2):
                    keys[h][m].append(kc[:, m * d:(m + 1) * d].astype(BF16))
                vals[h].append(with_ones(vc_ref[:, lanes]))
        return keys[h], vals[h]

    g = g_ref[...]

    def scores(h, qb, m):
        rows = slice(qb * qrows, (qb + 1) * qrows)
        q = q_ref[rows, h * 2 * d:(h + 1) * 2 * d]
        if has_ctx:
            q = _rope(q, cos_ref[rows, :], sin_ref[rows, :])
        qm = (q[:, m * d:(m + 1) * d] * (d ** -0.5 * LOG2_E)).astype(BF16)
        return [_dot(qm, kg, _NT) for kg in head_operands(h)[0][m]]

    def attend(h, ss):
        mx = ss[0].max(axis=-1, keepdims=True)
        for s in ss[1:]:
            mx = jnp.maximum(mx, s.max(axis=-1, keepdims=True))
        acc = None
        for s, vg in zip(ss, head_operands(h)[1]):
            o = _dot(jnp.exp2(s - mx).astype(BF16), vg)
            acc = o if acc is None else acc + o
        return acc[:, 0:2 * d] / acc[:, 2 * d:4 * d]

    units = [(h, qb, m) for h in range(heads) for qb in range(seq_len // qrows)
             for m in range(2)]
    n_keys = seq_len + (kc_ref.shape[0] if has_ctx else 0)
    ahead = max(1, min(ATTN_MAX_UNITS_AHEAD, ATTN_PENDING_SCORE_BYTES // (qrows * n_keys * 4)))
    pending = [scores(*u) for u in units[:ahead]]
    o1 = None
    for i, (h, qb, m) in enumerate(units):
        if i + ahead < len(units):
            pending.append(scores(*units[i + ahead]))
        o = attend(h, pending.pop(0))
        if m == 0:
            o1 = o
            continue
        o = o1 - lam * o
        ms = jnp.mean(o * o, axis=-1, keepdims=True)
        rows = slice(qb * qrows, (qb + 1) * qrows)
        o_ref[rows, h * 2 * d:(h + 1) * 2 * d] = o * lax.rsqrt(ms + LN_EPS) * g * (1.0 - lam_init)


def _rope_tables(n):
    rows = n // GRID_W
    row = jnp.repeat(jnp.arange(rows, dtype=F32), GRID_W)
    col = jnp.tile(jnp.arange(GRID_W, dtype=F32), rows)
    inv = ROPE_BASE ** (-jnp.arange(ROPE_PAIRS_PER_AXIS, dtype=F32) / ROPE_PAIRS_PER_AXIS)
    ang = jnp.concatenate([row[:, None] * inv, col[:, None] * inv], -1)
    cos, sin = jnp.cos(ang), jnp.sin(ang)
    return jnp.tile(cos, (1, 4)), jnp.tile(jnp.concatenate([-sin, sin], -1), (1, 2))


def _attention(q2d, k2d, v2d, da_lambda, subln_g, batch, seq_len, layer, ctx=None):
    has_ctx = ctx is not None
    w = 2 * DA_HEAD_DIM
    heads = DA_HEADS if seq_len * DA_HEADS <= ATTN_MAX_ROWS else 1
    head = lambda b, h: (b, h)
    const = lambda b, h: (0, 0)
    in_specs = [pl.BlockSpec((seq_len, heads * w), head)] * 3
    args = [q2d, k2d, v2d]
    if has_ctx:
        kc, vc = ctx
        past = kc.shape[1]
        in_specs += [pl.BlockSpec((None, past, heads * w), lambda b, h: (b, 0, h))] * 2
        in_specs += [pl.BlockSpec((seq_len, w), const)] * 2
        args += [kc, vc, *_rope_tables(seq_len)]
    in_specs += [pl.BlockSpec((4, DA_HEAD_DIM), const), pl.BlockSpec((1, w), const)]
    args += [da_lambda, subln_g]
    lam_init = 0.8 - 0.6 * math.exp(-0.3 * layer)
    return pl.pallas_call(
        functools.partial(_attn_kernel, has_ctx=has_ctx, seq_len=seq_len, heads=heads,
                          lam_init=lam_init),
        grid=(batch, DA_HEADS // heads),
        in_specs=in_specs,
        out_specs=pl.BlockSpec((seq_len, heads * w), head),
        out_shape=jax.ShapeDtypeStruct((batch * seq_len, DA_WIDTH), F32),
        compiler_params=pltpu.CompilerParams(
            dimension_semantics=("parallel", "parallel"), vmem_limit_bytes=VMEM_LIMIT),
        name="attn",
    )(*args)


PAIR = 2 * RW_HEAD_DIM
N_PAIRS = RW_HEADS // 2
FINISH_ROWS = 2 * CHUNK
RWKV_GROUP_ROWS = 1024

_M_STRICT = (0, 2)
_M_INCL = (1, 3)
_M_BLOCK8, _M_OFF16, _M_OFF32, _M_OFF64, _M_EYE = 4, 5, 6, 7, 8


def _pair_masks():
    t = jnp.arange(PAIR)[:, None]
    s = jnp.arange(PAIR)[None, :]
    same = lambda n: (t // n) == (s // n)
    head = same(CHUNK)
    masks = [head & (t > s), head & (t >= s), head & (t < s), head & (t <= s),
             same(8), same(16) & ~same(8), same(32) & ~same(16), head & ~same(32), t == s]
    return jnp.stack(masks).astype(BF16)


def _cumsum_matrices():
    t = jnp.arange(CHUNK)[:, None]
    s = jnp.arange(CHUNK)[None, :]
    return jnp.stack([t >= s, t <= s]).astype(BF16)


def _pair_sum_matrix():
    i = jnp.arange(PAIR)
    return ((i[:, None] // RW_HEAD_DIM) == (i[None, :] // RW_HEAD_DIM)).astype(BF16)


def _split2(x):
    hi = x.astype(BF16)
    lo = (x - hi.astype(F32)).astype(BF16)
    return hi, lo


def _head_sums(x, e_ref):
    rows = x.shape[0]
    xs = jnp.concatenate([x[:, p * PAIR:(p + 1) * PAIR] for p in range(N_PAIRS)], axis=0)
    s = _dot(xs.astype(BF16), e_ref[...])
    return jnp.concatenate([s[p * rows:(p + 1) * rows] for p in range(N_PAIRS)], axis=1)


def _run_schedule(order, **stages):
    for name in order:
        if name in stages:
            next(stages[name], None)
    for gen in stages.values():
        for _ in gen:
            pass


_STEP_ORDER = "B B B B C B A B C B A B C B A B".split()


def _rwkv_kernel(*refs, seq_len, group, has_state_in, has_state_out):
    refs = list(refs)
    rw_ref = refs.pop(0)
    s0_ref = refs.pop(0) if has_state_in else None
    (w0_ref, wup_ref, a0_ref, aup_ref, gup_ref, kk_ref, ka_ref, rk_ref,
     lng_ref, lnb_ref, e_ref, tri_ref, m_ref) = refs[:13]
    refs = refs[13:]
    o_ref = refs.pop(0)
    so_ref = refs.pop(0) if has_state_out else None
    (y_scr, bv_scr, gr_scr, s_scr, sio_scr, ar_scr, bk_scr, v_scr, ge_scr,
     t_scr, wy_scr, rl_scr) = refs

    C = CHUNK
    N = RW_HEAD_DIM
    W = RW_WIDTH
    P = PAIR
    nc = seq_len // C
    n_steps = group * nc
    total_rows = group * seq_len
    assert nc & (nc - 1) == 0 and n_steps % 2 == 0
    mm = lambda x, y: _dot(x.astype(BF16), y.astype(BF16))
    mmb = lambda x, y: _dot(x, y).astype(BF16)
    zeros_nn = jnp.zeros((N, N), F32)
    chains = [(d, p) for d in range(2) for p in range(N_PAIRS)]
    n = range(len(chains))

    if has_state_in:
        for g in range(group):
            for d, p in chains:
                top = jnp.concatenate([s0_ref[g, d, 2 * p], zeros_nn], axis=1)
                bot = jnp.concatenate([zeros_nn, s0_ref[g, d, 2 * p + 1]], axis=1)
                sio_scr[g, d, p] = jnp.concatenate([top, bot], axis=0)
    for d, p in chains:
        s_scr[d, p] = jnp.zeros((P, P), F32)

    for scr in (ar_scr, bk_scr, v_scr):
        scr[...] = jnp.zeros(scr.shape, scr.dtype)

    def put_block_diag(scr, slot, d, row0, x):
        xb = x.astype(BF16)
        for p in range(N_PAIRS):
            for hh in range(2):
                lanes = slice(hh * N, (hh + 1) * N)
                scr[slot, d, p, row0 + hh * C:row0 + (hh + 1) * C, lanes] = (
                    xb[:, p * P + hh * N:p * P + (hh + 1) * N])

    def step_chunks(i):
        g = i // nc
        il = i % nc
        return g, il, (il, nc - 1 - il)

    def prep_stages(i, slot):
        g, _, cs = step_chunks(i)
        st = []
        for d, c in enumerate(cs):
            r0 = pl.multiple_of(g * seq_len + c * C, C)
            rows = pl.ds(r0, C)
            xm = rw_ref[rows, :]
            w_lo = xm[:, 3 * W:3 * W + DECAY_LORA]
            a_lo = xm[:, 3 * W + DECAY_LORA:3 * W + DECAY_LORA + AAA_LORA]
            st.append(dict(rows=rows, r=xm[:, 0:W], kr=xm[:, W:2 * W], vr=xm[:, 2 * W:3 * W],
                           g_lo=xm[:, 3 * W + DECAY_LORA + AAA_LORA:RW_COLS],
                           w_up=_dot(jnp.tanh(w_lo).astype(BF16), wup_ref[d]),
                           a_up=_dot(a_lo.astype(BF16), aup_ref[d])))
        yield
        for d, x in enumerate(st):
            x['log_decay'] = -math.exp(-0.5) * jax.nn.sigmoid(w0_ref[d:d + 1, :] + x['w_up'])
            x['a'] = jax.nn.sigmoid(a0_ref[d:d + 1, :] + x['a_up'])
            x['keff'] = x['kr'] * (1.0 + (x['a'] - 1.0) * ka_ref[...])
            x['kk'] = x['kr'] * kk_ref[...]
            x['sums'] = _head_sums(
                jnp.concatenate([x['kk'] * x['kk'], x['r'] * x['keff'] * rk_ref[...]], axis=0), e_ref)
            if d == 0:
                gr_scr[x['rows'], :] = _dot(jax.nn.sigmoid(x['g_lo']).astype(BF16), gup_ref[...])
            cum = _dot(tri_ref[d], jnp.concatenate(_split2(x['log_decay']), axis=1))
            x['cum'] = cum[:, 0:W] + cum[:, W:2 * W]
        yield
        for d, x in enumerate(st):
            kk = x['kk'] / jnp.maximum(jnp.sqrt(x['sums'][0:C]), 1e-12)
            bv_scr[d, x['rows'], :] = x['sums'][C:2 * C] * x['vr']
            cum = x['cum']
            g_in = jnp.exp(cum)
            g_inv = jnp.exp(-cum)
            a_t = -kk * jnp.exp(cum - x['log_decay'])
            r_t = x['r'] * g_in
            b_t = kk * x['a'] * g_inv
            k_t = x['keff'] * g_inv
            ge_scr[slot, d] = g_in[C - 1:C, :] if d == 0 else g_in[0:1, :]
            put_block_diag(ar_scr, slot, d, 0, a_t)
            put_block_diag(ar_scr, slot, d, P, r_t)
            put_block_diag(bk_scr, slot, d, 0, b_t)
            put_block_diag(bk_scr, slot, d, P, k_t)
            put_block_diag(v_scr, slot, d, 0, x['vr'])
        yield

    def local_stages(slot):
        ar = [ar_scr[slot, d, p] for d, p in chains]
        bk = [bk_scr[slot, d, p] for d, p in chains]
        vb = [v_scr[slot, d, p] for d, p in chains]
        strict = [m_ref[_M_STRICT[d]] for d, _ in chains]
        incl = [m_ref[_M_INCL[d]] for d, _ in chains]
        x = [_dot(ar[i], bk[i], _NT).astype(BF16) for i in n]
        yield
        amat = [x[i][0:P, 0:P] * strict[i] for i in n]
        a8 = [a * m_ref[_M_BLOCK8] for a in amat]
        a2 = [mmb(a, a) for a in a8]
        for i, (d, p) in enumerate(chains):
            rl_scr[slot, d, p] = _dot(x[i][0:P, P:2 * P] * strict[i], vb[i])
            wy_scr[slot, d, p] = jnp.concatenate(
                [x[i][P:2 * P, 0:P] * incl[i], x[i][P:2 * P, P:2 * P] * incl[i]], axis=1)
        yield
        a4 = [mmb(a, a) for a in a2]
        ps = [m_ref[_M_EYE] + a for a in a8]
        ps = [t + mmb(t, a) for t, a in zip(ps, a2)]
        yield
        ps = [t + mmb(t, a) for t, a in zip(ps, a4)]
        yield
        for off in (_M_OFF16, _M_OFF32, _M_OFF64):
            ts = [mmb(t, a * m_ref[off]) for t, a in zip(ps, amat)]
            yield
            ps = [t + mmb(q, t) for t, q in zip(ps, ts)]
            if off == _M_OFF64:
                for i, (d, p) in enumerate(chains):
                    t_scr[slot, d, p] = ps[i]
            yield

    def state_stages(i, slot):
        g, il, cs = step_chunks(i)
        ar = [ar_scr[slot, d, p] for d, p in chains]
        bk = [bk_scr[slot, d, p] for d, p in chains]
        vb = [v_scr[slot, d, p] for d, p in chains]
        g_end = [ge_scr[slot, d] for d in range(2)]
        if has_state_in:
            s_old = [jnp.where(il == 0, sio_scr[g, d, p], s_scr[d, p]) for d, p in chains]
        else:
            s_old = [jnp.where(il == 0, 0.0, s_scr[d, p]) for d, p in chains]
        xs = [_dot(ar[i], s_old[i].astype(BF16), _NT) for i in n]
        yield
        u = [mm(t_scr[slot, d, p], xs[i][0:P] + rl_scr[slot, d, p]).astype(BF16)
             for i, (d, p) in enumerate(chains)]
        yield
        uv = [jnp.concatenate([u[i], vb[i]], axis=0) for i in n]
        y = [xs[i][P:2 * P] + _dot(wy_scr[slot, d, p], uv[i]) for i, (d, p) in enumerate(chains)]
        s_new = [s_old[i] + _dot(uv[i], bk[i], _TN) for i in n]
        for i, (d, p) in enumerate(chains):
            sl = slice(p * P, (p + 1) * P)
            rows = pl.ds(pl.multiple_of(g * seq_len + cs[d] * C, C), C)
            y_scr[d, rows, sl] = y[i][0:C] + y[i][C:2 * C]
            s_end = s_new[i] * g_end[d][:, sl]
            s_scr[d, p] = s_end
            if has_state_out:
                sio_scr[g, d, p] = s_end
        yield

    for _ in prep_stages(0, 0):
        pass
    _run_schedule("B A B A B A".split(), A=prep_stages(1, 1), B=local_stages(0))

    def scan_body(j, carry):
        i = 2 * j
        _run_schedule(_STEP_ORDER, C=state_stages(i, 0), B=local_stages(1), A=prep_stages(i + 2, 0))
        _run_schedule(_STEP_ORDER, C=state_stages(i + 1, 1), B=local_stages(0), A=prep_stages(i + 3, 1))
        return carry

    lax.fori_loop(0, n_steps // 2 - 1, scan_body, 0)

    def finish_stages(starts):
        for row0 in starts:
            rows = pl.ds(row0, FINISH_ROWS)
            y = y_scr[0, rows, :] + y_scr[1, rows, :]
            yc = y - _head_sums(y, e_ref) * (1.0 / N)
            var = _head_sums(yc * yc, e_ref) * (1.0 / N)
            yn = yc * lax.rsqrt(var + RW_LNX_EPS) * lng_ref[...] + lnb_ref[...]
            o_ref[rows, :] = (yn + bv_scr[0, rows, :] + bv_scr[1, rows, :]) * gr_scr[rows, :]
            yield

    assert FINISH_ROWS == 2 * C and seq_len % FINISH_ROWS == 0
    last_seq = (group - 1) * seq_len
    late = sorted({last_seq, last_seq + seq_len - FINISH_ROWS})
    early = [r for r in range(0, total_rows, FINISH_ROWS) if r not in late]
    _run_schedule(_STEP_ORDER, C=state_stages(jnp.int32(n_steps - 2), 0), B=local_stages(1),
                  A=finish_stages(early[:len(early) // 2]))
    _run_schedule(_STEP_ORDER, C=state_stages(jnp.int32(n_steps - 1), 1),
                  A=finish_stages(early[len(early) // 2:]))
    for _ in finish_stages(late):
        pass
    if has_state_out:
        for g in range(group):
            for d, p in chains:
                s_pair = sio_scr[g, d, p]
                so_ref[g, d, 2 * p] = s_pair[0:N, 0:N]
                so_ref[g, d, 2 * p + 1] = s_pair[N:P, N:P]


def _rwkv(rw2d, p, batch, seq_len, state_in=None, want_state=False):
    has_state_in = state_in is not None
    assert not (has_state_in and want_state)
    group = max(1, RWKV_GROUP_ROWS // seq_len)
    assert batch % group == 0
    rows = group * seq_len
    const2 = lambda b: (0, 0)
    const3 = lambda b: (0, 0, 0)
    state_spec = pl.BlockSpec((group, 2, RW_HEADS, RW_HEAD_DIM, RW_HEAD_DIM), lambda b: (b, 0, 0, 0, 0))
    in_specs = [pl.BlockSpec((rows, RW_COLS), lambda b: (b, 0))]
    args = [rw2d]
    if has_state_in:
        in_specs.append(state_spec)
        args.append(state_in)
    in_specs += [
        pl.BlockSpec((2, RW_WIDTH), const2),
        pl.BlockSpec((2, DECAY_LORA, RW_WIDTH), const3),
        pl.BlockSpec((2, RW_WIDTH), const2),
        pl.BlockSpec((2, AAA_LORA, RW_WIDTH), const3),
        pl.BlockSpec((GATE_LORA, RW_WIDTH), const2),
        pl.BlockSpec((1, RW_WIDTH), const2),
        pl.BlockSpec((1, RW_WIDTH), const2),
        pl.BlockSpec((1, RW_WIDTH), const2),
        pl.BlockSpec((1, RW_WIDTH), const2),
        pl.BlockSpec((1, RW_WIDTH), const2),
        pl.BlockSpec((PAIR, PAIR), const2),
        pl.BlockSpec((2, CHUNK, CHUNK), const3),
        pl.BlockSpec((9, PAIR, PAIR), const3),
    ]
    args += [p['rw_w0'], p['rw_w_up'].astype(BF16), p['rw_a0'], p['rw_a_up'].astype(BF16),
             p['rw_g_up'].astype(BF16), p['rw_k_k'][None], p['rw_k_a'][None],
             p['rw_r_k'].reshape(1, RW_WIDTH), p['rw_lnx_g'][None], p['rw_lnx_b'][None],
             _pair_sum_matrix(), _cumsum_matrices(), _pair_masks()]
    out_specs = [pl.BlockSpec((rows, RW_WIDTH), lambda b: (b, 0))]
    out_shape = [jax.ShapeDtypeStruct((batch * seq_len, RW_WIDTH), F32)]
    if want_state:
        out_specs.append(state_spec)
        out_shape.append(jax.ShapeDtypeStruct((batch, 2, RW_HEADS, RW_HEAD_DIM, RW_HEAD_DIM), F32))
    per_chain = (2, 2, N_PAIRS)
    outs = pl.pallas_call(
        functools.partial(_rwkv_kernel, seq_len=seq_len, group=group, has_state_in=has_state_in,
                          has_state_out=want_state),
        grid=(batch // group,),
        in_specs=in_specs,
        out_specs=out_specs,
        out_shape=out_shape,
        scratch_shapes=[
            pltpu.VMEM((2, rows, RW_WIDTH), F32),
            pltpu.VMEM((2, rows, RW_WIDTH), F32),
            pltpu.VMEM((rows, RW_WIDTH), F32),
            pltpu.VMEM((2, N_PAIRS, PAIR, PAIR), F32),
            pltpu.VMEM((group, 2, N_PAIRS, PAIR, PAIR), F32),
            pltpu.VMEM(per_chain + (2 * PAIR, PAIR), BF16),
            pltpu.VMEM(per_chain + (2 * PAIR, PAIR), BF16),
            pltpu.VMEM(per_chain + (PAIR, PAIR), BF16),
            pltpu.VMEM((2, 2, 1, RW_WIDTH), F32),
            pltpu.VMEM(per_chain + (PAIR, PAIR), BF16),
            pltpu.VMEM(per_chain + (PAIR, 2 * PAIR), BF16),
            pltpu.VMEM(per_chain + (PAIR, PAIR), F32),
        ],
        compiler_params=pltpu.CompilerParams(
            dimension_semantics=("parallel",), vmem_limit_bytes=VMEM_LIMIT),
        name="rwkv",
    )(*args)
    return outs if want_state else (outs[0], None)


def _merge_kernel(x_ref, mod_ref, oa_ref, yg_ref, win_ref, woa_ref, wor_ref, wout_ref,
                  g_ref, b_ref, o_ref):
    mod = mod_ref[...]
    sh1 = mod[:, 0:D_MODEL]
    sc1 = mod[:, D_MODEL:2 * D_MODEL]
    g1 = mod[:, 2 * D_MODEL:3 * D_MODEL]
    x = x_ref[...]
    h = (x * (1.0 + sc1) + sh1).astype(BF16)
    gates = jax.nn.sigmoid(_dot(h, win_ref[:, QKVR_COLS:N_IN]))
    att = _dot(oa_ref[...].astype(BF16), woa_ref[...])
    rwk = _dot(yg_ref[...].astype(BF16), wor_ref[...])
    merged = gates[:, 0:D_MODEL] * att + gates[:, D_MODEL:2 * D_MODEL] * rwk
    mix = _dot(merged.astype(BF16), wout_ref[...])
    o_ref[...] = _layer_norm(ALPHA * x + g1 * mix, g_ref[...], b_ref[...], LN_EPS)


def _merge(x2d, mod3, o_att, yg, w_in, p, seq_len, fixed_row):
    m = x2d.shape[0]
    row = lambda i: (i, 0)
    const = lambda i: (0, 0)
    return pl.pallas_call(
        _merge_kernel,
        grid=(m // ROW_TILE,),
        in_specs=[
            pl.BlockSpec((ROW_TILE, D_MODEL), row),
            pl.BlockSpec((None, 1, 6 * D_MODEL), _mod_row_map(seq_len, fixed_row)),
            pl.BlockSpec((ROW_TILE, DA_WIDTH), row),
            pl.BlockSpec((ROW_TILE, RW_WIDTH), row),
            pl.BlockSpec((D_MODEL, N_IN), const, pipeline_mode=pl.Buffered(1)),
            pl.BlockSpec((DA_WIDTH, D_MODEL), const),
            pl.BlockSpec((RW_WIDTH, D_MODEL), const),
            pl.BlockSpec((D_MODEL, D_MODEL), const),
            pl.BlockSpec((1, D_MODEL), const),
            pl.BlockSpec((1, D_MODEL), const),
        ],
        out_specs=pl.BlockSpec((ROW_TILE, D_MODEL), row),
        out_shape=jax.ShapeDtypeStruct((m, D_MODEL), F32),
        compiler_params=pltpu.CompilerParams(
            dimension_semantics=("parallel",), vmem_limit_bytes=VMEM_LIMIT),
        name="merge",
    )(x2d, mod3, o_att, yg, w_in, p['w_o_attn'].astype(BF16), p['w_o_rwkv'].astype(BF16),
      p['w_out'].astype(BF16), p['ln1_g'][None], p['ln1_b'][None])


def _mlp_kernel(x_ref, mod_ref, wup_ref, cw_ref, cb_ref, wd_ref, g_ref, b_ref, o_ref,
                h_scr, act_scr, *, seq_len):
    mod = mod_ref[...]
    sh2 = mod[:, 3 * D_MODEL:4 * D_MODEL]
    sc2 = mod[:, 4 * D_MODEL:5 * D_MODEL]
    g2 = mod[:, 5 * D_MODEL:6 * D_MODEL]
    h_scr[...] = (x_ref[...] * (1.0 + sc2) + sh2).astype(BF16)
    rows = x_ref.shape[0]
    pos = lax.broadcasted_iota(jnp.int32, (rows, 1), 0) & (seq_len - 1)
    first = pos == 0
    last = pos == seq_len - 1
    for j in range(D_FF // FF_TILE):
        cols = slice(j * FF_TILE, (j + 1) * FF_TILE)
        h = h_scr[...]
        u = _dot(h, wup_ref[:, cols])
        val = _dot(h, wup_ref[:, D_FF + j * FF_TILE:D_FF + (j + 1) * FF_TILE])
        prev = jnp.where(first, 0.0, pltpu.roll(u, 1, 0))
        nxt = jnp.where(last, 0.0, pltpu.roll(u, rows - 1, 0))
        cw = cw_ref[:, cols]
        u = prev * cw[0:1, :] + u * cw[1:2, :] + nxt * cw[2:3, :] + cb_ref[:, cols]
        act_scr[:, cols] = (jax.nn.gelu(u) * val).astype(BF16)
    f = _dot(act_scr[...], wd_ref[...])
    o_ref[...] = _layer_norm(ALPHA * x_ref[...] + g2 * f, g_ref[...], b_ref[...], LN_EPS)


def _mlp(x2d, mod3, p, seq_len, fixed_row):
    m = x2d.shape[0]
    assert seq_len & (seq_len - 1) == 0 and MLP_ROW_TILE % seq_len == 0
    if fixed_row is not None:
        mod_map = lambda i: (fixed_row, 0, 0)
    else:
        mod_map = lambda i: (i * MLP_ROW_TILE // seq_len, 0, 0)
    row = lambda i: (i, 0)
    const = lambda i: (0, 0)
    resident = pl.Buffered(1)
    return pl.pallas_call(
        functools.partial(_mlp_kernel, seq_len=seq_len),
        grid=(m // MLP_ROW_TILE,),
        in_specs=[
            pl.BlockSpec((MLP_ROW_TILE, D_MODEL), row),
            pl.BlockSpec((None, 1, 6 * D_MODEL), mod_map),
            pl.BlockSpec((D_MODEL, 2 * D_FF), const, pipeline_mode=resident),
            pl.BlockSpec((3, D_FF), const),
            pl.BlockSpec((1, D_FF), const),
            pl.BlockSpec((D_FF, D_MODEL), const, pipeline_mode=resident),
            pl.BlockSpec((1, D_MODEL), const),
            pl.BlockSpec((1, D_MODEL), const),
        ],
        out_specs=pl.BlockSpec((MLP_ROW_TILE, D_MODEL), row),
        out_shape=jax.ShapeDtypeStruct((m, D_MODEL), F32),
        scratch_shapes=[
            pltpu.VMEM((MLP_ROW_TILE, D_MODEL), BF16),
            pltpu.VMEM((MLP_ROW_TILE, D_FF), BF16),
        ],
        compiler_params=pltpu.CompilerParams(
            dimension_semantics=("parallel",), vmem_limit_bytes=MLP_VMEM_LIMIT),
        name="mlp",
    )(x2d, mod3, p['w_up'].astype(BF16), p['conv_w'], p['conv_b'][None], p['w_down'].astype(BF16),
      p['ln2_g'][None], p['ln2_b'][None])


def _trunk_layer(x, mod3, fixed_row, p, w_in, layer, ctx=None):
    batch, seq_len, _ = x.shape
    x2d = x.reshape(batch * seq_len, D_MODEL)
    q, k, v, rw = _input_projection(x2d, mod3, w_in, p['rw_mu'], seq_len, fixed_row)
    if ctx is None:
        o_att = _attention(q, k, v, p['da_lambda'], p['da_subln_g'][None], batch, seq_len, layer)
        yg, state = _rwkv(rw, p, batch, seq_len, want_state=True)
    else:
        k_ctx, v_ctx, s_ctx = ctx
        past = k_ctx.shape[1]
        o_att = _attention(q, k, v, p['da_lambda'], p['da_subln_g'][None], batch, seq_len, layer,
                           ctx=(k_ctx.reshape(batch, past, DA_WIDTH), v_ctx.reshape(batch, past, DA_WIDTH)))
        yg, state = _rwkv(rw, p, batch, seq_len, state_in=s_ctx)
    x1 = _merge(x2d, mod3, o_att, yg, w_in, p, seq_len, fixed_row)
    y = _mlp(x1, mod3, p, seq_len, fixed_row)
    new_ctx = None
    if ctx is None:
        new_ctx = (k.reshape(batch, seq_len, DA_HEADS, 2, DA_HEAD_DIM),
                   v.reshape(batch, seq_len, DA_HEADS, 2 * DA_HEAD_DIM), state)
    return y.reshape(batch, seq_len, D_MODEL), new_ctx


def kernel(x_prompt, x_sample, cache_k, cache_v, state_rwkv, c, c_ctx, w_ada, b_ada, w_in, rw_mu, rw_w0, rw_w_up, rw_a0, rw_a_up, rw_g_up, rw_k_k, rw_k_a, rw_r_k, rw_lnx_g, rw_lnx_b, da_lambda, da_subln_g, w_o_attn, w_o_rwkv, w_out, ln1_g, ln1_b, w_up, conv_w, conv_b, w_down, ln2_g, ln2_b):
    dec_batch = x_sample.shape[0]
    assert dec_batch < MOD_ROWS
    y_prompt, y_sample = x_prompt, x_sample
    new_k, new_v, new_s = [], [], []
    for l in range(DEPTH):
        p = {
            'rw_mu': rw_mu[l], 'rw_w0': rw_w0[l], 'rw_w_up': rw_w_up[l], 'rw_a0': rw_a0[l],
            'rw_a_up': rw_a_up[l], 'rw_g_up': rw_g_up[l], 'rw_k_k': rw_k_k[l], 'rw_k_a': rw_k_a[l],
            'rw_r_k': rw_r_k[l], 'rw_lnx_g': rw_lnx_g[l], 'rw_lnx_b': rw_lnx_b[l],
            'da_lambda': da_lambda[l], 'da_subln_g': da_subln_g[l], 'w_o_attn': w_o_attn[l],
            'w_o_rwkv': w_o_rwkv[l], 'w_out': w_out[l], 'ln1_g': ln1_g[l], 'ln1_b': ln1_b[l],
            'w_up': w_up[l], 'conv_w': conv_w[l], 'conv_b': conv_b[l], 'w_down': w_down[l],
            'ln2_g': ln2_g[l], 'ln2_b': ln2_b[l],
        }
        cvec = jnp.concatenate(
            [c, c_ctx[None], jnp.zeros((MOD_ROWS - dec_batch - 1, D_MODEL), F32)], axis=0)
        mod3 = _modulation(cvec, w_ada[l], b_ada[l][None]).reshape(MOD_ROWS, 1, 6 * D_MODEL)
        w_in_l = w_in[l].astype(BF16)
        y_prompt, ctx_l = _trunk_layer(y_prompt, mod3, dec_batch, p, w_in_l, l)
        new_k.append(ctx_l[0])
        new_v.append(ctx_l[1])
        new_s.append(ctx_l[2])
        y_sample, _ = _trunk_layer(y_sample, mod3, None, p, w_in_l, l,
                                   ctx=(cache_k[:, l], cache_v[:, l], state_rwkv[:, l]))
    return (y_prompt, y_sample, jnp.stack(new_k, axis=1), jnp.stack(new_v, axis=1),
            jnp.stack(new_s, axis=1))
```
